```python
import jax, jax.numpy as jnp
from jax import lax
import numpy as np

D_MODEL = 2048
BATCH = 2
SEQ = 4096
DEPTH = 1

CHUNK = 64
Q_BLOCK = 128
PLE_DIM = 256
FOX_HEADS = 8
FOX_HEAD_DIM = D_MODEL // 16
GLA_HEADS = 4
GLA_KEY_DIM = D_MODEL // 16
GLA_VAL_DIM = D_MODEL // 8
GLA_GATE_RANK = 16
GLA_GATE_TAU = 16.0
D_FF = ((8 * D_MODEL // 3 + 255) // 256) * 256
EPS = 1e-6

FOX_W = FOX_HEADS * FOX_HEAD_DIM
GLA_KW = GLA_HEADS * GLA_KEY_DIM
GLA_VW = GLA_HEADS * GLA_VAL_DIM
IN_SPLITS = (FOX_W, FOX_W, FOX_W, FOX_HEADS, GLA_KW, GLA_KW, GLA_VW, GLA_VW, GLA_GATE_RANK)
D_IN = sum(IN_SPLITS)

kernel_name = "hybrid_fox_gla_macaron_ple"


def rms_norm(x, g):
    xf = x.astype(jnp.float32)
    y = xf * lax.rsqrt(jnp.mean(xf * xf, axis=-1, keepdims=True) + EPS)
    return (y * g.astype(jnp.float32)).astype(x.dtype)


def swiglu(x, w_gate, w_up, w_down):
    return (jax.nn.silu(x @ w_gate) * (x @ w_up)) @ w_down


def split_cols(t, sizes):
    out, start = [], 0
    for s in sizes:
        out.append(t[..., start:start + s])
        start += s
    return out


def forgetting_attention(q, k, v, log_f):
    b, s, h, d = q.shape
    nb = s // Q_BLOCK
    f_cum = jnp.cumsum(log_f.astype(jnp.float32), axis=1).transpose(0, 2, 1)
    q_blocks = q.reshape(b, nb, Q_BLOCK, h, d).transpose(1, 0, 2, 3, 4)
    fq_blocks = f_cum.reshape(b, h, nb, Q_BLOCK).transpose(2, 0, 1, 3)
    key_pos = jnp.arange(s)
    scale = d ** -0.5

    def block(args):
        q_i, fq_i, i = args
        logits = jnp.einsum('bqhd,bkhd->bhqk', q_i, k).astype(jnp.float32) * scale
        logits = logits + fq_i[..., :, None] - f_cum[:, :, None, :]
        q_pos = i * Q_BLOCK + jnp.arange(Q_BLOCK)
        causal = key_pos[None, :] <= q_pos[:, None]
        logits = jnp.where(causal, logits, -jnp.inf)
        probs = jax.nn.softmax(logits, axis=-1).astype(v.dtype)
        return jnp.einsum('bhqk,bkhd->bqhd', probs, v)

    out = lax.map(block, (q_blocks, fq_blocks, jnp.arange(nb)))
    return out.transpose(1, 0, 2, 3, 4).reshape(b, s, h * d)


def gla_chunk_causal(q, k, v, log_a):
    b, s, h, dk = q.shape
    dv = v.shape[-1]
    nc = s // CHUNK

    def chunks(t):
        return t.astype(jnp.float32).reshape(b, nc, CHUNK, h, t.shape[-1]).transpose(1, 0, 2, 3, 4)

    qc, kc, vc, ac = chunks(q), chunks(k), chunks(v), chunks(log_a)
    a_cum = jnp.cumsum(ac, axis=2)
    a_tot = a_cum[:, :, -1]
    k_dec = kc * jnp.exp(a_tot[:, :, None] - a_cum)
    qc = qc * (dk ** -0.5)

    def step(state, inp):
        q_c, k_c, v_c, a_c = inp
        state = jnp.exp(a_c)[..., None] * state + jnp.einsum('bchk,bchv->bhkv', k_c, v_c)
        o = jnp.einsum('bchk,bhkv->bchv', q_c, state)
        return state, o

    state0 = jnp.zeros((b, h, dk, dv), jnp.float32)
    _, o = lax.scan(step, state0, (qc, k_dec, vc, a_tot))
    return o.transpose(1, 0, 2, 3, 4).reshape(b, s, h, dv)


def setup_inputs(seed: int = 0) -> dict:
    key = jax.random.key(seed)
    ks = iter(jax.random.split(key, 32))
    f32 = jnp.float32

    def w(shape, fan_in):
        return jax.random.normal(next(ks), (DEPTH,) + shape, f32) * (fan_in ** -0.5)

    def gain(shape):
        return 1.0 + 0.05 * jax.random.normal(next(ks), shape, f32)

    def bias(shape, mean=0.0, std=0.01):
        return mean + std * jax.random.normal(next(ks), shape, f32)

    return {
        "x": jax.random.normal(next(ks), (BATCH, SEQ, D_MODEL), f32),
        "p": jax.random.normal(next(ks), (DEPTH, BATCH, SEQ, PLE_DIM), f32),
        "ffn1_norm": gain((DEPTH, D_MODEL)),
        "ffn1_w_gate": w((D_MODEL, D_FF), D_MODEL),
        "ffn1_w_up": w((D_MODEL, D_FF), D_MODEL),
        "ffn1_w_down": w((D_FF, D_MODEL), D_FF),
        "mix_norm": gain((DEPTH, D_MODEL)),
        "w_in": w((D_MODEL, D_IN), D_MODEL),
        "fox_forget_bias": bias((DEPTH, FOX_HEADS), mean=3.0, std=0.1),
        "gla_gate_up": w((GLA_GATE_RANK, GLA_KW), GLA_GATE_RANK),
        "gla_gate_bias": bias((DEPTH, GLA_KW)),
        "gla_head_norm": gain((DEPTH, GLA_VAL_DIM)),
        "w_branch_fox": w((FOX_W, D_MODEL), FOX_W),
        "w_branch_gla": w((GLA_VW, D_MODEL), GLA_VW),
        "w_merge_gate": w((D_MODEL, 2 * D_MODEL), D_MODEL),
        "b_merge_gate": bias((DEPTH, 2 * D_MODEL)),
        "w_out": w((D_MODEL, D_MODEL), D_MODEL),
        "ffn2_norm": gain((DEPTH, D_MODEL)),
        "ffn2_w_gate": w((D_MODEL, D_FF), D_MODEL),
        "ffn2_w_up": w((D_MODEL, D_FF), D_MODEL),
        "ffn2_w_down": w((D_FF, D_MODEL), D_FF),
        "ple_norm": gain((DEPTH, D_MODEL)),
        "w_ple_proj": w((PLE_DIM, D_MODEL), PLE_DIM),
        "w_ple_gate": w((D_MODEL, D_MODEL), D_MODEL),
        "final_norm": gain((D_MODEL,)),
    }


def reference(x, p, ffn1_norm, ffn1_w_gate, ffn1_w_up, ffn1_w_down, mix_norm, w_in,
              fox_forget_bias, gla_gate_up, gla_gate_bias, gla_head_norm,
              w_branch_fox, w_branch_gla, w_merge_gate, b_merge_gate, w_out,
              ffn2_norm, ffn2_w_gate, ffn2_w_up, ffn2_w_down,
              ple_norm, w_ple_proj, w_ple_gate, final_norm):
    b, s, _ = x.shape
    h = x
    for i in range(DEPTH):
        h = h + 0.5 * swiglu(rms_norm(h, ffn1_norm[i]), ffn1_w_gate[i], ffn1_w_up[i], ffn1_w_down[i])

        u = rms_norm(h, mix_norm[i])
        (fq, fk, fv, f_logit, gq, gk, gv, gr, g_down) = split_cols(u @ w_in[i], IN_SPLITS)

        log_f = jax.nn.log_sigmoid((f_logit + fox_forget_bias[i]).astype(jnp.float32))
        y_fox = forgetting_attention(
            fq.reshape(b, s, FOX_HEADS, FOX_HEAD_DIM),
            fk.reshape(b, s, FOX_HEADS, FOX_HEAD_DIM),
            fv.reshape(b, s, FOX_HEADS, FOX_HEAD_DIM),
            log_f.astype(x.dtype))

        log_a = jax.nn.log_sigmoid((g_down @ gla_gate_up[i] + gla_gate_bias[i]).astype(jnp.float32)) / GLA_GATE_TAU
        o_gla = gla_chunk_causal(
            gq.reshape(b, s, GLA_HEADS, GLA_KEY_DIM),
            gk.reshape(b, s, GLA_HEADS, GLA_KEY_DIM),
            gv.reshape(b, s, GLA_HEADS, GLA_VAL_DIM),
            log_a.reshape(b, s, GLA_HEADS, GLA_KEY_DIM))
        o_gla = rms_norm(o_gla, gla_head_norm[i]).reshape(b, s, GLA_VW).astype(x.dtype)
        y_gla = o_gla * jax.nn.silu(gr)

        gates = jax.nn.sigmoid(u @ w_merge_gate[i] + b_merge_gate[i])
        g_fox, g_gla = gates[..., :D_MODEL], gates[..., D_MODEL:]
        merged = g_fox * (y_fox @ w_branch_fox[i]) + g_gla * (y_gla @ w_branch_gla[i])
        h = h + merged @ w_out[i]

        h = h + 0.5 * swiglu(rms_norm(h, ffn2_norm[i]), ffn2_w_gate[i], ffn2_w_up[i], ffn2_w_down[i])

        ple_gate = jax.nn.sigmoid(rms_norm(h, ple_norm[i]) @ w_ple_gate[i])
        h = h + ple_gate * (p[i].astype(h.dtype) @ w_ple_proj[i])
    return rms_norm(h, final_norm)
```

```python
import functools
import math

import jax
import jax.numpy as jnp
from jax import lax
from jax.experimental import pallas as pl
from jax.experimental.pallas import tpu as pltpu

EPS = 1e-6
CHUNK = 64
GLA_GATE_TAU = 16.0
LANES = 128
V7X_VMEM_LIMIT_BYTES = 60 * 1024 * 1024
LOG2E = math.log2(math.e)

F32 = jnp.float32
BF16 = jnp.bfloat16


def _rms(x, g):
    return x * lax.rsqrt(jnp.mean(x * x, axis=-1, keepdims=True) + EPS) * g


def _log_sigmoid(z):
    return jnp.minimum(z, 0.0) - jnp.log1p(jnp.exp(-jnp.abs(z)))


def _dot(a, b):
    return jnp.dot(a, b, preferred_element_type=F32)


def _params(*sem):
    return pltpu.CompilerParams(dimension_semantics=sem,
                                vmem_limit_bytes=V7X_VMEM_LIMIT_BYTES)


def _ffn_kernel(x_ref, g_ref, wg_ref, wu_ref, wd_ref, o_ref, xn_ref):
    @pl.when(pl.program_id(1) == 0)
    def _():
        x = x_ref[...]
        xn_ref[...] = _rms(x, g_ref[...]).astype(BF16)
        o_ref[...] = x

    xn = xn_ref[...]
    gate = _dot(xn, wg_ref[...])
    up = _dot(xn, wu_ref[...])
    hid = (0.5 * (gate * jax.nn.sigmoid(gate)) * up).astype(BF16)
    o_ref[...] += _dot(hid, wd_ref[...])


def _ffn(x, g, wg, wu, wd, *, tm, tf):
    n, d = x.shape
    f = wg.shape[1]
    return pl.pallas_call(
        _ffn_kernel,
        grid=(n // tm, f // tf),
        in_specs=[
            pl.BlockSpec((tm, d), lambda i, j: (i, 0)),
            pl.BlockSpec((1, d), lambda i, j: (0, 0)),
            pl.BlockSpec((d, tf), lambda i, j: (0, j)),
            pl.BlockSpec((d, tf), lambda i, j: (0, j)),
            pl.BlockSpec((tf, d), lambda i, j: (j, 0)),
        ],
        out_specs=pl.BlockSpec((tm, d), lambda i, j: (i, 0)),
        out_shape=jax.ShapeDtypeStruct((n, d), F32),
        scratch_shapes=[pltpu.VMEM((tm, d), BF16)],
        compiler_params=_params("parallel", "arbitrary"),
        name="ffn",
    )(x, g, wg, wu, wd)


def _proj_kernel(h_ref, g_ref, w_ref, ws_ref, qkv_ref, f32_ref, small_ref, un_ref,
                 *, n_bf16_tiles, gq_tile, gq_scale):
    j = pl.program_id(1)

    @pl.when(j == 0)
    def _():
        un = _rms(h_ref[...], g_ref[...]).astype(BF16)
        un_ref[...] = un
        small_ref[...] = _dot(un, ws_ref[...])

    r = _dot(un_ref[...], w_ref[...])

    @pl.when(j < n_bf16_tiles)
    def _():
        scale = jnp.where(j == gq_tile, gq_scale, 1.0).astype(F32)
        qkv_ref[...] = (r * scale).astype(BF16)

    @pl.when(j >= n_bf16_tiles)
    def _():
        f32_ref[...] = r


def _proj(h, g, w_main, w_small, *, tm, tn, n_bf16_cols, gq_col, gq_scale):
    n, d = h.shape
    n_cols = w_main.shape[1]
    n_tiles = n_cols // tn
    n_bf16_tiles = n_bf16_cols // tn
    kern = functools.partial(_proj_kernel, n_bf16_tiles=n_bf16_tiles,
                             gq_tile=gq_col // tn, gq_scale=gq_scale)
    return pl.pallas_call(
        kern,
        grid=(n // tm, n_tiles),
        in_specs=[
            pl.BlockSpec((tm, d), lambda i, j: (i, 0)),
            pl.BlockSpec((1, d), lambda i, j: (0, 0)),
            pl.BlockSpec((d, tn), lambda i, j: (0, j)),
            pl.BlockSpec((d, LANES), lambda i, j: (0, 0)),
        ],
        out_specs=[
            pl.BlockSpec((tm, tn), lambda i, j: (i, jnp.minimum(j, n_bf16_tiles - 1))),
            pl.BlockSpec((tm, tn), lambda i, j: (i, jnp.maximum(j - n_bf16_tiles, 0))),
            pl.BlockSpec((tm, LANES), lambda i, j: (i, 0)),
        ],
        out_shape=[
            jax.ShapeDtypeStruct((n, n_bf16_cols), BF16),
            jax.ShapeDtypeStruct((n, n_cols - n_bf16_cols), F32),
            jax.ShapeDtypeStruct((n, LANES), F32),
        ],
        scratch_shapes=[pltpu.VMEM((tm, d), BF16)],
        compiler_params=_params("parallel", "arbitrary"),
        name="proj",
    )(h, g, w_main, w_small)


def _fcum_kernel(small_ref, bias_ref, f_ref, c_ref, *, n_heads):
    s = small_ref.shape[0]
    c_ref[...] = _log_sigmoid(small_ref[...] + bias_ref[...])
    row = lax.broadcasted_iota(jnp.int32, (LANES, LANES), 0)
    col = lax.broadcasted_iota(jnp.int32, (LANES, LANES), 1)
    tril = (row >= col).astype(F32)

    def body(r, carry):
        rows = pl.ds(pl.multiple_of(r * LANES, LANES), LANES)
        c = jnp.dot(tril, c_ref[rows, :], precision=lax.Precision.HIGHEST,
                    preferred_element_type=F32) + carry
        c_ref[rows, :] = c
        return c[LANES - 1:LANES, :]

    lax.fori_loop(0, s // LANES, body, jnp.zeros((1, LANES), F32))
    f_ref[0] = c_ref[...].T[:n_heads, :]


def _fcum(small, bias_row, *, batch, seq, n_heads):
    return pl.pallas_call(
        functools.partial(_fcum_kernel, n_heads=n_heads),
        grid=(batch,),
        in_specs=[
            pl.BlockSpec((seq, LANES), lambda b: (b, 0)),
            pl.BlockSpec((1, LANES), lambda b: (0, 0)),
        ],
        out_specs=pl.BlockSpec((1, n_heads, seq), lambda b: (b, 0, 0)),
        out_shape=jax.ShapeDtypeStruct((batch, n_heads, seq), F32),
        scratch_shapes=[pltpu.VMEM((seq, LANES), F32)],
        compiler_params=_params("parallel"),
        name="fcum",
    )(small, bias_row)


def _fox_kernel(q_ref, k_ref, v_ref, f_ref, o_ref, m_ref, l_ref, acc_ref, *, t, c2):
    i = pl.program_id(2)
    q = q_ref[0]
    q0 = pl.multiple_of(i * t, t)
    f_base = f_ref[0, :, pl.ds(q0, t)][:, 0:1]
    m_ref[...] = jnp.full(m_ref.shape, -jnp.inf, F32)
    l_ref[...] = jnp.zeros(l_ref.shape, F32)
    acc_ref[...] = jnp.zeros(acc_ref.shape, F32)
    reps = t // LANES

    def tile(kk, masked):
        k0 = pl.multiple_of(kk * t, t)
        k = k_ref[0, pl.ds(k0, t), :]
        v = v_ref[0, pl.ds(k0, t), :]
        bias = (f_base - f_ref[0, :, pl.ds(k0, t)]) * LOG2E
        s = lax.dot_general(q, k, (((1,), (1,)), ((), ())), preferred_element_type=F32)
        s = s * c2 + bias
        if masked:
            row = lax.broadcasted_iota(jnp.int32, (t, t), 0)
            col = lax.broadcasted_iota(jnp.int32, (t, t), 1)
            s = jnp.where(col <= row, s, -jnp.inf)
        m_old = m_ref[...]
        m_new = jnp.maximum(m_old, jnp.max(s, axis=-1, keepdims=True))
        alpha = jnp.exp2(m_old - m_new)
        p = jnp.exp2(s - jnp.tile(m_new, (1, reps)))
        l_ref[...] = alpha * l_ref[...] + jnp.sum(p, axis=-1, keepdims=True)
        acc_ref[...] = alpha * acc_ref[...] + _dot(p.astype(BF16), v)
        m_ref[...] = m_new

    def body(kk, carry):
        tile(kk, False)
        return carry

    lax.fori_loop(0, i, body, 0)
    tile(i, True)
    o_ref[0] = (acc_ref[...] / l_ref[...]).astype(o_ref.dtype)


def _fox(qkv, fcum, *, batch, seq, n_heads, head_dim, t):
    assert head_dim == LANES
    c2 = (head_dim ** -0.5) * LOG2E
    return pl.pallas_call(
        functools.partial(_fox_kernel, t=t, c2=c2),
        grid=(batch, n_heads, seq // t),
        in_specs=[
            pl.BlockSpec((1, t, head_dim), lambda b, h, i: (b, i, h)),
            pl.BlockSpec((1, seq, head_dim), lambda b, h, i: (b, 0, n_heads + h)),
            pl.BlockSpec((1, seq, head_dim), lambda b, h, i: (b, 0, 2 * n_heads + h)),
            pl.BlockSpec((1, 1, seq), lambda b, h, i: (b * n_heads + h, 0, 0)),
        ],
        out_specs=pl.BlockSpec((1, t, head_dim), lambda b, h, i: (b, i, h)),
        out_shape=jax.ShapeDtypeStruct((batch, seq, n_heads * head_dim), BF16),
        scratch_shapes=[pltpu.VMEM((t, LANES), F32), pltpu.VMEM((t, LANES), F32),
                        pltpu.VMEM((t, head_dim), F32)],
        compiler_params=_params("parallel", "parallel", "arbitrary"),
        name="fox",
    )(qkv, qkv, qkv, fcum)


def _gla_kernel(q_ref, k_ref, v_ref, r_ref, small_ref, wgu_ref, gb_ref, gain_ref, o_ref,
                st_ref, la_ref, kd_ref, ea_ref, *, n_heads, dk, dv):
    tb = q_ref.shape[1]
    n_chunks = tb // CHUNK

    @pl.when(pl.program_id(1) == 0)
    def _():
        st_ref[...] = jnp.zeros(st_ref.shape, F32)

    pre = _dot(small_ref[0].astype(BF16), wgu_ref[...]) + gb_ref[...]
    la_ref[...] = _log_sigmoid(pre) * (1.0 / GLA_GATE_TAU)

    row = lax.broadcasted_iota(jnp.int32, (CHUNK, CHUNK), 0)
    col = lax.broadcasted_iota(jnp.int32, (CHUNK, CHUNK), 1)
    after = (col > row).astype(F32)

    def prep(c, carry):
        rows = pl.ds(pl.multiple_of(c * CHUNK, CHUNK), CHUNK)
        la = la_ref[rows, :]
        rev = jnp.dot(after, la, precision=lax.Precision.HIGHEST, preferred_element_type=F32)
        kd_ref[rows, :] = (k_ref[0, rows, :] * jnp.exp(rev)).astype(BF16)
        ea_ref[c] = jnp.exp(jnp.sum(la, axis=0, keepdims=True))
        return carry

    lax.fori_loop(0, n_chunks, prep, 0)

    def step(c, carry):
        rows = pl.ds(pl.multiple_of(c * CHUNK, CHUNK), CHUNK)
        ea = ea_ref[c]
        for h in range(n_heads):
            ks = slice(h * dk, (h + 1) * dk)
            vs = slice(h * dv, (h + 1) * dv)
            kv_t = lax.dot_general(v_ref[0, rows, vs], kd_ref[rows, ks],
                                   (((0,), (0,)), ((), ())), preferred_element_type=F32)
            st = st_ref[h] * ea[:, ks] + kv_t
            st_ref[h] = st
            o = lax.dot_general(q_ref[0, rows, ks], st.astype(BF16),
                                (((1,), (1,)), ((), ())), preferred_element_type=F32)
            on = _rms(o, gain_ref[...])
            r = r_ref[0, rows, vs]
            o_ref[0, rows, vs] = (on * (r * jax.nn.sigmoid(r))).astype(o_ref.dtype)
        return carry

    lax.fori_loop(0, n_chunks, step, 0)


def _gla(qkv, f32x, small, wgu, gate_bias, gain, *, batch, seq, n_heads, dk, dv, tb,
         v_col, q_col, r_col, k_col):
    kw, vw = n_heads * dk, n_heads * dv
    return pl.pallas_call(
        functools.partial(_gla_kernel, n_heads=n_heads, dk=dk, dv=dv),
        grid=(batch, seq // tb),
        in_specs=[
            pl.BlockSpec((1, tb, kw), lambda b, t: (b, t, q_col // kw)),
            pl.BlockSpec((1, tb, kw), lambda b, t: (b, t, k_col // kw)),
            pl.BlockSpec((1, tb, vw), lambda b, t: (b, t, v_col // vw)),
            pl.BlockSpec((1, tb, vw), lambda b, t: (b, t, r_col // vw)),
            pl.BlockSpec((1, tb, LANES), lambda b, t: (b, t, 0)),
            pl.BlockSpec((LANES, kw), lambda b, t: (0, 0)),
            pl.BlockSpec((1, kw), lambda b, t: (0, 0)),
            pl.BlockSpec((1, dv), lambda b, t: (0, 0)),
        ],
        out_specs=pl.BlockSpec((1, tb, vw), lambda b, t: (b, t, 0)),
        out_shape=jax.ShapeDtypeStruct((batch, seq, vw), BF16),
        scratch_shapes=[
            pltpu.VMEM((n_heads, dv, dk), F32),
            pltpu.VMEM((tb, kw), F32),
            pltpu.VMEM((tb, kw), BF16),
            pltpu.VMEM((tb // CHUNK, 1, kw), F32),
        ],
        compiler_params=_params("parallel", "arbitrary"),
        name="gla",
    )(qkv, f32x, qkv, f32x, small, wgu, gate_bias, gain)


def _merge_kernel(h_ref, g_ref, yf_ref, yg_ref, wgf_ref, wgg_ref, bgf_ref, bgg_ref,
                  wa_ref, wb_ref, wo_ref, o_ref, un_ref):
    @pl.when(pl.program_id(1) == 0)
    def _():
        h = h_ref[...]
        un_ref[...] = _rms(h, g_ref[...]).astype(BF16)
        o_ref[...] = h

    un = un_ref[...]
    g_fox = jax.nn.sigmoid(_dot(un, wgf_ref[...]) + bgf_ref[...])
    g_gla = jax.nn.sigmoid(_dot(un, wgg_ref[...]) + bgg_ref[...])
    merged = g_fox * _dot(yf_ref[...], wa_ref[...]) + g_gla * _dot(yg_ref[...], wb_ref[...])
    o_ref[...] += _dot(merged.astype(BF16), wo_ref[...])


def _merge(h, g, y_fox, y_gla, w_gate, b_gate, w_a, w_b, w_o, *, tm, tn):
    n, d = h.shape
    nj = d // tn
    return pl.pallas_call(
        _merge_kernel,
        grid=(n // tm, nj),
        in_specs=[
            pl.BlockSpec((tm, d), lambda i, j: (i, 0)),
            pl.BlockSpec((1, d), lambda i, j: (0, 0)),
            pl.BlockSpec((tm, y_fox.shape[1]), lambda i, j: (i, 0)),
            pl.BlockSpec((tm, y_gla.shape[1]), lambda i, j: (i, 0)),
            pl.BlockSpec((d, tn), lambda i, j: (0, j)),
            pl.BlockSpec((d, tn), lambda i, j: (0, nj + j)),
            pl.BlockSpec((1, tn), lambda i, j: (0, j)),
            pl.BlockSpec((1, tn), lambda i, j: (0, nj + j)),
            pl.BlockSpec((w_a.shape[0], tn), lambda i, j: (0, j)),
            pl.BlockSpec((w_b.shape[0], tn), lambda i, j: (0, j)),
            pl.BlockSpec((tn, d), lambda i, j: (j, 0)),
        ],
        out_specs=pl.BlockSpec((tm, d), lambda i, j: (i, 0)),
        out_shape=jax.ShapeDtypeStruct((n, d), F32),
        scratch_shapes=[pltpu.VMEM((tm, d), BF16)],
        compiler_params=_params("parallel", "arbitrary"),
        name="merge",
    )(h, g, y_fox, y_gla, w_gate, w_gate, b_gate, b_gate, w_a, w_b, w_o)


def _ple_kernel(h_ref, p_ref, gp_ref, gf_ref, wpg_ref, wpe_ref, o_ref, *, final):
    h = h_ref[...]
    hn = _rms(h, gp_ref[...]).astype(BF16)
    gate = jax.nn.sigmoid(_dot(hn, wpg_ref[...]))
    out = h + gate * _dot(p_ref[...].astype(BF16), wpe_ref[...])
    if final:
        out = _rms(out, gf_ref[...])
    o_ref[...] = out


def _ple(h, p, g_ple, g_final, w_gate, w_proj, *, tm, final):
    n, d = h.shape
    dp = p.shape[1]
    return pl.pallas_call(
        functools.partial(_ple_kernel, final=final),
        grid=(n // tm,),
        in_specs=[
            pl.BlockSpec((tm, d), lambda i: (i, 0)),
            pl.BlockSpec((tm, dp), lambda i: (i, 0)),
            pl.BlockSpec((1, d), lambda i: (0, 0)),
            pl.BlockSpec((1, d), lambda i: (0, 0)),
            pl.BlockSpec((d, d), lambda i: (0, 0)),
            pl.BlockSpec((dp, d), lambda i: (0, 0)),
        ],
        out_specs=pl.BlockSpec((tm, d), lambda i: (i, 0)),
        out_shape=jax.ShapeDtypeStruct((n, d), F32),
        compiler_params=_params("parallel"),
        name="ple",
    )(h, p, g_ple, g_final, w_gate, w_proj)


def _tile(n, want):
    t = min(n, want)
    assert n % t == 0, (n, want)
    return t


def kernel(x, p, ffn1_norm, ffn1_w_gate, ffn1_w_up, ffn1_w_down, mix_norm, w_in, fox_forget_bias, gla_gate_up, gla_gate_bias, gla_head_norm, w_branch_fox, w_branch_gla, w_merge_gate, b_merge_gate, w_out, ffn2_norm, ffn2_w_gate, ffn2_w_up, ffn2_w_down, ple_norm, w_ple_proj, w_ple_gate, final_norm):
    batch, seq, d = x.shape
    depth = p.shape[0]
    n = batch * seq
    fox_heads = fox_forget_bias.shape[-1]
    fox_dim = w_branch_fox.shape[1] // fox_heads
    fox_w = fox_heads * fox_dim
    rank, gla_kw = gla_gate_up.shape[1:]
    gla_dv = gla_head_norm.shape[-1]
    gla_vw = w_branch_gla.shape[1]
    gla_heads = gla_vw // gla_dv
    gla_dk = gla_kw // gla_heads
    assert fox_heads + rank <= LANES

    tn = 512
    offs = {}
    start = 0
    for name, size in (("fq", fox_w), ("fk", fox_w), ("fv", fox_w), ("fl", fox_heads),
                       ("gq", gla_kw), ("gk", gla_kw), ("gv", gla_vw), ("gr", gla_vw),
                       ("gd", rank)):
        offs[name] = (start, size)
        start += size
    assert start == w_in.shape[-1]
    bf16_order = ("fq", "fk", "fv", "gv", "gq")
    f32_order = ("gr", "gk")
    out_col = {}
    pos = 0
    for name in bf16_order:
        out_col[name] = pos
        pos += offs[name][1]
    n_bf16_cols = pos
    pos = 0
    for name in f32_order:
        out_col[name] = pos
        pos += offs[name][1]

    row = lambda v: v.reshape(1, -1).astype(F32)
    h = x.reshape(n, d)
    tm_ffn = _tile(n, 1024)
    tf = _tile(ffn1_w_gate.shape[-1], 512)
    tm_proj = _tile(n, 1024)
    tm_merge = _tile(n, 512)
    tm_ple = _tile(n, 512)
    t_fox = _tile(seq, 512)
    tb_gla = _tile(seq, 1024)

    for i in range(depth):
        cols = lambda name: w_in[i][:, offs[name][0]:offs[name][0] + offs[name][1]]
        w_main = jnp.concatenate([cols(nm) for nm in bf16_order + f32_order], axis=1).astype(BF16)
        w_small = jnp.concatenate(
            [cols("fl"), cols("gd"), jnp.zeros((d, LANES - fox_heads - rank), F32)], axis=1).astype(BF16)
        fbias = jnp.zeros((1, LANES), F32).at[0, :fox_heads].set(fox_forget_bias[i])
        wgu = jnp.zeros((LANES, gla_kw), F32).at[fox_heads:fox_heads + rank].set(gla_gate_up[i]).astype(BF16)

        h = _ffn(h, row(ffn1_norm[i]), ffn1_w_gate[i].astype(BF16), ffn1_w_up[i].astype(BF16),
                 ffn1_w_down[i].astype(BF16), tm=tm_ffn, tf=tf)

        qkv, f32x, small = _proj(h, row(mix_norm[i]), w_main, w_small, tm=tm_proj, tn=tn,
                                 n_bf16_cols=n_bf16_cols, gq_col=out_col["gq"],
                                 gq_scale=gla_dk ** -0.5)

        fcum = _fcum(small, fbias, batch=batch, seq=seq, n_heads=fox_heads)
        y_fox = _fox(qkv.reshape(batch, seq, -1), fcum.reshape(batch * fox_heads, 1, seq),
                     batch=batch, seq=seq, n_heads=fox_heads, head_dim=fox_dim, t=t_fox)

        y_gla = _gla(qkv.reshape(batch, seq, -1), f32x.reshape(batch, seq, -1),
                     small.reshape(batch, seq, LANES), wgu, row(gla_gate_bias[i]),
                     row(gla_head_norm[i]), batch=batch, seq=seq, n_heads=gla_heads,
                     dk=gla_dk, dv=gla_dv, tb=tb_gla, v_col=out_col["gv"], q_col=out_col["gq"],
                     r_col=out_col["gr"], k_col=out_col["gk"])

        h = _merge(h, row(mix_norm[i]), y_fox.reshape(n, fox_w), y_gla.reshape(n, gla_vw),
                   w_merge_gate[i].astype(BF16), row(b_merge_gate[i]),
                   w_branch_fox[i].astype(BF16), w_branch_gla[i].astype(BF16),
                   w_out[i].astype(BF16), tm=tm_merge, tn=tn)

        h = _ffn(h, row(ffn2_norm[i]), ffn2_w_gate[i].astype(BF16), ffn2_w_up[i].astype(BF16),
                 ffn2_w_down[i].astype(BF16), tm=tm_ffn, tf=tf)

        h = _ple(h, p[i].reshape(n, -1), row(ple_norm[i]), row(final_norm),
                 w_ple_gate[i].astype(BF16), w_ple_proj[i].astype(BF16), tm=tm_ple,
                 final=(i == depth - 1))

    return h.reshape(batch, seq, d)
```

```python
import functools
import math

import jax
import jax.numpy as jnp
from jax import lax
from jax.experimental import pallas as pl
from jax.experimental.pallas import tpu as pltpu

EPS = 1e-6
CHUNK = 64
GLA_GATE_TAU = 16.0
LANES = 128
V7X_VMEM_LIMIT_BYTES = 60 * 1024 * 1024
LOG2E = math.log2(math.e)

F32 = jnp.float32
BF16 = jnp.bfloat16


def _rms(x, g):
    return x * lax.rsqrt(jnp.mean(x * x, axis=-1, keepdims=True) + EPS) * g


def _log_sigmoid(z):
    return jnp.minimum(z, 0.0) - jnp.log1p(jnp.exp(-jnp.abs(z)))


def _dot(a, b):
    return jnp.dot(a, b, preferred_element_type=F32)


def _params(*sem):
    return pltpu.CompilerParams(dimension_semantics=sem,
                                vmem_limit_bytes=V7X_VMEM_LIMIT_BYTES)


def _ffn_kernel(x_ref, g_ref, wg_ref, wu_ref, wd_ref, o_ref, xn_ref):
    @pl.when(pl.program_id(1) == 0)
    def _():
        x = x_ref[...]
        xn_ref[...] = _rms(x, g_ref[...]).astype(BF16)
        o_ref[...] = x

    xn = xn_ref[...]
    gate = _dot(xn, wg_ref[...])
    up = _dot(xn, wu_ref[...])
    hid = (0.5 * (gate * jax.nn.sigmoid(gate)) * up).astype(BF16)
    o_ref[...] += _dot(hid, wd_ref[...])


def _ffn(x, g, wg, wu, wd, *, tm, tf):
    n, d = x.shape
    f = wg.shape[1]
    return pl.pallas_call(
        _ffn_kernel,
        grid=(n // tm, f // tf),
        in_specs=[
            pl.BlockSpec((tm, d), lambda i, j: (i, 0)),
            pl.BlockSpec((1, d), lambda i, j: (0, 0)),
            pl.BlockSpec((d, tf), lambda i, j: (0, j)),
            pl.BlockSpec((d, tf), lambda i, j: (0, j)),
            pl.BlockSpec((tf, d), lambda i, j: (j, 0)),
        ],
        out_specs=pl.BlockSpec((tm, d), lambda i, j: (i, 0)),
        out_shape=jax.ShapeDtypeStruct((n, d), F32),
        scratch_shapes=[pltpu.VMEM((tm, d), BF16)],
        compiler_params=_params("parallel", "arbitrary"),
        name="ffn",
    )(x, g, wg, wu, wd)


def _proj_kernel(h_ref, g_ref, w_ref, ws_ref, qkv_ref, f32_ref, small_ref, un_ref,
                 *, n_bf16_tiles, gq_tile, gq_scale):
    j = pl.program_id(1)

    @pl.when(j == 0)
    def _():
        un = _rms(h_ref[...], g_ref[...]).astype(BF16)
        un_ref[...] = un
        small_ref[...] = _dot(un, ws_ref[...])

    r = _dot(un_ref[...], w_ref[...])

    @pl.when(j < n_bf16_tiles)
    def _():
        scale = jnp.where(j == gq_tile, gq_scale, 1.0).astype(F32)
        qkv_ref[...] = (r * scale).astype(BF16)

    @pl.when(j >= n_bf16_tiles)
    def _():
        f32_ref[...] = r


def _proj(h, g, w_main, w_small, *, tm, tn, n_bf16_cols, gq_col, gq_scale):
    n, d = h.shape
    n_cols = w_main.shape[1]
    n_tiles = n_cols // tn
    n_bf16_tiles = n_bf16_cols // tn
    kern = functools.partial(_proj_kernel, n_bf16_tiles=n_bf16_tiles,
                             gq_tile=gq_col // tn, gq_scale=gq_scale)
    return pl.pallas_call(
        kern,
        grid=(n // tm, n_tiles),
        in_specs=[
            pl.BlockSpec((tm, d), lambda i, j: (i, 0)),
            pl.BlockSpec((1, d), lambda i, j: (0, 0)),
            pl.BlockSpec((d, tn), lambda i, j: (0, j)),
            pl.BlockSpec((d, LANES), lambda i, j: (0, 0)),
        ],
        out_specs=[
            pl.BlockSpec((tm, tn), lambda i, j: (i, jnp.minimum(j, n_bf16_tiles - 1))),
            pl.BlockSpec((tm, tn), lambda i, j: (i, jnp.maximum(j - n_bf16_tiles, 0))),
            pl.BlockSpec((tm, LANES), lambda i, j: (i, 0)),
        ],
        out_shape=[
            jax.ShapeDtypeStruct((n, n_bf16_cols), BF16),
            jax.ShapeDtypeStruct((n, n_cols - n_bf16_cols), F32),
            jax.ShapeDtypeStruct((n, LANES), F32),
        ],
        scratch_shapes=[pltpu.VMEM((tm, d), BF16)],
        compiler_params=_params("parallel", "arbitrary"),
        name="proj",
    )(h, g, w_main, w_small)


def _fcum_kernel(small_ref, bias_ref, f_ref):
    s = small_ref.shape[0]
    f_ref[0] = _log_sigmoid(small_ref[...] + bias_ref[...])
    row = lax.broadcasted_iota(jnp.int32, (LANES, LANES), 0)
    col = lax.broadcasted_iota(jnp.int32, (LANES, LANES), 1)
    tril = (row >= col).astype(F32)

    def body(r, carry):
        rows = pl.ds(pl.multiple_of(r * LANES, LANES), LANES)
        c = jnp.dot(tril, f_ref[0, rows, :], precision=lax.Precision.HIGHEST,
                    preferred_element_type=F32) + carry
        f_ref[0, rows, :] = c
        return c[LANES - 1:LANES, :]

    lax.fori_loop(0, s // LANES, body, jnp.zeros((1, LANES), F32))


def _fcum(small, bias_row, *, batch, seq):
    return pl.pallas_call(
        _fcum_kernel,
        grid=(batch,),
        in_specs=[
            pl.BlockSpec((seq, LANES), lambda b: (b, 0)),
            pl.BlockSpec((1, LANES), lambda b: (0, 0)),
        ],
        out_specs=pl.BlockSpec((1, seq, LANES), lambda b: (b, 0, 0)),
        out_shape=jax.ShapeDtypeStruct((batch, seq, LANES), F32),
        compiler_params=_params("parallel"),
        name="fcum",
    )(small, bias_row)


def _fox_kernel(q_ref, k_ref, v_ref, f_ref, o_ref, frep_ref, vt_ref, m_ref, l_ref, acc_ref,
                s0_ref, s1_ref, mc0_ref, mc1_ref, *, t, c2, hp, dh):
    g = pl.program_id(1)
    i = pl.program_id(2)
    s_refs = (s0_ref, s1_ref)
    mc_refs = (mc0_ref, mc1_ref)

    @pl.when(i == 0)
    def _():
        row = lax.broadcasted_iota(jnp.int32, (LANES, LANES), 0)
        for h in range(hp):
            onehot = (row == g * hp + h).astype(F32)
            frep_ref[h] = jnp.dot(f_ref[0], onehot, precision=lax.Precision.HIGHEST,
                                  preferred_element_type=F32)
            vt_ref[h] = v_ref[0, :, h * dh:(h + 1) * dh].astype(F32).T.astype(BF16)

    q0 = pl.multiple_of(i * t, t)
    m_ref[...] = jnp.full(m_ref.shape, -jnp.inf, F32)
    l_ref[...] = jnp.zeros(l_ref.shape, F32)
    acc_ref[...] = jnp.zeros(acc_ref.shape, F32)
    reps = t // LANES

    def scores(kk, slot, masked):
        k0 = pl.multiple_of(kk * t, t)
        for h in range(hp):
            hs = slice(h * dh, (h + 1) * dh)
            f_base = frep_ref[h, pl.ds(q0, 8), :][0:1, :]
            bias = (f_base - frep_ref[h, pl.ds(k0, t), :]) * LOG2E
            s = lax.dot_general(k_ref[0, pl.ds(k0, t), hs], q_ref[0, :, hs],
                                (((1,), (1,)), ((), ())), preferred_element_type=F32)
            s = s * c2 + jnp.tile(bias, (1, reps))
            if masked:
                key = lax.broadcasted_iota(jnp.int32, (t, t), 0)
                qry = lax.broadcasted_iota(jnp.int32, (t, t), 1)
                s = jnp.where(key <= qry, s, -jnp.inf)
            s_refs[slot][h] = s
            mc_refs[slot][h] = jnp.max(s, axis=0, keepdims=True)

    def absorb(kk, slot):
        k0 = pl.multiple_of(kk * t, t)
        for h in range(hp):
            m_old = m_ref[h]
            m_new = jnp.maximum(m_old, mc_refs[slot][h])
            alpha = jnp.exp2(m_old - m_new)
            p = jnp.exp2(s_refs[slot][h] - m_new)
            l_ref[h] = alpha * l_ref[h] + jnp.sum(p, axis=0, keepdims=True)
            acc_ref[h] = alpha * acc_ref[h] + _dot(vt_ref[h, :, pl.ds(k0, t)], p.astype(BF16))
            m_ref[h] = m_new

    scores(i, 0, True)

    def body(kk, carry):
        prev = jnp.where(kk == 0, i, kk - 1)
        for par in range(2):
            @pl.when(kk % 2 == par)
            def _():
                absorb(prev, par)
                scores(kk, 1 - par, False)
        return carry

    lax.fori_loop(0, i, body, 0)
    last = jnp.where(i == 0, i, i - 1)
    for par in range(2):
        @pl.when(i % 2 == par)
        def _():
            absorb(last, par)
    for h in range(hp):
        o_ref[0, :, h * dh:(h + 1) * dh] = (acc_ref[h] / l_ref[h]).T.astype(o_ref.dtype)


def _fox(qkv, fcum, *, batch, seq, n_heads, head_dim, t, hp):
    assert head_dim == LANES and n_heads % hp == 0
    c2 = (head_dim ** -0.5) * LOG2E
    ng = n_heads // hp
    w = hp * head_dim
    return pl.pallas_call(
        functools.partial(_fox_kernel, t=t, c2=c2, hp=hp, dh=head_dim),
        grid=(batch, ng, seq // t),
        in_specs=[
            pl.BlockSpec((1, t, w), lambda b, g, i: (b, i, g)),
            pl.BlockSpec((1, seq, w), lambda b, g, i: (b, 0, ng + g)),
            pl.BlockSpec((1, seq, w), lambda b, g, i: (b, 0, 2 * ng + g)),
            pl.BlockSpec((1, seq, LANES), lambda b, g, i: (b, 0, 0)),
        ],
        out_specs=pl.BlockSpec((1, t, w), lambda b, g, i: (b, i, g)),
        out_shape=jax.ShapeDtypeStruct((batch, seq, n_heads * head_dim), BF16),
        scratch_shapes=[pltpu.VMEM((hp, seq, LANES), F32), pltpu.VMEM((hp, head_dim, seq), BF16),
                        pltpu.VMEM((hp, 1, t), F32), pltpu.VMEM((hp, 1, t), F32),
                        pltpu.VMEM((hp, head_dim, t), F32),
                        pltpu.VMEM((hp, t, t), F32), pltpu.VMEM((hp, t, t), F32),
                        pltpu.VMEM((hp, 1, t), F32), pltpu.VMEM((hp, 1, t), F32)],
        compiler_params=_params("parallel", "arbitrary", "arbitrary"),
        name="fox",
    )(qkv, qkv, qkv, fcum)


def _gla_kernel(q_ref, k_ref, v_ref, r_ref, small_ref, wgu_ref, gb_ref, gain_ref, o_ref,
                st_ref, la_ref, kd_ref, ea_ref, *, n_heads, dk, dv):
    tb = q_ref.shape[1]
    n_chunks = tb // CHUNK

    @pl.when(pl.program_id(1) == 0)
    def _():
        st_ref[...] = jnp.zeros(st_ref.shape, F32)

    pre = _dot(small_ref[0].astype(BF16), wgu_ref[...]) + gb_ref[...]
    la_ref[...] = _log_sigmoid(pre) * (1.0 / GLA_GATE_TAU)

    row = lax.broadcasted_iota(jnp.int32, (CHUNK, CHUNK), 0)
    col = lax.broadcasted_iota(jnp.int32, (CHUNK, CHUNK), 1)
    after = (col > row).astype(F32)

    def prep(c, carry):
        rows = pl.ds(pl.multiple_of(c * CHUNK, CHUNK), CHUNK)
        la = la_ref[rows, :]
        rev = jnp.dot(after, la, precision=lax.Precision.HIGHEST, preferred_element_type=F32)
        kd_ref[rows, :] = (k_ref[0, rows, :] * jnp.exp(rev)).astype(BF16)
        ea_ref[c] = jnp.exp(jnp.sum(la, axis=0, keepdims=True))
        return carry

    lax.fori_loop(0, n_chunks, prep, 0)

    def step(c, carry):
        rows = pl.ds(pl.multiple_of(c * CHUNK, CHUNK), CHUNK)
        ea = ea_ref[c]
        for h in range(n_heads):
            ks = slice(h * dk, (h + 1) * dk)
            vs = slice(h * dv, (h + 1) * dv)
            kv_t = lax.dot_general(v_ref[0, rows, vs], kd_ref[rows, ks],
                                   (((0,), (0,)), ((), ())), preferred_element_type=F32)
            st = st_ref[h] * ea[:, ks] + kv_t
            st_ref[h] = st
            o = lax.dot_general(q_ref[0, rows, ks], st.astype(BF16),
                                (((1,), (1,)), ((), ())), preferred_element_type=F32)
            on = _rms(o, gain_ref[...])
            r = r_ref[0, rows, vs]
            o_ref[0, rows, vs] = (on * (r * jax.nn.sigmoid(r))).astype(o_ref.dtype)
        return carry

    lax.fori_loop(0, n_chunks, step, 0)


def _gla(qkv, f32x, small, wgu, gate_bias, gain, *, batch, seq, n_heads, dk, dv, tb,
         v_col, q_col, r_col, k_col):
    kw, vw = n_heads * dk, n_heads * dv
    return pl.pallas_call(
        functools.partial(_gla_kernel, n_heads=n_heads, dk=dk, dv=dv),
        grid=(batch, seq // tb),
        in_specs=[
            pl.BlockSpec((1, tb, kw), lambda b, t: (b, t, q_col // kw)),
            pl.BlockSpec((1, tb, kw), lambda b, t: (b, t, k_col // kw)),
            pl.BlockSpec((1, tb, vw), lambda b, t: (b, t, v_col // vw)),
            pl.BlockSpec((1, tb, vw), lambda b, t: (b, t, r_col // vw)),
            pl.BlockSpec((1, tb, LANES), lambda b, t: (b, t, 0)),
            pl.BlockSpec((LANES, kw), lambda b, t: (0, 0)),
            pl.BlockSpec((1, kw), lambda b, t: (0, 0)),
            pl.BlockSpec((1, dv), lambda b, t: (0, 0)),
        ],
        out_specs=pl.BlockSpec((1, tb, vw), lambda b, t: (b, t, 0)),
        out_shape=jax.ShapeDtypeStruct((batch, seq, vw), BF16),
        scratch_shapes=[
            pltpu.VMEM((n_heads, dv, dk), F32),
            pltpu.VMEM((tb, kw), F32),
            pltpu.VMEM((tb, kw), BF16),
            pltpu.VMEM((tb // CHUNK, 1, kw), F32),
        ],
        compiler_params=_params("parallel", "arbitrary"),
        name="gla",
    )(qkv, f32x, qkv, f32x, small, wgu, gate_bias, gain)


def _merge_kernel(h_ref, g_ref, yf_ref, yg_ref, wgf_ref, wgg_ref, bgf_ref, bgg_ref,
                  wa_ref, wb_ref, wo_ref, o_ref, un_ref):
    @pl.when(pl.program_id(1) == 0)
    def _():
        h = h_ref[...]
        un_ref[...] = _rms(h, g_ref[...]).astype(BF16)
        o_ref[...] = h

    un = un_ref[...]
    g_fox = jax.nn.sigmoid(_dot(un, wgf_ref[...]) + bgf_ref[...])
    g_gla = jax.nn.sigmoid(_dot(un, wgg_ref[...]) + bgg_ref[...])
    merged = g_fox * _dot(yf_ref[...], wa_ref[...]) + g_gla * _dot(yg_ref[...], wb_ref[...])
    o_ref[...] += _dot(merged.astype(BF16), wo_ref[...])


def _merge(h, g, y_fox, y_gla, w_gate, b_gate, w_a, w_b, w_o, *, tm, tn):
    n, d = h.shape
    nj = d // tn
    return pl.pallas_call(
        _merge_kernel,
        grid=(n // tm, nj),
        in_specs=[
            pl.BlockSpec((tm, d), lambda i, j: (i, 0)),
            pl.BlockSpec((1, d), lambda i, j: (0, 0)),
            pl.BlockSpec((tm, y_fox.shape[1]), lambda i, j: (i, 0)),
            pl.BlockSpec((tm, y_gla.shape[1]), lambda i, j: (i, 0)),
            pl.BlockSpec((d, tn), lambda i, j: (0, j)),
            pl.BlockSpec((d, tn), lambda i, j: (0, nj + j)),
            pl.BlockSpec((1, tn), lambda i, j: (0, j)),
            pl.BlockSpec((1, tn), lambda i, j: (0, nj + j)),
            pl.BlockSpec((w_a.shape[0], tn), lambda i, j: (0, j)),
            pl.BlockSpec((w_b.shape[0], tn), lambda i, j: (0, j)),
            pl.BlockSpec((tn, d), lambda i, j: (j, 0)),
        ],
        out_specs=pl.BlockSpec((tm, d), lambda i, j: (i, 0)),
        out_shape=jax.ShapeDtypeStruct((n, d), F32),
        scratch_shapes=[pltpu.VMEM((tm, d), BF16)],
        compiler_params=_params("parallel", "arbitrary"),
        name="merge",
    )(h, g, y_fox, y_gla, w_gate, w_gate, b_gate, b_gate, w_a, w_b, w_o)


def _ple_kernel(h_ref, p_ref, gp_ref, gf_ref, wpg_ref, wpe_ref, o_ref, *, final):
    h = h_ref[...]
    hn = _rms(h, gp_ref[...]).astype(BF16)
    gate = jax.nn.sigmoid(_dot(hn, wpg_ref[...]))
    out = h + gate * _dot(p_ref[...].astype(BF16), wpe_ref[...])
    if final:
        out = _rms(out, gf_ref[...])
    o_ref[...] = out


def _ple(h, p, g_ple, g_final, w_gate, w_proj, *, tm, final):
    n, d = h.shape
    dp = p.shape[1]
    return pl.pallas_call(
        functools.partial(_ple_kernel, final=final),
        grid=(n // tm,),
        in_specs=[
            pl.BlockSpec((tm, d), lambda i: (i, 0)),
            pl.BlockSpec((tm, dp), lambda i: (i, 0)),
            pl.BlockSpec((1, d), lambda i: (0, 0)),
            pl.BlockSpec((1, d), lambda i: (0, 0)),
            pl.BlockSpec((d, d), lambda i: (0, 0)),
            pl.BlockSpec((dp, d), lambda i: (0, 0)),
        ],
        out_specs=pl.BlockSpec((tm, d), lambda i: (i, 0)),
        out_shape=jax.ShapeDtypeStruct((n, d), F32),
        compiler_params=_params("parallel"),
        name="ple",
    )(h, p, g_ple, g_final, w_gate, w_proj)


def _tile(n, want):
    t = min(n, want)
    assert n % t == 0, (n, want)
    return t


def kernel(x, p, ffn1_norm, ffn1_w_gate, ffn1_w_up, ffn1_w_down, mix_norm, w_in, fox_forget_bias, gla_gate_up, gla_gate_bias, gla_head_norm, w_branch_fox, w_branch_gla, w_merge_gate, b_merge_gate, w_out, ffn2_norm, ffn2_w_gate, ffn2_w_up, ffn2_w_down, ple_norm, w_ple_proj, w_ple_gate, final_norm):
    batch, seq, d = x.shape
    depth = p.shape[0]
    n = batch * seq
    fox_heads = fox_forget_bias.shape[-1]
    fox_dim = w_branch_fox.shape[1] // fox_heads
    fox_w = fox_heads * fox_dim
    rank, gla_kw = gla_gate_up.shape[1:]
    gla_dv = gla_head_norm.shape[-1]
    gla_vw = w_branch_gla.shape[1]
    gla_heads = gla_vw // gla_dv
    gla_dk = gla_kw // gla_heads
    assert fox_heads + rank <= LANES

    tn = 512
    offs = {}
    start = 0
    for name, size in (("fq", fox_w), ("fk", fox_w), ("fv", fox_w), ("fl", fox_heads),
                       ("gq", gla_kw), ("gk", gla_kw), ("gv", gla_vw), ("gr", gla_vw),
                       ("gd", rank)):
        offs[name] = (start, size)
        start += size
    assert start == w_in.shape[-1]
    bf16_order = ("fq", "fk", "fv", "gv", "gq")
    f32_order = ("gr", "gk")
    out_col = {}
    pos = 0
    for name in bf16_order:
        out_col[name] = pos
        pos += offs[name][1]
    n_bf16_cols = pos
    pos = 0
    for name in f32_order:
        out_col[name] = pos
        pos += offs[name][1]

    row = lambda v: v.reshape(1, -1).astype(F32)
    h = x.reshape(n, d)
    tm_ffn = _tile(n, 1024)
    tf = _tile(ffn1_w_gate.shape[-1], 512)
    tm_proj = _tile(n, 1024)
    tm_merge = _tile(n, 512)
    tm_ple = _tile(n, 512)
    t_fox = _tile(seq, 512)
    fox_hp = 4
    tb_gla = _tile(seq, 1024)

    for i in range(depth):
        cols = lambda name: w_in[i][:, offs[name][0]:offs[name][0] + offs[name][1]]
        w_main = jnp.concatenate([cols(nm) for nm in bf16_order + f32_order], axis=1).astype(BF16)
        w_small = jnp.concatenate(
            [cols("fl"), cols("gd"), jnp.zeros((d, LANES - fox_heads - rank), F32)], axis=1).astype(BF16)
        fbias = jnp.zeros((1, LANES), F32).at[0, :fox_heads].set(fox_forget_bias[i])
        wgu = jnp.zeros((LANES, gla_kw), F32).at[fox_heads:fox_heads + rank].set(gla_gate_up[i]).astype(BF16)

        h = _ffn(h, row(ffn1_norm[i]), ffn1_w_gate[i].astype(BF16), ffn1_w_up[i].astype(BF16),
                 ffn1_w_down[i].astype(BF16), tm=tm_ffn, tf=tf)

        qkv, f32x, small = _proj(h, row(mix_norm[i]), w_main, w_small, tm=tm_proj, tn=tn,
                                 n_bf16_cols=n_bf16_cols, gq_col=out_col["gq"],
                                 gq_scale=gla_dk ** -0.5)

        fcum = _fcum(small, fbias, batch=batch, seq=seq)
        y_fox = _fox(qkv.reshape(batch, seq, -1), fcum,
                     batch=batch, seq=seq, n_heads=fox_heads, head_dim=fox_dim, t=t_fox, hp=fox_hp)

        y_gla = _gla(qkv.reshape(batch, seq, -1), f32x.reshape(batch, seq, -1),
                     small.reshape(batch, seq, LANES), wgu, row(gla_gate_bias[i]),
                     row(gla_head_norm[i]), batch=batch, seq=seq, n_heads=gla_heads,
                     dk=gla_dk, dv=gla_dv, tb=tb_gla, v_col=out_col["gv"], q_col=out_col["gq"],
                     r_col=out_col["gr"], k_col=out_col["gk"])

        h = _merge(h, row(mix_norm[i]), y_fox.reshape(n, fox_w), y_gla.reshape(n, gla_vw),
                   w_merge_gate[i].astype(BF16), row(b_merge_gate[i]),
                   w_branch_fox[i].astype(BF16), w_branch_gla[i].astype(BF16),
                   w_out[i].astype(BF16), tm=tm_merge, tn=tn)

        h = _ffn(h, row(ffn2_norm[i]), ffn2_w_gate[i].astype(BF16), ffn2_w_up[i].astype(BF16),
                 ffn2_w_down[i].astype(BF16), tm=tm_ffn, tf=tf)

        h = _ple(h, p[i].reshape(n, -1), row(ple_norm[i]), row(final_norm),
                 w_ple_gate[i].astype(BF16), w_ple_proj[i].astype(BF16), tm=tm_ple,
                 final=(i == depth - 1))

    return h.reshape(batch, seq, d)
```

```python
import functools
import math

import jax
import jax.numpy as jnp
from jax import lax
from jax.experimental import pallas as pl
from jax.experimental.pallas import tpu as pltpu

EPS = 1e-6
CHUNK = 64
GLA_GATE_TAU = 16.0
LANES = 128
V7X_VMEM_LIMIT_BYTES = 60 * 1024 * 1024
LOG2E = math.log2(math.e)

F32 = jnp.float32
BF16 = jnp.bfloat16


def _rms(x, g):
    return x * lax.rsqrt(jnp.mean(x * x, axis=-1, keepdims=True) + EPS) * g


def _log_sigmoid(z):
    return jnp.minimum(z, 0.0) - jnp.log1p(jnp.exp(-jnp.abs(z)))


def _dot(a, b):
    return lax.dot_general(a, b, (((1,), (0,)), ((), ())), preferred_element_type=F32)


def _params(*sem):
    return pltpu.CompilerParams(dimension_semantics=sem,
                                vmem_limit_bytes=V7X_VMEM_LIMIT_BYTES)


def _ffn_kernel(x_ref, g_ref, wg_ref, wu_ref, wd_ref, o_ref, xn_ref):
    @pl.when(pl.program_id(1) == 0)
    def _():
        x = x_ref[...]
        xn_ref[...] = _rms(x, g_ref[...]).astype(BF16)
        o_ref[...] = x

    xn = xn_ref[...]
    gate = _dot(xn, wg_ref[...])
    up = _dot(xn, wu_ref[...])
    hid = (0.5 * (gate * jax.nn.sigmoid(gate)) * up).astype(BF16)
    o_ref[...] += _dot(hid, wd_ref[...])


def _ffn(x, g, wg, wu, wd, *, tm, tf):
    n, d = x.shape
    f = wg.shape[1]
    return pl.pallas_call(
        _ffn_kernel,
        grid=(n // tm, f // tf),
        in_specs=[
            pl.BlockSpec((tm, d), lambda i, j: (i, 0), pipeline_mode=pl.Buffered(1)),
            pl.BlockSpec((1, d), lambda i, j: (0, 0)),
            pl.BlockSpec((d, tf), lambda i, j: (0, j)),
            pl.BlockSpec((d, tf), lambda i, j: (0, j)),
            pl.BlockSpec((tf, d), lambda i, j: (j, 0)),
        ],
        out_specs=pl.BlockSpec((tm, d), lambda i, j: (i, 0)),
        out_shape=jax.ShapeDtypeStruct((n, d), F32),
        scratch_shapes=[pltpu.VMEM((tm, d), BF16)],
        compiler_params=_params("parallel", "arbitrary"),
        name="ffn",
    )(x, g, wg, wu, wd)


def _proj_kernel(h_ref, g_ref, wa_ref, wb_ref, ws_ref, qkv_ref, f32_ref, small_ref, un_ref,
                 *, n_direct, n_bf16_tiles, gq_tile, gq_scale):
    j = pl.program_id(1)

    @pl.when(j == 0)
    def _():
        un = _rms(h_ref[...], g_ref[...]).astype(BF16)
        un_ref[...] = un
        small_ref[...] = _dot(un, ws_ref[...])

    @pl.when(j < n_direct)
    def _():
        qkv_ref[...] = _dot(un_ref[...], wa_ref[...]).astype(BF16)

    @pl.when((j >= n_direct) & (j < n_bf16_tiles))
    def _():
        scale = jnp.where(j == gq_tile, gq_scale, 1.0).astype(F32)
        qkv_ref[...] = (_dot(un_ref[...], wb_ref[...]) * scale).astype(BF16)

    @pl.when(j >= n_bf16_tiles)
    def _():
        f32_ref[...] = _dot(un_ref[...], wb_ref[...])


def _proj(h, g, w_in, w_rest, w_small, *, tm, tn, n_direct_cols, n_bf16_cols, gq_col, gq_scale):
    n, d = h.shape
    n_cols = n_direct_cols + w_rest.shape[1]
    n_tiles = n_cols // tn
    n_direct = n_direct_cols // tn
    n_bf16_tiles = n_bf16_cols // tn
    assert n_direct_cols % tn == 0 and n_bf16_cols % tn == 0 and n_cols % tn == 0
    kern = functools.partial(_proj_kernel, n_direct=n_direct, n_bf16_tiles=n_bf16_tiles,
                             gq_tile=gq_col // tn, gq_scale=gq_scale)
    return pl.pallas_call(
        kern,
        grid=(n // tm, n_tiles),
        in_specs=[
            pl.BlockSpec((tm, d), lambda i, j: (i, 0)),
            pl.BlockSpec((1, d), lambda i, j: (0, 0)),
            pl.BlockSpec((d, tn), lambda i, j: (0, jnp.minimum(j, n_direct - 1))),
            pl.BlockSpec((d, tn), lambda i, j: (0, jnp.maximum(j - n_direct, 0))),
            pl.BlockSpec((d, LANES), lambda i, j: (0, 0)),
        ],
        out_specs=[
            pl.BlockSpec((tm, tn), lambda i, j: (i, jnp.minimum(j, n_bf16_tiles - 1))),
            pl.BlockSpec((tm, tn), lambda i, j: (i, jnp.maximum(j - n_bf16_tiles, 0))),
            pl.BlockSpec((tm, LANES), lambda i, j: (i, 0)),
        ],
        out_shape=[
            jax.ShapeDtypeStruct((n, n_bf16_cols), BF16),
            jax.ShapeDtypeStruct((n, n_cols - n_bf16_cols), F32),
            jax.ShapeDtypeStruct((n, LANES), F32),
        ],
        scratch_shapes=[pltpu.VMEM((tm, d), BF16)],
        compiler_params=_params("parallel", "arbitrary"),
        name="proj",
    )(h, g, w_in, w_rest, w_small)


def _fcum_kernel(small_ref, bias_ref, f_ref):
    s = small_ref.shape[0]
    f_ref[0] = _log_sigmoid(small_ref[...] + bias_ref[...])
    row = lax.broadcasted_iota(jnp.int32, (LANES, LANES), 0)
    col = lax.broadcasted_iota(jnp.int32, (LANES, LANES), 1)
    tril = (row >= col).astype(F32)

    def body(r, carry):
        rows = pl.ds(pl.multiple_of(r * LANES, LANES), LANES)
        c = jnp.dot(tril, f_ref[0, rows, :], precision=lax.Precision.HIGHEST,
                    preferred_element_type=F32) + carry
        f_ref[0, rows, :] = c
        return c[LANES - 1:LANES, :]

    lax.fori_loop(0, s // LANES, body, jnp.zeros((1, LANES), F32))


def _fcum(small, bias_row, *, batch, seq):
    return pl.pallas_call(
        _fcum_kernel,
        grid=(batch,),
        in_specs=[
            pl.BlockSpec((seq, LANES), lambda b: (b, 0)),
            pl.BlockSpec((1, LANES), lambda b: (0, 0)),
        ],
        out_specs=pl.BlockSpec((1, seq, LANES), lambda b: (b, 0, 0)),
        out_shape=jax.ShapeDtypeStruct((batch, seq, LANES), F32),
        compiler_params=_params("parallel"),
        name="fcum",
    )(small, bias_row)


def _fox_kernel(q_ref, k_ref, v_ref, f_ref, o_ref, frep_ref, vt_ref, m_ref, l_ref, acc_ref,
                s0_ref, s1_ref, mc0_ref, mc1_ref, *, t, c2, hp, dh):
    g = pl.program_id(1)
    i = pl.program_id(2)
    s_refs = (s0_ref, s1_ref)
    mc_refs = (mc0_ref, mc1_ref)

    @pl.when(i == 0)
    def _():
        row = lax.broadcasted_iota(jnp.int32, (LANES, LANES), 0)
        for h in range(hp):
            onehot = (row == g * hp + h).astype(F32)
            frep_ref[h] = jnp.dot(f_ref[0], onehot, precision=lax.Precision.HIGHEST,
                                  preferred_element_type=F32)
            vt_ref[h] = v_ref[0, :, h * dh:(h + 1) * dh].astype(F32).T.astype(BF16)

    q0 = pl.multiple_of(i * t, t)
    m_ref[...] = jnp.full(m_ref.shape, -jnp.inf, F32)
    l_ref[...] = jnp.zeros(l_ref.shape, F32)
    acc_ref[...] = jnp.zeros(acc_ref.shape, F32)
    reps = t // LANES

    def scores(kk, slot, masked):
        k0 = pl.multiple_of(kk * t, t)
        for h in range(hp):
            hs = slice(h * dh, (h + 1) * dh)
            f_base = frep_ref[h, pl.ds(q0, 8), :][0:1, :]
            bias = (f_base - frep_ref[h, pl.ds(k0, t), :]) * LOG2E
            s = lax.dot_general(k_ref[0, pl.ds(k0, t), hs], q_ref[0, :, hs],
                                (((1,), (1,)), ((), ())), preferred_element_type=F32)
            s = s * c2 + jnp.tile(bias, (1, reps))
            if masked:
                key = lax.broadcasted_iota(jnp.int32, (t, t), 0)
                qry = lax.broadcasted_iota(jnp.int32, (t, t), 1)
                s = jnp.where(key <= qry, s, -jnp.inf)
            s_refs[slot][h] = s
            mc_refs[slot][h] = jnp.max(s, axis=0, keepdims=True)

    def absorb(kk, slot):
        k0 = pl.multiple_of(kk * t, t)
        for h in range(hp):
            m_old = m_ref[h]
            m_new = jnp.maximum(m_old, mc_refs[slot][h])
            alpha = jnp.exp2(m_old - m_new)
            p = jnp.exp2(s_refs[slot][h] - m_new)
            l_ref[h] = alpha * l_ref[h] + jnp.sum(p, axis=0, keepdims=True)
            acc_ref[h] = alpha * acc_ref[h] + _dot(vt_ref[h, :, pl.ds(k0, t)], p.astype(BF16))
            m_ref[h] = m_new

    scores(i, 0, True)

    def body(kk, carry):
        prev = jnp.where(kk == 0, i, kk - 1)
        for par in range(2):
            @pl.when(kk % 2 == par)
            def _():
                absorb(prev, par)
                scores(kk, 1 - par, False)
        return carry

    lax.fori_loop(0, i, body, 0)
    last = jnp.where(i == 0, i, i - 1)
    for par in range(2):
        @pl.when(i % 2 == par)
        def _():
            absorb(last, par)
    for h in range(hp):
        o_ref[0, :, h * dh:(h + 1) * dh] = (acc_ref[h] / l_ref[h]).T.astype(o_ref.dtype)


def _fox(qkv, fcum, *, batch, seq, n_heads, head_dim, t, hp):
    assert head_dim == LANES and n_heads % hp == 0
    c2 = (head_dim ** -0.5) * LOG2E
    ng = n_heads // hp
    w = hp * head_dim
    return pl.pallas_call(
        functools.partial(_fox_kernel, t=t, c2=c2, hp=hp, dh=head_dim),
        grid=(batch, ng, seq // t),
        in_specs=[
            pl.BlockSpec((1, t, w), lambda b, g, i: (b, i, g)),
            pl.BlockSpec((1, seq, w), lambda b, g, i: (b, 0, ng + g)),
            pl.BlockSpec((1, seq, w), lambda b, g, i: (b, 0, 2 * ng + g)),
            pl.BlockSpec((1, seq, LANES), lambda b, g, i: (b, 0, 0)),
        ],
        out_specs=pl.BlockSpec((1, t, w), lambda b, g, i: (b, i, g)),
        out_shape=jax.ShapeDtypeStruct((batch, seq, n_heads * head_dim), BF16),
        scratch_shapes=[pltpu.VMEM((hp, seq, LANES), F32), pltpu.VMEM((hp, head_dim, seq), BF16),
                        pltpu.VMEM((hp, 1, t), F32), pltpu.VMEM((hp, 1, t), F32),
                        pltpu.VMEM((hp, head_dim, t), F32),
                        pltpu.VMEM((hp, t, t), F32), pltpu.VMEM((hp, t, t), F32),
                        pltpu.VMEM((hp, 1, t), F32), pltpu.VMEM((hp, 1, t), F32)],
        compiler_params=_params("parallel", "arbitrary", "arbitrary"),
        name="fox",
    )(qkv, qkv, qkv, fcum)


def _gla_kernel(q_ref, k_ref, v_ref, r_ref, small_ref, wgu_ref, gb_ref, gain_ref, o_ref,
                st_ref, la_ref, kd_ref, ea_ref, *, n_heads, dk, dv):
    tb = q_ref.shape[1]
    n_chunks = tb // CHUNK

    @pl.when(pl.program_id(1) == 0)
    def _():
        st_ref[...] = jnp.zeros(st_ref.shape, F32)

    pre = _dot(small_ref[0].astype(BF16), wgu_ref[...]) + gb_ref[...]
    la_ref[...] = _log_sigmoid(pre) * (1.0 / GLA_GATE_TAU)

    row = lax.broadcasted_iota(jnp.int32, (CHUNK, CHUNK), 0)
    col = lax.broadcasted_iota(jnp.int32, (CHUNK, CHUNK), 1)
    after = (col > row).astype(F32)

    def prep(c, carry):
        rows = pl.ds(pl.multiple_of(c * CHUNK, CHUNK), CHUNK)
        la = la_ref[rows, :]
        rev = jnp.dot(after, la, precision=lax.Precision.HIGHEST, preferred_element_type=F32)
        kd_ref[rows, :] = (k_ref[0, rows, :] * jnp.exp(rev)).astype(BF16)
        ea_ref[c] = jnp.exp(jnp.sum(la, axis=0, keepdims=True))
        return carry

    lax.fori_loop(0, n_chunks, prep, 0)

    def step(c, carry):
        rows = pl.ds(pl.multiple_of(c * CHUNK, CHUNK), CHUNK)
        ea = ea_ref[c]
        for h in range(n_heads):
            ks = slice(h * dk, (h + 1) * dk)
            vs = slice(h * dv, (h + 1) * dv)
            kv_t = lax.dot_general(v_ref[0, rows, vs], kd_ref[rows, ks],
                                   (((0,), (0,)), ((), ())), preferred_element_type=F32)
            st = st_ref[h] * ea[:, ks] + kv_t
            st_ref[h] = st
            o = lax.dot_general(q_ref[0, rows, ks], st.astype(BF16),
                                (((1,), (1,)), ((), ())), preferred_element_type=F32)
            on = _rms(o, gain_ref[...])
            r = r_ref[0, rows, vs]
            o_ref[0, rows, vs] = (on * (r * jax.nn.sigmoid(r))).astype(o_ref.dtype)
        return carry

    lax.fori_loop(0, n_chunks, step, 0)


def _gla(qkv, f32x, small, wgu, gate_bias, gain, *, batch, seq, n_heads, dk, dv, tb,
         v_col, q_col, r_col, k_col):
    kw, vw = n_heads * dk, n_heads * dv
    return pl.pallas_call(
        functools.partial(_gla_kernel, n_heads=n_heads, dk=dk, dv=dv),
        grid=(batch, seq // tb),
        in_specs=[
            pl.BlockSpec((1, tb, kw), lambda b, t: (b, t, q_col // kw)),
            pl.BlockSpec((1, tb, kw), lambda b, t: (b, t, k_col // kw)),
            pl.BlockSpec((1, tb, vw), lambda b, t: (b, t, v_col // vw)),
            pl.BlockSpec((1, tb, vw), lambda b, t: (b, t, r_col // vw)),
            pl.BlockSpec((1, tb, LANES), lambda b, t: (b, t, 0)),
            pl.BlockSpec((LANES, kw), lambda b, t: (0, 0)),
            pl.BlockSpec((1, kw), lambda b, t: (0, 0)),
            pl.BlockSpec((1, dv), lambda b, t: (0, 0)),
        ],
        out_specs=pl.BlockSpec((1, tb, vw), lambda b, t: (b, t, 0)),
        out_shape=jax.ShapeDtypeStruct((batch, seq, vw), BF16),
        scratch_shapes=[
            pltpu.VMEM((n_heads, dv, dk), F32),
            pltpu.VMEM((tb, kw), F32),
            pltpu.VMEM((tb, kw), BF16),
            pltpu.VMEM((tb // CHUNK, 1, kw), F32),
        ],
        compiler_params=_params("parallel", "arbitrary"),
        name="gla",
    )(qkv, f32x, qkv, f32x, small, wgu, gate_bias, gain)


def _merge_kernel(h_ref, g_ref, yf_ref, yg_ref, wgf_ref, wgg_ref, bgf_ref, bgg_ref,
                  wa_ref, wb_ref, wo_ref, o_ref, un_ref):
    @pl.when(pl.program_id(1) == 0)
    def _():
        h = h_ref[...]
        un_ref[...] = _rms(h, g_ref[...]).astype(BF16)
        o_ref[...] = h

    un = un_ref[...]
    g_fox = jax.nn.sigmoid(_dot(un, wgf_ref[...]) + bgf_ref[...])
    g_gla = jax.nn.sigmoid(_dot(un, wgg_ref[...]) + bgg_ref[...])
    merged = g_fox * _dot(yf_ref[...], wa_ref[...]) + g_gla * _dot(yg_ref[...], wb_ref[...])
    o_ref[...] += _dot(merged.astype(BF16), wo_ref[...])


def _merge(h, g, y_fox, y_gla, w_gate, b_gate, w_a, w_b, w_o, *, tm, tn):
    n, d = h.shape
    nj = d // tn
    return pl.pallas_call(
        _merge_kernel,
        grid=(n // tm, nj),
        in_specs=[
            pl.BlockSpec((tm, d), lambda i, j: (i, 0)),
            pl.BlockSpec((1, d), lambda i, j: (0, 0)),
            pl.BlockSpec((tm, y_fox.shape[1]), lambda i, j: (i, 0)),
            pl.BlockSpec((tm, y_gla.shape[1]), lambda i, j: (i, 0)),
            pl.BlockSpec((d, tn), lambda i, j: (0, j)),
            pl.BlockSpec((d, tn), lambda i, j: (0, nj + j)),
            pl.BlockSpec((1, tn), lambda i, j: (0, j)),
            pl.BlockSpec((1, tn), lambda i, j: (0, nj + j)),
            pl.BlockSpec((w_a.shape[0], tn), lambda i, j: (0, j)),
            pl.BlockSpec((w_b.shape[0], tn), lambda i, j: (0, j)),
            pl.BlockSpec((tn, d), lambda i, j: (j, 0)),
        ],
        out_specs=pl.BlockSpec((tm, d), lambda i, j: (i, 0)),
        out_shape=jax.ShapeDtypeStruct((n, d), F32),
        scratch_shapes=[pltpu.VMEM((tm, d), BF16)],
        compiler_params=_params("parallel", "arbitrary"),
        name="merge",
    )(h, g, y_fox, y_gla, w_gate, w_gate, b_gate, b_gate, w_a, w_b, w_o)


def _ple_kernel(h_ref, p_ref, gp_ref, gf_ref, wpg_ref, wpe_ref, o_ref, *, final):
    h = h_ref[...]
    hn = _rms(h, gp_ref[...]).astype(BF16)
    gate = jax.nn.sigmoid(_dot(hn, wpg_ref[...]))
    out = h + gate * _dot(p_ref[...].astype(BF16), wpe_ref[...])
    if final:
        out = _rms(out, gf_ref[...])
    o_ref[...] = out


def _ple(h, p, g_ple, g_final, w_gate, w_proj, *, tm, final):
    n, d = h.shape
    dp = p.shape[1]
    return pl.pallas_call(
        functools.partial(_ple_kernel, final=final),
        grid=(n // tm,),
        in_specs=[
            pl.BlockSpec((tm, d), lambda i: (i, 0)),
            pl.BlockSpec((tm, dp), lambda i: (i, 0)),
            pl.BlockSpec((1, d), lambda i: (0, 0)),
            pl.BlockSpec((1, d), lambda i: (0, 0)),
            pl.BlockSpec((d, d), lambda i: (0, 0)),
            pl.BlockSpec((dp, d), lambda i: (0, 0)),
        ],
        out_specs=pl.BlockSpec((tm, d), lambda i: (i, 0)),
        out_shape=jax.ShapeDtypeStruct((n, d), F32),
        compiler_params=_params("parallel"),
        name="ple",
    )(h, p, g_ple, g_final, w_gate, w_proj)


def _tile(n, want):
    t = min(n, want)
    assert n % t == 0, (n, want)
    return t


def kernel(x, p, ffn1_norm, ffn1_w_gate, ffn1_w_up, ffn1_w_down, mix_norm, w_in, fox_forget_bias, gla_gate_up, gla_gate_bias, gla_head_norm, w_branch_fox, w_branch_gla, w_merge_gate, b_merge_gate, w_out, ffn2_norm, ffn2_w_gate, ffn2_w_up, ffn2_w_down, ple_norm, w_ple_proj, w_ple_gate, final_norm):
    batch, seq, d = x.shape
    depth = p.shape[0]
    n = batch * seq
    fox_heads = fox_forget_bias.shape[-1]
    fox_dim = w_branch_fox.shape[1] // fox_heads
    fox_w = fox_heads * fox_dim
    rank, gla_kw = gla_gate_up.shape[1:]
    gla_dv = gla_head_norm.shape[-1]
    gla_vw = w_branch_gla.shape[1]
    gla_heads = gla_vw // gla_dv
    gla_dk = gla_kw // gla_heads
    assert fox_heads + rank <= LANES

    tn = 512
    offs = {}
    start = 0
    for name, size in (("fq", fox_w), ("fk", fox_w), ("fv", fox_w), ("fl", fox_heads),
                       ("gq", gla_kw), ("gk", gla_kw), ("gv", gla_vw), ("gr", gla_vw),
                       ("gd", rank)):
        offs[name] = (start, size)
        start += size
    assert start == w_in.shape[-1]
    bf16_order = ("fq", "fk", "fv", "gv", "gq")
    f32_order = ("gr", "gk")
    n_direct_regions = 3
    n_direct_cols = 3 * fox_w
    assert offs["fv"][0] + fox_w == n_direct_cols
    out_col = {}
    pos = 0
    for name in bf16_order:
        out_col[name] = pos
        pos += offs[name][1]
    n_bf16_cols = pos
    pos = 0
    for name in f32_order:
        out_col[name] = pos
        pos += offs[name][1]

    row = lambda v: v.reshape(1, -1).astype(F32)
    h = x.reshape(n, d)
    tm_ffn = _tile(n, 1024)
    tf = _tile(ffn1_w_gate.shape[-1], 512)
    tm_proj = _tile(n, 1024)
    tm_merge = _tile(n, 512)
    tm_ple = _tile(n, 512)
    t_fox = _tile(seq, 512)
    fox_hp = 4
    tb_gla = _tile(seq, 1024)

    for i in range(depth):
        cols = lambda name: w_in[i][:, offs[name][0]:offs[name][0] + offs[name][1]]
        w_rest = jnp.concatenate([cols(nm) for nm in (bf16_order + f32_order)[n_direct_regions:]], axis=1)
        w_small = jnp.concatenate(
            [cols("fl"), cols("gd"), jnp.zeros((d, LANES - fox_heads - rank), F32)], axis=1).astype(BF16)
        fbias = jnp.zeros((1, LANES), F32).at[0, :fox_heads].set(fox_forget_bias[i])
        wgu = jnp.zeros((LANES, gla_kw), F32).at[fox_heads:fox_heads + rank].set(gla_gate_up[i]).astype(BF16)

        h = _ffn(h, row(ffn1_norm[i]), ffn1_w_gate[i], ffn1_w_up[i], ffn1_w_down[i],
                 tm=tm_ffn, tf=tf)

        qkv, f32x, small = _proj(h, row(mix_norm[i]), w_in[i], w_rest, w_small, tm=tm_proj, tn=tn,
                                 n_direct_cols=n_direct_cols, n_bf16_cols=n_bf16_cols,
                                 gq_col=out_col["gq"], gq_scale=gla_dk ** -0.5)

        fcum = _fcum(small, fbias, batch=batch, seq=seq)
        y_fox = _fox(qkv.reshape(batch, seq, -1), fcum,
                     batch=batch, seq=seq, n_heads=fox_heads, head_dim=fox_dim, t=t_fox, hp=fox_hp)

        y_gla = _gla(qkv.reshape(batch, seq, -1), f32x.reshape(batch, seq, -1),
                     small.reshape(batch, seq, LANES), wgu, row(gla_gate_bias[i]),
                     row(gla_head_norm[i]), batch=batch, seq=seq, n_heads=gla_heads,
                     dk=gla_dk, dv=gla_dv, tb=tb_gla, v_col=out_col["gv"], q_col=out_col["gq"],
                     r_col=out_col["gr"], k_col=out_col["gk"])

        h = _merge(h, row(mix_norm[i]), y_fox.reshape(n, fox_w), y_gla.reshape(n, gla_vw),
                   w_merge_gate[i], row(b_merge_gate[i]), w_branch_fox[i], w_branch_gla[i],
                   w_out[i], tm=tm_merge, tn=tn)

        h = _ffn(h, row(ffn2_norm[i]), ffn2_w_gate[i], ffn2_w_up[i], ffn2_w_down[i],
                 tm=tm_ffn, tf=tf)

        h = _ple(h, p[i].reshape(n, -1), row(ple_norm[i]), row(final_norm),
                 w_ple_gate[i].astype(BF16), w_ple_proj[i].astype(BF16), tm=tm_ple,
                 final=(i == depth - 1))

    return h.reshape(batch, seq, d)
```

```python
import functools
import math
from typing import Callable, NamedTuple

import jax
import jax.numpy as jnp
from jax import lax
from jax.experimental import pallas as pl
from jax.experimental.pallas import tpu as pltpu

EPS = 1e-6
CHUNK = 64
GLA_GATE_TAU = 16.0
LANES = 128
V7X_VMEM_LIMIT_BYTES = 60 * 1024 * 1024
LOG2E = math.log2(math.e)

F32 = jnp.float32
BF16 = jnp.bfloat16


def _rms(x, g):
    return x * lax.rsqrt(jnp.mean(x * x, axis=-1, keepdims=True) + EPS) * g


def _log_sigmoid(z):
    return jnp.minimum(z, 0.0) - jnp.log1p(jnp.exp(-jnp.abs(z)))


def _dot(a, b):
    return lax.dot_general(a, b, (((1,), (0,)), ((), ())), preferred_element_type=F32)


def _params(*sem):
    return pltpu.CompilerParams(dimension_semantics=sem,
                                vmem_limit_bytes=V7X_VMEM_LIMIT_BYTES)


class _Cast(NamedTuple):
    src: jax.Array
    block: tuple
    index: Callable
    live: Callable


def _cast_cols(src, ni, tn, j0, nj):
    r, c = src.shape
    assert r % ni == 0 and (r // ni) % 16 == 0 and c == nj * tn, (src.shape, ni, tn, nj)
    return _Cast(src, (r // ni, tn),
                 lambda i, j: (i, jnp.clip(j - j0, 0, nj - 1)),
                 lambda i, j: (j >= j0) & (j < j0 + nj))


def _cast_rows(src, ni, tn, j0, nj):
    r, c = src.shape
    assert c % ni == 0 and (c // ni) % LANES == 0 and r == nj * tn, (src.shape, ni, tn, nj)
    return _Cast(src, (tn, c // ni),
                 lambda i, j: (jnp.clip(j - j0, 0, nj - 1), i),
                 lambda i, j: (j >= j0) & (j < j0 + nj))


def _cast_specs(casts):
    specs = [pl.BlockSpec(c.block, c.index) for c in casts]
    shapes = [jax.ShapeDtypeStruct(c.src.shape, BF16) for c in casts]
    return specs, shapes


def _run_casts(lives, src_refs, dst_refs):
    i, j = pl.program_id(0), pl.program_id(1)
    for live, src, dst in zip(lives, src_refs, dst_refs):
        @pl.when(live(i, j))
        def _():
            dst[...] = src[...].astype(BF16)


def _ffn_kernel(*refs, lives):
    nc = len(lives)
    x_ref, g_ref, wg_ref, wu_ref, wd_ref = refs[:5]
    o_ref, xn_ref = refs[5 + nc], refs[6 + 2 * nc]

    @pl.when(pl.program_id(1) == 0)
    def _():
        x = x_ref[...]
        xn_ref[...] = _rms(x, g_ref[...]).astype(BF16)
        o_ref[...] = x

    xn = xn_ref[...]
    gate = _dot(xn, wg_ref[...])
    up = _dot(xn, wu_ref[...])
    hid = (0.5 * (gate * jax.nn.sigmoid(gate)) * up).astype(BF16)
    o_ref[...] += _dot(hid, wd_ref[...])
    _run_casts(lives, refs[5:5 + nc], refs[6 + nc:6 + 2 * nc])


def _ffn(x, g, wg, wu, wd, *, tm, tf, casts=()):
    n, d = x.shape
    f = wg.shape[1]
    cast_specs, cast_shapes = _cast_specs(casts)
    return pl.pallas_call(
        functools.partial(_ffn_kernel, lives=tuple(c.live for c in casts)),
        grid=(n // tm, f // tf),
        in_specs=[
            pl.BlockSpec((tm, d), lambda i, j: (i, 0)),
            pl.BlockSpec((1, d), lambda i, j: (0, 0)),
            pl.BlockSpec((d, tf), lambda i, j: (0, j)),
            pl.BlockSpec((d, tf), lambda i, j: (0, j)),
            pl.BlockSpec((tf, d), lambda i, j: (j, 0)),
        ] + cast_specs,
        out_specs=[pl.BlockSpec((tm, d), lambda i, j: (i, 0))] + cast_specs,
        out_shape=[jax.ShapeDtypeStruct((n, d), F32)] + cast_shapes,
        scratch_shapes=[pltpu.VMEM((tm, d), BF16)],
        compiler_params=_params("parallel", "arbitrary"),
        name="ffn",
    )(x, g, wg, wu, wd, *[c.src for c in casts])


def _proj_kernel(*refs, lives, n_direct, n_bf16_tiles, gq_tile, gq_scale):
    nc = len(lives)
    h_ref, g_ref, wa_ref, wb_ref, ws_ref = refs[:5]
    qkv_ref, f32_ref, small_ref = refs[5 + nc:8 + nc]
    un_ref = refs[8 + 2 * nc]
    j = pl.program_id(1)
    _run_casts(lives, refs[5:5 + nc], refs[8 + nc:8 + 2 * nc])

    @pl.when(j == 0)
    def _():
        un = _rms(h_ref[...], g_ref[...]).astype(BF16)
        un_ref[...] = un
        small_ref[...] = _dot(un, ws_ref[...])

    @pl.when(j < n_direct)
    def _():
        qkv_ref[...] = _dot(un_ref[...], wa_ref[...]).astype(BF16)

    @pl.when((j >= n_direct) & (j < n_bf16_tiles))
    def _():
        scale = jnp.where(j == gq_tile, gq_scale, 1.0).astype(F32)
        qkv_ref[...] = (_dot(un_ref[...], wb_ref[...]) * scale).astype(BF16)

    @pl.when(j >= n_bf16_tiles)
    def _():
        f32_ref[...] = _dot(un_ref[...], wb_ref[...])


def _proj(h, g, w_in, w_rest, w_small, *, tm, tn, n_direct_cols, rest_tiles, n_bf16_cols,
          gq_col, gq_scale, casts=()):
    n, d = h.shape
    n_direct = n_direct_cols // tn
    n_tiles = n_direct + len(rest_tiles)
    n_cols = n_tiles * tn
    n_bf16_tiles = n_bf16_cols // tn
    assert n_direct_cols % tn == 0 and n_bf16_cols % tn == 0

    def rest_tile(j):
        k = jnp.clip(j - n_direct, 0, len(rest_tiles) - 1)
        t = jnp.int32(rest_tiles[-1])
        for kk in range(len(rest_tiles) - 2, -1, -1):
            t = jnp.where(k == kk, rest_tiles[kk], t)
        return t

    cast_specs, cast_shapes = _cast_specs(casts)
    kern = functools.partial(_proj_kernel, lives=tuple(c.live for c in casts), n_direct=n_direct,
                             n_bf16_tiles=n_bf16_tiles, gq_tile=gq_col // tn, gq_scale=gq_scale)
    return pl.pallas_call(
        kern,
        grid=(n // tm, n_tiles),
        in_specs=[
            pl.BlockSpec((tm, d), lambda i, j: (i, 0)),
            pl.BlockSpec((1, d), lambda i, j: (0, 0)),
            pl.BlockSpec((d, tn), lambda i, j: (0, jnp.minimum(j, n_direct - 1))),
            pl.BlockSpec((d, tn), lambda i, j: (0, rest_tile(j))),
            pl.BlockSpec((d, LANES), lambda i, j: (0, 0)),
        ] + cast_specs,
        out_specs=[
            pl.BlockSpec((tm, tn), lambda i, j: (i, jnp.minimum(j, n_bf16_tiles - 1))),
            pl.BlockSpec((tm, tn), lambda i, j: (i, jnp.maximum(j - n_bf16_tiles, 0))),
            pl.BlockSpec((tm, LANES), lambda i, j: (i, 0)),
        ] + cast_specs,
        out_shape=[
            jax.ShapeDtypeStruct((n, n_bf16_cols), BF16),
            jax.ShapeDtypeStruct((n, n_cols - n_bf16_cols), F32),
            jax.ShapeDtypeStruct((n, LANES), F32),
        ] + cast_shapes,
        scratch_shapes=[pltpu.VMEM((tm, d), BF16)],
        compiler_params=_params("parallel", "arbitrary"),
        name="proj",
    )(h, g, w_in, w_rest, w_small, *[c.src for c in casts])


def _fcum_kernel(small_ref, bias_ref, f_ref):
    s = small_ref.shape[0]
    f_ref[0] = _log_sigmoid(small_ref[...] + bias_ref[...])
    row = lax.broadcasted_iota(jnp.int32, (LANES, LANES), 0)
    col = lax.broadcasted_iota(jnp.int32, (LANES, LANES), 1)
    tril = (row >= col).astype(F32)

    def body(r, carry):
        rows = pl.ds(pl.multiple_of(r * LANES, LANES), LANES)
        c = jnp.dot(tril, f_ref[0, rows, :], precision=lax.Precision.HIGHEST,
                    preferred_element_type=F32) + carry
        f_ref[0, rows, :] = c
        return c[LANES - 1:LANES, :]

    lax.fori_loop(0, s // LANES, body, jnp.zeros((1, LANES), F32))


def _fcum(small, bias_row, *, batch, seq):
    return pl.pallas_call(
        _fcum_kernel,
        grid=(batch,),
        in_specs=[
            pl.BlockSpec((seq, LANES), lambda b: (b, 0)),
            pl.BlockSpec((1, LANES), lambda b: (0, 0)),
        ],
        out_specs=pl.BlockSpec((1, seq, LANES), lambda b: (b, 0, 0)),
        out_shape=jax.ShapeDtypeStruct((batch, seq, LANES), F32),
        compiler_params=_params("parallel"),
        name="fcum",
    )(small, bias_row)


def _fox_kernel(q_ref, k_ref, v_ref, f_ref, o_ref, frep_ref, vt_ref, m_ref, l_ref, acc_ref,
                s0_ref, s1_ref, mc0_ref, mc1_ref, *, t, c2, hp, dh):
    g = pl.program_id(1)
    i = pl.program_id(2)
    s_refs = (s0_ref, s1_ref)
    mc_refs = (mc0_ref, mc1_ref)

    @pl.when(i == 0)
    def _():
        row = lax.broadcasted_iota(jnp.int32, (LANES, LANES), 0)
        for h in range(hp):
            onehot = (row == g * hp + h).astype(F32)
            frep_ref[h] = jnp.dot(f_ref[0], onehot, precision=lax.Precision.HIGHEST,
                                  preferred_element_type=F32)
            vt_ref[h] = v_ref[0, :, h * dh:(h + 1) * dh].astype(F32).T.astype(BF16)

    q0 = pl.multiple_of(i * t, t)
    m_ref[...] = jnp.full(m_ref.shape, -jnp.inf, F32)
    l_ref[...] = jnp.zeros(l_ref.shape, F32)
    acc_ref[...] = jnp.zeros(acc_ref.shape, F32)
    reps = t // LANES

    def scores(kk, slot, masked):
        k0 = pl.multiple_of(kk * t, t)
        for h in range(hp):
            hs = slice(h * dh, (h + 1) * dh)
            f_base = frep_ref[h, pl.ds(q0, 8), :][0:1, :]
            bias = (f_base - frep_ref[h, pl.ds(k0, t), :]) * LOG2E
            s = lax.dot_general(k_ref[0, pl.ds(k0, t), hs], q_ref[0, :, hs],
                                (((1,), (1,)), ((), ())), preferred_element_type=F32)
            s = s * c2 + jnp.tile(bias, (1, reps))
            if masked:
                key = lax.broadcasted_iota(jnp.int32, (t, t), 0)
                qry = lax.broadcasted_iota(jnp.int32, (t, t), 1)
                s = jnp.where(key <= qry, s, -jnp.inf)
            s_refs[slot][h] = s
            mc_refs[slot][h] = jnp.max(s, axis=0, keepdims=True)

    def absorb(kk, slot):
        k0 = pl.multiple_of(kk * t, t)
        for h in range(hp):
            m_old = m_ref[h]
            m_new = jnp.maximum(m_old, mc_refs[slot][h])
            alpha = jnp.exp2(m_old - m_new)
            p = jnp.exp2(s_refs[slot][h] - m_new)
            l_ref[h] = alpha * l_ref[h] + jnp.sum(p, axis=0, keepdims=True)
            acc_ref[h] = alpha * acc_ref[h] + _dot(vt_ref[h, :, pl.ds(k0, t)], p.astype(BF16))
            m_ref[h] = m_new

    scores(i, 0, True)

    def body(kk, carry):
        prev = jnp.where(kk == 0, i, kk - 1)
        for par in range(2):
            @pl.when(kk % 2 == par)
            def _():
                absorb(prev, par)
                scores(kk, 1 - par, False)
        return carry

    lax.fori_loop(0, i, body, 0)
    last = jnp.where(i == 0, i, i - 1)
    for par in range(2):
        @pl.when(i % 2 == par)
        def _():
            absorb(last, par)
    for h in range(hp):
        o_ref[0, :, h * dh:(h + 1) * dh] = (acc_ref[h] / l_ref[h]).T.astype(o_ref.dtype)


def _fox(qkv, fcum, *, batch, seq, n_heads, head_dim, t, hp):
    assert head_dim == LANES and n_heads % hp == 0
    c2 = (head_dim ** -0.5) * LOG2E
    ng = n_heads // hp
    w = hp * head_dim
    return pl.pallas_call(
        functools.partial(_fox_kernel, t=t, c2=c2, hp=hp, dh=head_dim),
        grid=(batch, ng, seq // t),
        in_specs=[
            pl.BlockSpec((1, t, w), lambda b, g, i: (b, i, g)),
            pl.BlockSpec((1, seq, w), lambda b, g, i: (b, 0, ng + g)),
            pl.BlockSpec((1, seq, w), lambda b, g, i: (b, 0, 2 * ng + g)),
            pl.BlockSpec((1, seq, LANES), lambda b, g, i: (b, 0, 0)),
        ],
        out_specs=pl.BlockSpec((1, t, w), lambda b, g, i: (b, i, g)),
        out_shape=jax.ShapeDtypeStruct((batch, seq, n_heads * head_dim), BF16),
        scratch_shapes=[pltpu.VMEM((hp, seq, LANES), F32), pltpu.VMEM((hp, head_dim, seq), BF16),
                        pltpu.VMEM((hp, 1, t), F32), pltpu.VMEM((hp, 1, t), F32),
                        pltpu.VMEM((hp, head_dim, t), F32),
                        pltpu.VMEM((hp, t, t), F32), pltpu.VMEM((hp, t, t), F32),
                        pltpu.VMEM((hp, 1, t), F32), pltpu.VMEM((hp, 1, t), F32)],
        compiler_params=_params("parallel", "arbitrary", "arbitrary"),
        name="fox",
    )(qkv, qkv, qkv, fcum)


def _gla_kernel(q_ref, k_ref, v_ref, r_ref, small_ref, wgu_ref, gb_ref, gain_ref, o_ref,
                st_ref, la_ref, kd_ref, ea_ref, *, n_heads, dk, dv):
    tb = q_ref.shape[1]
    n_chunks = tb // CHUNK

    @pl.when(pl.program_id(1) == 0)
    def _():
        st_ref[...] = jnp.zeros(st_ref.shape, F32)

    pre = _dot(small_ref[0].astype(BF16), wgu_ref[...]) + gb_ref[...]
    la_ref[...] = _log_sigmoid(pre) * (1.0 / GLA_GATE_TAU)

    row = lax.broadcasted_iota(jnp.int32, (CHUNK, CHUNK), 0)
    col = lax.broadcasted_iota(jnp.int32, (CHUNK, CHUNK), 1)
    after = (col > row).astype(F32)

    def prep(c, carry):
        rows = pl.ds(pl.multiple_of(c * CHUNK, CHUNK), CHUNK)
        la = la_ref[rows, :]
        rev = jnp.dot(after, la, precision=lax.Precision.HIGHEST, preferred_element_type=F32)
        kd_ref[rows, :] = (k_ref[0, rows, :] * jnp.exp(rev)).astype(BF16)
        ea_ref[c] = jnp.exp(jnp.sum(la, axis=0, keepdims=True))
        return carry

    lax.fori_loop(0, n_chunks, prep, 0)

    def step(c, carry):
        rows = pl.ds(pl.multiple_of(c * CHUNK, CHUNK), CHUNK)
        ea = ea_ref[c]
        for h in range(n_heads):
            ks = slice(h * dk, (h + 1) * dk)
            vs = slice(h * dv, (h + 1) * dv)
            kv_t = lax.dot_general(v_ref[0, rows, vs], kd_ref[rows, ks],
                                   (((0,), (0,)), ((), ())), preferred_element_type=F32)
            st = st_ref[h] * ea[:, ks] + kv_t
            st_ref[h] = st
            o = lax.dot_general(q_ref[0, rows, ks], st.astype(BF16),
                                (((1,), (1,)), ((), ())), preferred_element_type=F32)
            on = _rms(o, gain_ref[...])
            r = r_ref[0, rows, vs]
            o_ref[0, rows, vs] = (on * (r * jax.nn.sigmoid(r))).astype(o_ref.dtype)
        return carry

    lax.fori_loop(0, n_chunks, step, 0)


def _gla(qkv, f32x, small, wgu, gate_bias, gain, *, batch, seq, n_heads, dk, dv, tb,
         v_col, q_col, r_col, k_col):
    kw, vw = n_heads * dk, n_heads * dv
    return pl.pallas_call(
        functools.partial(_gla_kernel, n_heads=n_heads, dk=dk, dv=dv),
        grid=(batch, seq // tb),
        in_specs=[
            pl.BlockSpec((1, tb, kw), lambda b, t: (b, t, q_col // kw)),
            pl.BlockSpec((1, tb, kw), lambda b, t: (b, t, k_col // kw)),
            pl.BlockSpec((1, tb, vw), lambda b, t: (b, t, v_col // vw)),
            pl.BlockSpec((1, tb, vw), lambda b, t: (b, t, r_col // vw)),
            pl.BlockSpec((1, tb, LANES), lambda b, t: (b, t, 0)),
            pl.BlockSpec((LANES, kw), lambda b, t: (0, 0)),
            pl.BlockSpec((1, kw), lambda b, t: (0, 0)),
            pl.BlockSpec((1, dv), lambda b, t: (0, 0)),
        ],
        out_specs=pl.BlockSpec((1, tb, vw), lambda b, t: (b, t, 0)),
        out_shape=jax.ShapeDtypeStruct((batch, seq, vw), BF16),
        scratch_shapes=[
            pltpu.VMEM((n_heads, dv, dk), F32),
            pltpu.VMEM((tb, kw), F32),
            pltpu.VMEM((tb, kw), BF16),
            pltpu.VMEM((tb // CHUNK, 1, kw), F32),
        ],
        compiler_params=_params("parallel", "arbitrary"),
        name="gla",
    )(qkv, f32x, qkv, f32x, small, wgu, gate_bias, gain)


def _merge_kernel(*refs, lives):
    nc = len(lives)
    (h_ref, g_ref, yf_ref, yg_ref, wgf_ref, wgg_ref, bgf_ref, bgg_ref,
     wa_ref, wb_ref, wo_ref) = refs[:11]
    o_ref, un_ref = refs[11 + nc], refs[12 + 2 * nc]
    _run_casts(lives, refs[11:11 + nc], refs[12 + nc:12 + 2 * nc])

    @pl.when(pl.program_id(1) == 0)
    def _():
        h = h_ref[...]
        un_ref[...] = _rms(h, g_ref[...]).astype(BF16)
        o_ref[...] = h

    un = un_ref[...]
    g_fox = jax.nn.sigmoid(_dot(un, wgf_ref[...]) + bgf_ref[...])
    g_gla = jax.nn.sigmoid(_dot(un, wgg_ref[...]) + bgg_ref[...])
    merged = g_fox * _dot(yf_ref[...], wa_ref[...]) + g_gla * _dot(yg_ref[...], wb_ref[...])
    o_ref[...] += _dot(merged.astype(BF16), wo_ref[...])


def _merge(h, g, y_fox, y_gla, w_gate, b_gate, w_a, w_b, w_o, *, tm, tn, casts=()):
    n, d = h.shape
    nj = d // tn
    cast_specs, cast_shapes = _cast_specs(casts)
    return pl.pallas_call(
        functools.partial(_merge_kernel, lives=tuple(c.live for c in casts)),
        grid=(n // tm, nj),
        in_specs=[
            pl.BlockSpec((tm, d), lambda i, j: (i, 0)),
            pl.BlockSpec((1, d), lambda i, j: (0, 0)),
            pl.BlockSpec((tm, y_fox.shape[1]), lambda i, j: (i, 0)),
            pl.BlockSpec((tm, y_gla.shape[1]), lambda i, j: (i, 0)),
            pl.BlockSpec((d, tn), lambda i, j: (0, j)),
            pl.BlockSpec((d, tn), lambda i, j: (0, nj + j)),
            pl.BlockSpec((1, tn), lambda i, j: (0, j)),
            pl.BlockSpec((1, tn), lambda i, j: (0, nj + j)),
            pl.BlockSpec((w_a.shape[0], tn), lambda i, j: (0, j)),
            pl.BlockSpec((w_b.shape[0], tn), lambda i, j: (0, j)),
            pl.BlockSpec((tn, d), lambda i, j: (j, 0)),
        ] + cast_specs,
        out_specs=[pl.BlockSpec((tm, d), lambda i, j: (i, 0))] + cast_specs,
        out_shape=[jax.ShapeDtypeStruct((n, d), F32)] + cast_shapes,
        scratch_shapes=[pltpu.VMEM((tm, d), BF16)],
        compiler_params=_params("parallel", "arbitrary"),
        name="merge",
    )(h, g, y_fox, y_gla, w_gate, w_gate, b_gate, b_gate, w_a, w_b, w_o, *[c.src for c in casts])


def _ple_kernel(h_ref, p_ref, gp_ref, gf_ref, wpg_ref, wpe_ref, o_ref, *, final):
    h = h_ref[...]
    hn = _rms(h, gp_ref[...]).astype(BF16)
    gate = jax.nn.sigmoid(_dot(hn, wpg_ref[...]))
    out = h + gate * _dot(p_ref[...].astype(BF16), wpe_ref[...])
    if final:
        out = _rms(out, gf_ref[...])
    o_ref[...] = out


def _ple(h, p, g_ple, g_final, w_gate, w_proj, *, tm, final):
    n, d = h.shape
    dp = p.shape[1]
    return pl.pallas_call(
        functools.partial(_ple_kernel, final=final),
        grid=(n // tm,),
        in_specs=[
            pl.BlockSpec((tm, d), lambda i: (i, 0)),
            pl.BlockSpec((tm, dp), lambda i: (i, 0)),
            pl.BlockSpec((1, d), lambda i: (0, 0)),
            pl.BlockSpec((1, d), lambda i: (0, 0)),
            pl.BlockSpec((d, d), lambda i: (0, 0)),
            pl.BlockSpec((dp, d), lambda i: (0, 0)),
        ],
        out_specs=pl.BlockSpec((tm, d), lambda i: (i, 0)),
        out_shape=jax.ShapeDtypeStruct((n, d), F32),
        compiler_params=_params("parallel"),
        name="ple",
    )(h, p, g_ple, g_final, w_gate, w_proj)


def _tile(n, want):
    t = min(n, want)
    assert n % t == 0, (n, want)
    return t


def kernel(x, p, ffn1_norm, ffn1_w_gate, ffn1_w_up, ffn1_w_down, mix_norm, w_in, fox_forget_bias, gla_gate_up, gla_gate_bias, gla_head_norm, w_branch_fox, w_branch_gla, w_merge_gate, b_merge_gate, w_out, ffn2_norm, ffn2_w_gate, ffn2_w_up, ffn2_w_down, ple_norm, w_ple_proj, w_ple_gate, final_norm):
    batch, seq, d = x.shape
    depth = p.shape[0]
    n = batch * seq
    fox_heads = fox_forget_bias.shape[-1]
    fox_dim = w_branch_fox.shape[1] // fox_heads
    fox_w = fox_heads * fox_dim
    rank, gla_kw = gla_gate_up.shape[1:]
    gla_dv = gla_head_norm.shape[-1]
    gla_vw = w_branch_gla.shape[1]
    gla_heads = gla_vw // gla_dv
    gla_dk = gla_kw // gla_heads
    assert fox_heads + rank <= LANES

    tn = 512
    offs = {}
    start = 0
    for name, size in (("fq", fox_w), ("fk", fox_w), ("fv", fox_w), ("fl", fox_heads),
                       ("gq", gla_kw), ("gk", gla_kw), ("gv", gla_vw), ("gr", gla_vw),
                       ("gd", rank)):
        offs[name] = (start, size)
        start += size
    assert start == w_in.shape[-1]
    bf16_order = ("fq", "fk", "fv", "gv", "gq")
    f32_order = ("gr", "gk")
    n_direct_regions = 3
    n_direct_cols = 3 * fox_w
    assert offs["fv"][0] + fox_w == n_direct_cols
    out_col = {}
    pos = 0
    for name in bf16_order:
        out_col[name] = pos
        pos += offs[name][1]
    n_bf16_cols = pos
    pos = 0
    for name in f32_order:
        out_col[name] = pos
        pos += offs[name][1]

    row = lambda v: v.reshape(1, -1).astype(F32)
    h = x.reshape(n, d)
    tm_ffn = _tile(n, 1024)
    tf = _tile(ffn1_w_gate.shape[-1], 512)
    tm_proj = _tile(n, 1024)
    tm_merge = _tile(n, 512)
    tm_ple = _tile(n, 512)
    t_fox = _tile(seq, 512)
    fox_hp = 4
    tb_gla = _tile(seq, 1024)

    rest_names = (bf16_order + f32_order)[n_direct_regions:]
    g0 = min(offs[nm][0] for nm in rest_names)
    g1 = max(offs[nm][0] + offs[nm][1] for nm in rest_names)
    assert g1 - g0 == sum(offs[nm][1] for nm in rest_names)
    rest_tiles = []
    for nm in rest_names:
        assert (offs[nm][0] - g0) % tn == 0 and offs[nm][1] % tn == 0
        rest_tiles += range((offs[nm][0] - g0) // tn, (offs[nm][0] - g0 + offs[nm][1]) // tn)
    ni_ffn, ni_proj, ni_merge = n // tm_ffn, n // tm_proj, n // tm_merge
    nj_ffn = ffn1_w_gate.shape[-1] // tf

    for i in range(depth):
        cols = lambda name: w_in[i][:, offs[name][0]:offs[name][0] + offs[name][1]]
        w_rest = w_in[i][:, g0:g1].astype(BF16)
        w_small = jnp.concatenate(
            [cols("fl"), cols("gd"), jnp.zeros((d, LANES - fox_heads - rank), F32)], axis=1).astype(BF16)
        fbias = jnp.zeros((1, LANES), F32).at[0, :fox_heads].set(fox_forget_bias[i])
        wgu = jnp.zeros((LANES, gla_kw), F32).at[fox_heads:fox_heads + rank].set(gla_gate_up[i]).astype(BF16)

        h, w2g, w2u, w2d = _ffn(
            h, row(ffn1_norm[i]), ffn1_w_gate[i].astype(BF16), ffn1_w_up[i].astype(BF16),
            ffn1_w_down[i].astype(BF16), tm=tm_ffn, tf=tf,
            casts=(_cast_cols(ffn2_w_gate[i], ni_ffn, tf, 0, nj_ffn),
                   _cast_cols(ffn2_w_up[i], ni_ffn, tf, 0, nj_ffn),
                   _cast_rows(ffn2_w_down[i], ni_ffn, tf, 0, nj_ffn)))

        nd = d // tn
        qkv, f32x, small, wmg, wbf, wbg, wo = _proj(
            h, row(mix_norm[i]), w_in[i], w_rest, w_small, tm=tm_proj, tn=tn,
            n_direct_cols=n_direct_cols, rest_tiles=tuple(rest_tiles), n_bf16_cols=n_bf16_cols,
            gq_col=out_col["gq"], gq_scale=gla_dk ** -0.5,
            casts=(_cast_cols(w_merge_gate[i], ni_proj, tn, 0, 2 * nd),
                   _cast_cols(w_branch_fox[i], ni_proj, tn, 0, nd),
                   _cast_cols(w_branch_gla[i], ni_proj, tn, nd, nd),
                   _cast_cols(w_out[i], ni_proj, tn, 2 * nd, nd)))

        fcum = _fcum(small, fbias, batch=batch, seq=seq)
        y_fox = _fox(qkv.reshape(batch, seq, -1), fcum,
                     batch=batch, seq=seq, n_heads=fox_heads, head_dim=fox_dim, t=t_fox, hp=fox_hp)

        y_gla = _gla(qkv.reshape(batch, seq, -1), f32x.reshape(batch, seq, -1),
                     small.reshape(batch, seq, LANES), wgu, row(gla_gate_bias[i]),
                     row(gla_head_norm[i]), batch=batch, seq=seq, n_heads=gla_heads,
                     dk=gla_dk, dv=gla_dv, tb=tb_gla, v_col=out_col["gv"], q_col=out_col["gq"],
                     r_col=out_col["gr"], k_col=out_col["gk"])

        h, wpg, wpe = _merge(
            h, row(mix_norm[i]), y_fox.reshape(n, fox_w), y_gla.reshape(n, gla_vw),
            wmg, row(b_merge_gate[i]), wbf, wbg, wo, tm=tm_merge, tn=tn,
            casts=(_cast_cols(w_ple_gate[i], ni_merge, tn, 0, nd),
                   _cast_cols(w_ple_proj[i], ni_merge, tn, 0, nd)))

        h, = _ffn(h, row(ffn2_norm[i]), w2g, w2u, w2d, tm=tm_ffn, tf=tf)

        h = _ple(h, p[i].reshape(n, -1), row(ple_norm[i]), row(final_norm), wpg, wpe,
                 tm=tm_ple, final=(i == depth - 1))

    return h.reshape(batch, seq, d)
```

```python
import functools
import math
from typing import Callable, NamedTuple

import jax
import jax.numpy as jnp
from jax import lax
from jax.experimental import pallas as pl
from jax.experimental.pallas import tpu as pltpu

EPS = 1e-6
CHUNK = 64
GLA_GATE_TAU = 16.0
LANES = 128
V7X_VMEM_LIMIT_BYTES = 60 * 1024 * 1024
LOG2E = math.log2(math.e)

F32 = jnp.float32
BF16 = jnp.bfloat16


def _rms(x, g):
    return x * lax.rsqrt(jnp.mean(x * x, axis=-1, keepdims=True) + EPS) * g


def _log_sigmoid(z):
    return jnp.minimum(z, 0.0) - jnp.log1p(jnp.exp(-jnp.abs(z)))


def _dot(a, b):
    return lax.dot_general(a, b, (((1,), (0,)), ((), ())), preferred_element_type=F32)


def _params(*sem):
    return pltpu.CompilerParams(dimension_semantics=sem,
                                vmem_limit_bytes=V7X_VMEM_LIMIT_BYTES)


class _Cast(NamedTuple):
    src: jax.Array
    block: tuple
    index: Callable
    live: Callable


def _cast_cols(src, ni, tn, j0, nj):
    r, c = src.shape
    assert r % ni == 0 and (r // ni) % 16 == 0 and c == nj * tn, (src.shape, ni, tn, nj)
    return _Cast(src, (r // ni, tn),
                 lambda i, j: (i, jnp.clip(j - j0, 0, nj - 1)),
                 lambda i, j: (j >= j0) & (j < j0 + nj))


def _cast_rows(src, ni, tn, j0, nj):
    r, c = src.shape
    assert c % ni == 0 and (c // ni) % LANES == 0 and r == nj * tn, (src.shape, ni, tn, nj)
    return _Cast(src, (tn, c // ni),
                 lambda i, j: (jnp.clip(j - j0, 0, nj - 1), i),
                 lambda i, j: (j >= j0) & (j < j0 + nj))


def _cast_specs(casts):
    specs = [pl.BlockSpec(c.block, c.index) for c in casts]
    shapes = [jax.ShapeDtypeStruct(c.src.shape, BF16) for c in casts]
    return specs, shapes


def _run_casts(lives, src_refs, dst_refs):
    i, j = pl.program_id(0), pl.program_id(1)
    for live, src, dst in zip(lives, src_refs, dst_refs):
        @pl.when(live(i, j))
        def _():
            dst[...] = src[...].astype(BF16)


def _ffn_kernel(*refs, lives):
    nc = len(lives)
    x_ref, g_ref, wg_ref, wu_ref, wd_ref = refs[:5]
    o_ref, xn_ref = refs[5 + nc], refs[6 + 2 * nc]

    @pl.when(pl.program_id(1) == 0)
    def _():
        x = x_ref[...]
        xn_ref[...] = _rms(x, g_ref[...]).astype(BF16)
        o_ref[...] = x

    xn = xn_ref[...]
    gate = _dot(xn, wg_ref[...])
    up = _dot(xn, wu_ref[...])
    hid = (0.5 * (gate * jax.nn.sigmoid(gate)) * up).astype(BF16)
    o_ref[...] += _dot(hid, wd_ref[...])
    _run_casts(lives, refs[5:5 + nc], refs[6 + nc:6 + 2 * nc])


def _ffn(x, g, wg, wu, wd, *, tm, tf, casts=()):
    n, d = x.shape
    f = wg.shape[1]
    cast_specs, cast_shapes = _cast_specs(casts)
    return pl.pallas_call(
        functools.partial(_ffn_kernel, lives=tuple(c.live for c in casts)),
        grid=(n // tm, f // tf),
        in_specs=[
            pl.BlockSpec((tm, d), lambda i, j: (i, 0)),
            pl.BlockSpec((1, d), lambda i, j: (0, 0)),
            pl.BlockSpec((d, tf), lambda i, j: (0, j)),
            pl.BlockSpec((d, tf), lambda i, j: (0, j)),
            pl.BlockSpec((tf, d), lambda i, j: (j, 0)),
        ] + cast_specs,
        out_specs=[pl.BlockSpec((tm, d), lambda i, j: (i, 0))] + cast_specs,
        out_shape=[jax.ShapeDtypeStruct((n, d), F32)] + cast_shapes,
        scratch_shapes=[pltpu.VMEM((tm, d), BF16)],
        compiler_params=_params("parallel", "arbitrary"),
        name="ffn",
    )(x, g, wg, wu, wd, *[c.src for c in casts])


class _ShiftedOut(NamedTuple):
    first: int
    count: int
    scale: float


def _shift_left(prev, cur, shift):
    width = prev.shape[1]
    body = pltpu.roll(prev, width - shift, axis=1)
    tail = pltpu.roll(cur[:, :LANES], LANES - shift, axis=1)
    lane = lax.broadcasted_iota(jnp.int32, (prev.shape[0], LANES), 1)
    last = jnp.where(lane < LANES - shift, body[:, width - LANES:], tail)
    return jnp.concatenate([body[:, :width - LANES], last], axis=1)


def _proj_kernel(*refs, lives, lead, outs, shift, rank):
    nc = len(lives)
    h_ref, g_ref, w_ref = refs[:3]
    fox_ref = refs[3 + nc]
    out_refs = refs[4 + nc:4 + nc + len(outs)]
    small_ref = refs[4 + nc + len(outs)]
    un_ref, prev_ref = refs[-2:]
    n_sh = sum(o.count for o in outs)
    j = pl.program_id(1)
    _run_casts(lives, refs[3:3 + nc], refs[5 + nc + len(outs):5 + 2 * nc + len(outs)])

    @pl.when(j == 0)
    def _():
        un_ref[...] = _rms(h_ref[...], g_ref[...]).astype(BF16)
        small_ref[...] = jnp.zeros(small_ref.shape, F32)

    r = _dot(un_ref[...], w_ref[...])

    @pl.when(j < lead)
    def _():
        fox_ref[...] = r.astype(BF16)

    k = j - lead

    @pl.when(k == 0)
    def _():
        small_ref[:, :shift] = r[:, :shift]

    for o_ref, o in zip(out_refs, outs):
        @pl.when((k > o.first) & (k <= o.first + o.count))
        def _():
            tile = _shift_left(prev_ref[...], r, shift)
            if o.scale != 1.0:
                tile = tile * o.scale
            o_ref[...] = tile.astype(o_ref.dtype)

    @pl.when(k == n_sh)
    def _():
        small_ref[:, shift:shift + rank] = r[:, shift:shift + rank]

    @pl.when(k >= 0)
    def _():
        prev_ref[...] = r


def _proj(h, g, w_in, layer, *, tm, tn, lead, outs, out_dtypes, shift, rank, casts=()):
    n, d = h.shape
    n_sh = sum(o.count for o in outs)
    assert shift + rank <= LANES and pl.cdiv(w_in.shape[2], tn) == lead + n_sh + 1
    cast_specs, cast_shapes = _cast_specs(casts)
    kern = functools.partial(_proj_kernel, lives=tuple(c.live for c in casts), lead=lead,
                             outs=outs, shift=shift, rank=rank)

    def out_spec(o):
        return pl.BlockSpec(
            (tm, tn), lambda i, j: (i, jnp.clip(j - (lead + o.first + 1), 0, o.count - 1)))

    return pl.pallas_call(
        kern,
        grid=(n // tm, lead + n_sh + 1),
        in_specs=[
            pl.BlockSpec((tm, d), lambda i, j: (i, 0)),
            pl.BlockSpec((1, d), lambda i, j: (0, 0)),
            pl.BlockSpec((None, d, tn), lambda i, j: (layer, 0, j)),
        ] + cast_specs,
        out_specs=[pl.BlockSpec((tm, tn), lambda i, j: (i, jnp.minimum(j, lead - 1)))]
        + [out_spec(o) for o in outs]
        + [pl.BlockSpec((tm, LANES), lambda i, j: (i, 0))] + cast_specs,
        out_shape=[jax.ShapeDtypeStruct((n, lead * tn), BF16)]
        + [jax.ShapeDtypeStruct((n, o.count * tn), dt) for o, dt in zip(outs, out_dtypes)]
        + [jax.ShapeDtypeStruct((n, LANES), F32)] + cast_shapes,
        scratch_shapes=[pltpu.VMEM((tm, d), BF16), pltpu.VMEM((tm, tn), F32)],
        compiler_params=_params("parallel", "arbitrary"),
        name="proj",
    )(h, g, w_in, *[c.src for c in casts])


def _fcum_kernel(small_ref, bias_ref, f_ref):
    s = small_ref.shape[0]
    f_ref[0] = _log_sigmoid(small_ref[...] + bias_ref[...])
    row = lax.broadcasted_iota(jnp.int32, (LANES, LANES), 0)
    col = lax.broadcasted_iota(jnp.int32, (LANES, LANES), 1)
    tril = (row >= col).astype(F32)

    def body(r, carry):
        rows = pl.ds(pl.multiple_of(r * LANES, LANES), LANES)
        c = jnp.dot(tril, f_ref[0, rows, :], precision=lax.Precision.HIGHEST,
                    preferred_element_type=F32) + carry
        f_ref[0, rows, :] = c
        return c[LANES - 1:LANES, :]

    lax.fori_loop(0, s // LANES, body, jnp.zeros((1, LANES), F32))


def _fcum(small, bias_row, *, batch, seq):
    return pl.pallas_call(
        _fcum_kernel,
        grid=(batch,),
        in_specs=[
            pl.BlockSpec((seq, LANES), lambda b: (b, 0)),
            pl.BlockSpec((1, LANES), lambda b: (0, 0)),
        ],
        out_specs=pl.BlockSpec((1, seq, LANES), lambda b: (b, 0, 0)),
        out_shape=jax.ShapeDtypeStruct((batch, seq, LANES), F32),
        compiler_params=_params("parallel"),
        name="fcum",
    )(small, bias_row)


def _fox_kernel(q_ref, k_ref, v_ref, f_ref, o_ref, frep_ref, vt_ref, m_ref, l_ref, acc_ref,
                s0_ref, s1_ref, mc0_ref, mc1_ref, *, t, c2, hp, dh):
    g = pl.program_id(1)
    i = pl.program_id(2)
    s_refs = (s0_ref, s1_ref)
    mc_refs = (mc0_ref, mc1_ref)

    @pl.when(i == 0)
    def _():
        row = lax.broadcasted_iota(jnp.int32, (LANES, LANES), 0)
        for h in range(hp):
            onehot = (row == g * hp + h).astype(F32)
            frep_ref[h] = jnp.dot(f_ref[0], onehot, precision=lax.Precision.HIGHEST,
                                  preferred_element_type=F32)
            vt_ref[h] = v_ref[0, :, h * dh:(h + 1) * dh].astype(F32).T.astype(BF16)

    q0 = pl.multiple_of(i * t, t)
    m_ref[...] = jnp.full(m_ref.shape, -jnp.inf, F32)
    l_ref[...] = jnp.zeros(l_ref.shape, F32)
    acc_ref[...] = jnp.zeros(acc_ref.shape, F32)
    reps = t // LANES

    def scores(kk, slot, masked):
        k0 = pl.multiple_of(kk * t, t)
        for h in range(hp):
            hs = slice(h * dh, (h + 1) * dh)
            f_base = frep_ref[h, pl.ds(q0, 8), :][0:1, :]
            bias = (f_base - frep_ref[h, pl.ds(k0, t), :]) * LOG2E
            s = lax.dot_general(k_ref[0, pl.ds(k0, t), hs], q_ref[0, :, hs],
                                (((1,), (1,)), ((), ())), preferred_element_type=F32)
            s = s * c2 + jnp.tile(bias, (1, reps))
            if masked:
                key = lax.broadcasted_iota(jnp.int32, (t, t), 0)
                qry = lax.broadcasted_iota(jnp.int32, (t, t), 1)
                s = jnp.where(key <= qry, s, -jnp.inf)
            s_refs[slot][h] = s
            mc_refs[slot][h] = jnp.max(s, axis=0, keepdims=True)

    def absorb(kk, slot):
        k0 = pl.multiple_of(kk * t, t)
        for h in range(hp):
            m_old = m_ref[h]
            m_new = jnp.maximum(m_old, mc_refs[slot][h])
            alpha = jnp.exp2(m_old - m_new)
            p = jnp.exp2(s_refs[slot][h] - m_new)
            l_ref[h] = alpha * l_ref[h] + jnp.sum(p, axis=0, keepdims=True)
            acc_ref[h] = alpha * acc_ref[h] + _dot(vt_ref[h, :, pl.ds(k0, t)], p.astype(BF16))
            m_ref[h] = m_new

    scores(i, 0, True)

    def body(kk, carry):
        prev = jnp.where(kk == 0, i, kk - 1)
        for par in range(2):
            @pl.when(kk % 2 == par)
            def _():
                absorb(prev, par)
                scores(kk, 1 - par, False)
        return carry

    lax.fori_loop(0, i, body, 0)
    last = jnp.where(i == 0, i, i - 1)
    for par in range(2):
        @pl.when(i % 2 == par)
        def _():
            absorb(last, par)
    for h in range(hp):
        o_ref[0, :, h * dh:(h + 1) * dh] = (acc_ref[h] / l_ref[h]).T.astype(o_ref.dtype)


def _fox(qkv, fcum, *, batch, seq, n_heads, head_dim, t, hp):
    assert head_dim == LANES and n_heads % hp == 0
    c2 = (head_dim ** -0.5) * LOG2E
    ng = n_heads // hp
    w = hp * head_dim
    return pl.pallas_call(
        functools.partial(_fox_kernel, t=t, c2=c2, hp=hp, dh=head_dim),
        grid=(batch, ng, seq // t),
        in_specs=[
            pl.BlockSpec((1, t, w), lambda b, g, i: (b, i, g)),
            pl.BlockSpec((1, seq, w), lambda b, g, i: (b, 0, ng + g)),
            pl.BlockSpec((1, seq, w), lambda b, g, i: (b, 0, 2 * ng + g)),
            pl.BlockSpec((1, seq, LANES), lambda b, g, i: (b, 0, 0)),
        ],
        out_specs=pl.BlockSpec((1, t, w), lambda b, g, i: (b, i, g)),
        out_shape=jax.ShapeDtypeStruct((batch, seq, n_heads * head_dim), BF16),
        scratch_shapes=[pltpu.VMEM((hp, seq, LANES), F32), pltpu.VMEM((hp, head_dim, seq), BF16),
                        pltpu.VMEM((hp, 1, t), F32), pltpu.VMEM((hp, 1, t), F32),
                        pltpu.VMEM((hp, head_dim, t), F32),
                        pltpu.VMEM((hp, t, t), F32), pltpu.VMEM((hp, t, t), F32),
                        pltpu.VMEM((hp, 1, t), F32), pltpu.VMEM((hp, 1, t), F32)],
        compiler_params=_params("parallel", "arbitrary", "arbitrary"),
        name="fox",
    )(qkv, qkv, qkv, fcum)


def _gla_kernel(q_ref, k_ref, v_ref, r_ref, small_ref, wgu_ref, gb_ref, gain_ref, o_ref,
                st_ref, la_ref, kd_ref, ea_ref, *, n_heads, dk, dv):
    tb = q_ref.shape[1]
    n_chunks = tb // CHUNK

    @pl.when(pl.program_id(1) == 0)
    def _():
        st_ref[...] = jnp.zeros(st_ref.shape, F32)

    pre = _dot(small_ref[0].astype(BF16), wgu_ref[...]) + gb_ref[...]
    la_ref[...] = _log_sigmoid(pre) * (1.0 / GLA_GATE_TAU)

    row = lax.broadcasted_iota(jnp.int32, (CHUNK, CHUNK), 0)
    col = lax.broadcasted_iota(jnp.int32, (CHUNK, CHUNK), 1)
    after = (col > row).astype(F32)

    def prep(c, carry):
        rows = pl.ds(pl.multiple_of(c * CHUNK, CHUNK), CHUNK)
        la = la_ref[rows, :]
        rev = jnp.dot(after, la, precision=lax.Precision.HIGHEST, preferred_element_type=F32)
        kd_ref[rows, :] = (k_ref[0, rows, :] * jnp.exp(rev)).astype(BF16)
        ea_ref[c] = jnp.exp(jnp.sum(la, axis=0, keepdims=True))
        return carry

    lax.fori_loop(0, n_chunks, prep, 0)

    def step(c, carry):
        rows = pl.ds(pl.multiple_of(c * CHUNK, CHUNK), CHUNK)
        ea = ea_ref[c]
        for h in range(n_heads):
            ks = slice(h * dk, (h + 1) * dk)
            vs = slice(h * dv, (h + 1) * dv)
            kv_t = lax.dot_general(v_ref[0, rows, vs], kd_ref[rows, ks],
                                   (((0,), (0,)), ((), ())), preferred_element_type=F32)
            st = st_ref[h] * ea[:, ks] + kv_t
            st_ref[h] = st
            o = lax.dot_general(q_ref[0, rows, ks], st.astype(BF16),
                                (((1,), (1,)), ((), ())), preferred_element_type=F32)
            on = _rms(o, gain_ref[...])
            r = r_ref[0, rows, vs]
            o_ref[0, rows, vs] = (on * (r * jax.nn.sigmoid(r))).astype(o_ref.dtype)
        return carry

    lax.fori_loop(0, n_chunks, step, 0)


def _gla(q, k, v, r, small, wgu, gate_bias, gain, *, batch, seq, n_heads, dk, dv, tb):
    kw, vw = n_heads * dk, n_heads * dv
    return pl.pallas_call(
        functools.partial(_gla_kernel, n_heads=n_heads, dk=dk, dv=dv),
        grid=(batch, seq // tb),
        in_specs=[
            pl.BlockSpec((1, tb, kw), lambda b, t: (b, t, 0)),
            pl.BlockSpec((1, tb, kw), lambda b, t: (b, t, 0)),
            pl.BlockSpec((1, tb, vw), lambda b, t: (b, t, 0)),
            pl.BlockSpec((1, tb, vw), lambda b, t: (b, t, 0)),
            pl.BlockSpec((1, tb, LANES), lambda b, t: (b, t, 0)),
            pl.BlockSpec((LANES, kw), lambda b, t: (0, 0)),
            pl.BlockSpec((1, kw), lambda b, t: (0, 0)),
            pl.BlockSpec((1, dv), lambda b, t: (0, 0)),
        ],
        out_specs=pl.BlockSpec((1, tb, vw), lambda b, t: (b, t, 0)),
        out_shape=jax.ShapeDtypeStruct((batch, seq, vw), BF16),
        scratch_shapes=[
            pltpu.VMEM((n_heads, dv, dk), F32),
            pltpu.VMEM((tb, kw), F32),
            pltpu.VMEM((tb, kw), BF16),
            pltpu.VMEM((tb // CHUNK, 1, kw), F32),
        ],
        compiler_params=_params("parallel", "arbitrary"),
        name="gla",
    )(q, k, v, r, small, wgu, gate_bias, gain)


def _merge_kernel(*refs, lives):
    nc = len(lives)
    (h_ref, g_ref, yf_ref, yg_ref, wgf_ref, wgg_ref, bgf_ref, bgg_ref,
     wa_ref, wb_ref, wo_ref) = refs[:11]
    o_ref, un_ref = refs[11 + nc], refs[12 + 2 * nc]
    _run_casts(lives, refs[11:11 + nc], refs[12 + nc:12 + 2 * nc])

    @pl.when(pl.program_id(1) == 0)
    def _():
        h = h_ref[...]
        un_ref[...] = _rms(h, g_ref[...]).astype(BF16)
        o_ref[...] = h

    un = un_ref[...]
    g_fox = jax.nn.sigmoid(_dot(un, wgf_ref[...]) + bgf_ref[...])
    g_gla = jax.nn.sigmoid(_dot(un, wgg_ref[...]) + bgg_ref[...])
    merged = g_fox * _dot(yf_ref[...], wa_ref[...]) + g_gla * _dot(yg_ref[...], wb_ref[...])
    o_ref[...] += _dot(merged.astype(BF16), wo_ref[...])


def _merge(h, g, y_fox, y_gla, w_gate, b_gate, w_a, w_b, w_o, *, tm, tn, casts=()):
    n, d = h.shape
    nj = d // tn
    cast_specs, cast_shapes = _cast_specs(casts)
    return pl.pallas_call(
        functools.partial(_merge_kernel, lives=tuple(c.live for c in casts)),
        grid=(n // tm, nj),
        in_specs=[
            pl.BlockSpec((tm, d), lambda i, j: (i, 0)),
            pl.BlockSpec((1, d), lambda i, j: (0, 0)),
            pl.BlockSpec((tm, y_fox.shape[1]), lambda i, j: (i, 0)),
            pl.BlockSpec((tm, y_gla.shape[1]), lambda i, j: (i, 0)),
            pl.BlockSpec((d, tn), lambda i, j: (0, j)),
            pl.BlockSpec((d, tn), lambda i, j: (0, nj + j)),
            pl.BlockSpec((1, tn), lambda i, j: (0, j)),
            pl.BlockSpec((1, tn), lambda i, j: (0, nj + j)),
            pl.BlockSpec((w_a.shape[0], tn), lambda i, j: (0, j)),
            pl.BlockSpec((w_b.shape[0], tn), lambda i, j: (0, j)),
            pl.BlockSpec((tn, d), lambda i, j: (j, 0)),
        ] + cast_specs,
        out_specs=[pl.BlockSpec((tm, d), lambda i, j: (i, 0))] + cast_specs,
        out_shape=[jax.ShapeDtypeStruct((n, d), F32)] + cast_shapes,
        scratch_shapes=[pltpu.VMEM((tm, d), BF16)],
        compiler_params=_params("parallel", "arbitrary"),
        name="merge",
    )(h, g, y_fox, y_gla, w_gate, w_gate, b_gate, b_gate, w_a, w_b, w_o, *[c.src for c in casts])


def _ple_kernel(h_ref, p_ref, gp_ref, gf_ref, wpg_ref, wpe_ref, o_ref, *, final):
    h = h_ref[...]
    hn = _rms(h, gp_ref[...]).astype(BF16)
    gate = jax.nn.sigmoid(_dot(hn, wpg_ref[...]))
    out = h + gate * _dot(p_ref[...].astype(BF16), wpe_ref[...])
    if final:
        out = _rms(out, gf_ref[...])
    o_ref[...] = out


def _ple(h, p, g_ple, g_final, w_gate, w_proj, *, tm, final):
    n, d = h.shape
    dp = p.shape[1]
    return pl.pallas_call(
        functools.partial(_ple_kernel, final=final),
        grid=(n // tm,),
        in_specs=[
            pl.BlockSpec((tm, d), lambda i: (i, 0)),
            pl.BlockSpec((tm, dp), lambda i: (i, 0)),
            pl.BlockSpec((1, d), lambda i: (0, 0)),
            pl.BlockSpec((1, d), lambda i: (0, 0)),
            pl.BlockSpec((d, d), lambda i: (0, 0)),
            pl.BlockSpec((dp, d), lambda i: (0, 0)),
        ],
        out_specs=pl.BlockSpec((tm, d), lambda i: (i, 0)),
        out_shape=jax.ShapeDtypeStruct((n, d), F32),
        compiler_params=_params("parallel"),
        name="ple",
    )(h, p, g_ple, g_final, w_gate, w_proj)


def _tile(n, want):
    t = min(n, want)
    assert n % t == 0, (n, want)
    return t


def kernel(x, p, ffn1_norm, ffn1_w_gate, ffn1_w_up, ffn1_w_down, mix_norm, w_in, fox_forget_bias, gla_gate_up, gla_gate_bias, gla_head_norm, w_branch_fox, w_branch_gla, w_merge_gate, b_merge_gate, w_out, ffn2_norm, ffn2_w_gate, ffn2_w_up, ffn2_w_down, ple_norm, w_ple_proj, w_ple_gate, final_norm):
    batch, seq, d = x.shape
    depth = p.shape[0]
    n = batch * seq
    fox_heads = fox_forget_bias.shape[-1]
    fox_dim = w_branch_fox.shape[1] // fox_heads
    fox_w = fox_heads * fox_dim
    rank, gla_kw = gla_gate_up.shape[1:]
    gla_dv = gla_head_norm.shape[-1]
    gla_vw = w_branch_gla.shape[1]
    gla_heads = gla_vw // gla_dv
    gla_dk = gla_kw // gla_heads
    assert fox_heads + rank <= LANES

    tn = 512
    assert w_in.shape[-1] == 3 * fox_w + fox_heads + 2 * gla_kw + 2 * gla_vw + rank
    assert (3 * fox_w) % tn == 0 and gla_kw % tn == 0 and gla_vw % tn == 0
    proj_lead = 3 * fox_w // tn
    kt, vt = gla_kw // tn, gla_vw // tn
    proj_outs = (_ShiftedOut(0, kt, gla_dk ** -0.5),
                 _ShiftedOut(kt, kt, 1.0),
                 _ShiftedOut(2 * kt, vt, 1.0),
                 _ShiftedOut(2 * kt + vt, vt, 1.0))
    proj_dtypes = (BF16, F32, BF16, F32)

    row = lambda v: v.reshape(1, -1).astype(F32)
    h = x.reshape(n, d)
    tm_ffn = _tile(n, 1024)
    tf = _tile(ffn1_w_gate.shape[-1], 512)
    tm_proj = _tile(n, 1024)
    tm_merge = _tile(n, 512)
    tm_ple = _tile(n, 512)
    t_fox = _tile(seq, 512)
    fox_hp = 4
    tb_gla = _tile(seq, 1024)

    ni_ffn, ni_proj, ni_merge = n // tm_ffn, n // tm_proj, n // tm_merge
    nj_ffn = ffn1_w_gate.shape[-1] // tf

    for i in range(depth):
        fbias = jnp.zeros((1, LANES), F32).at[0, :fox_heads].set(fox_forget_bias[i])
        wgu = jnp.zeros((LANES, gla_kw), F32).at[fox_heads:fox_heads + rank].set(gla_gate_up[i]).astype(BF16)

        h, w2g, w2u, w2d = _ffn(
            h, row(ffn1_norm[i]), ffn1_w_gate[i].astype(BF16), ffn1_w_up[i].astype(BF16),
            ffn1_w_down[i].astype(BF16), tm=tm_ffn, tf=tf,
            casts=(_cast_cols(ffn2_w_gate[i], ni_ffn, tf, 0, nj_ffn),
                   _cast_cols(ffn2_w_up[i], ni_ffn, tf, 0, nj_ffn),
                   _cast_rows(ffn2_w_down[i], ni_ffn, tf, 0, nj_ffn)))

        nd = d // tn
        fox_qkv, gq, gk, gv, gr, small, wmg, wbf, wbg, wo = _proj(
            h, row(mix_norm[i]), w_in, i, tm=tm_proj, tn=tn, lead=proj_lead, outs=proj_outs,
            out_dtypes=proj_dtypes, shift=fox_heads, rank=rank,
            casts=(_cast_cols(w_merge_gate[i], ni_proj, tn, 0, 2 * nd),
                   _cast_cols(w_branch_fox[i], ni_proj, tn, 0, nd),
                   _cast_cols(w_branch_gla[i], ni_proj, tn, nd, nd),
                   _cast_cols(w_out[i], ni_proj, tn, 2 * nd, nd)))

        fcum = _fcum(small, fbias, batch=batch, seq=seq)
        y_fox = _fox(fox_qkv.reshape(batch, seq, -1), fcum,
                     batch=batch, seq=seq, n_heads=fox_heads, head_dim=fox_dim, t=t_fox, hp=fox_hp)

        bsd = lambda a: a.reshape(batch, seq, -1)
        y_gla = _gla(bsd(gq), bsd(gk), bsd(gv), bsd(gr), bsd(small), wgu, row(gla_gate_bias[i]),
                     row(gla_head_norm[i]), batch=batch, seq=seq, n_heads=gla_heads,
                     dk=gla_dk, dv=gla_dv, tb=tb_gla)

        h, wpg, wpe = _merge(
            h, row(mix_norm[i]), y_fox.reshape(n, fox_w), y_gla.reshape(n, gla_vw),
            wmg, row(b_merge_gate[i]), wbf, wbg, wo, tm=tm_merge, tn=tn,
            casts=(_cast_cols(w_ple_gate[i], ni_merge, tn, 0, nd),
                   _cast_cols(w_ple_proj[i], ni_merge, tn, 0, nd)))

        h, = _ffn(h, row(ffn2_norm[i]), w2g, w2u, w2d, tm=tm_ffn, tf=tf)

        h = _ple(h, p[i].reshape(n, -1), row(ple_norm[i]), row(final_norm), wpg, wpe,
                 tm=tm_ple, final=(i == depth - 1))

    return h.reshape(batch, seq, d)
```

```python
import functools
import math
from typing import Callable, NamedTuple

import jax
import jax.numpy as jnp
from jax import lax
from jax.experimental import pallas as pl
from jax.experimental.pallas import tpu as pltpu

EPS = 1e-6
CHUNK = 64
GLA_GATE_TAU = 16.0
LANES = 128
V7X_VMEM_LIMIT_BYTES = 60 * 1024 * 1024
LOG2E = math.log2(math.e)

F32 = jnp.float32
BF16 = jnp.bfloat16


def _rms(x, g):
    return x * lax.rsqrt(jnp.mean(x * x, axis=-1, keepdims=True) + EPS) * g


def _log_sigmoid(z):
    return jnp.minimum(z, 0.0) - jnp.log1p(jnp.exp(-jnp.abs(z)))


def _dot(a, b):
    return lax.dot_general(a, b, (((1,), (0,)), ((), ())), preferred_element_type=F32)


def _params(*sem):
    return pltpu.CompilerParams(dimension_semantics=sem,
                                vmem_limit_bytes=V7X_VMEM_LIMIT_BYTES)


class _Cast(NamedTuple):
    src: jax.Array
    block: tuple
    index: Callable
    live: Callable


def _cast_cols(src, ni, tn, j0, nj):
    r, c = src.shape
    assert r % ni == 0 and (r // ni) % 16 == 0 and c == nj * tn, (src.shape, ni, tn, nj)
    return _Cast(src, (r // ni, tn),
                 lambda i, j: (i, jnp.clip(j - j0, 0, nj - 1)),
                 lambda i, j: (j >= j0) & (j < j0 + nj))


def _cast_rows(src, ni, tn, j0, nj):
    r, c = src.shape
    assert c % ni == 0 and (c // ni) % LANES == 0 and r == nj * tn, (src.shape, ni, tn, nj)
    return _Cast(src, (tn, c // ni),
                 lambda i, j: (jnp.clip(j - j0, 0, nj - 1), i),
                 lambda i, j: (j >= j0) & (j < j0 + nj))


def _cast_specs(casts):
    specs = [pl.BlockSpec(c.block, c.index) for c in casts]
    shapes = [jax.ShapeDtypeStruct(c.src.shape, BF16) for c in casts]
    return specs, shapes


def _run_casts(lives, src_refs, dst_refs):
    i, j = pl.program_id(0), pl.program_id(1)
    for live, src, dst in zip(lives, src_refs, dst_refs):
        @pl.when(live(i, j))
        def _():
            dst[...] = src[...].astype(BF16)


def _ffn_kernel(*refs, lives):
    nc = len(lives)
    x_ref, g_ref, wg_ref, wu_ref, wd_ref = refs[:5]
    o_ref, xn_ref = refs[5 + nc], refs[6 + 2 * nc]

    @pl.when(pl.program_id(1) == 0)
    def _():
        x = x_ref[...]
        xn_ref[...] = _rms(x, g_ref[...]).astype(BF16)
        o_ref[...] = x

    xn = xn_ref[...]
    gate = _dot(xn, wg_ref[...])
    up = _dot(xn, wu_ref[...])
    hid = (0.5 * (gate * jax.nn.sigmoid(gate)) * up).astype(BF16)
    o_ref[...] += _dot(hid, wd_ref[...])
    _run_casts(lives, refs[5:5 + nc], refs[6 + nc:6 + 2 * nc])


def _ffn(x, g, wg, wu, wd, *, tm, tf, casts=()):
    n, d = x.shape
    f = wg.shape[1]
    cast_specs, cast_shapes = _cast_specs(casts)
    return pl.pallas_call(
        functools.partial(_ffn_kernel, lives=tuple(c.live for c in casts)),
        grid=(n // tm, f // tf),
        in_specs=[
            pl.BlockSpec((tm, d), lambda i, j: (i, 0)),
            pl.BlockSpec((1, d), lambda i, j: (0, 0)),
            pl.BlockSpec((d, tf), lambda i, j: (0, j)),
            pl.BlockSpec((d, tf), lambda i, j: (0, j)),
            pl.BlockSpec((tf, d), lambda i, j: (j, 0)),
        ] + cast_specs,
        out_specs=[pl.BlockSpec((tm, d), lambda i, j: (i, 0))] + cast_specs,
        out_shape=[jax.ShapeDtypeStruct((n, d), F32)] + cast_shapes,
        scratch_shapes=[pltpu.VMEM((tm, d), BF16)],
        compiler_params=_params("parallel", "arbitrary"),
        name="ffn",
    )(x, g, wg, wu, wd, *[c.src for c in casts])


class _ProjOut(NamedTuple):
    first: int
    count: int
    scale: float
    dtype: type


def _dot_nt(a, b):
    return lax.dot_general(a, b, (((1,), (1,)), ((), ())), preferred_element_type=F32)


def _proj_kernel(*refs, lives, outs, n_fl, rank):
    nc = len(lives)
    h_ref, g_ref, w_ref, wfl_ref, wgd_ref = refs[:5]
    out_refs = refs[5 + nc:5 + nc + len(outs)]
    small_ref = refs[5 + nc + len(outs)]
    un_ref = refs[-1]
    j = pl.program_id(1)
    _run_casts(lives, refs[5:5 + nc], refs[6 + nc + len(outs):6 + 2 * nc + len(outs)])

    @pl.when(j == 0)
    def _():
        un = _rms(h_ref[...], g_ref[...]).astype(BF16)
        un_ref[...] = un
        small_ref[...] = jnp.zeros(small_ref.shape, F32)
        small_ref[:, :n_fl] = _dot_nt(un, wfl_ref[...])
        small_ref[:, n_fl:n_fl + rank] = _dot_nt(un, wgd_ref[...])

    for o_ref, o in zip(out_refs, outs):
        @pl.when((j >= o.first) & (j < o.first + o.count))
        def _():
            tile = _dot_nt(un_ref[...], w_ref[...])
            if o.scale != 1.0:
                tile = tile * o.scale
            o_ref[...] = tile.astype(o_ref.dtype)


def _proj(h, g, w_in_t, layer, *, tm, tn, outs, row_start, fl_rows, gd_rows, casts=()):
    n, d = h.shape
    n_steps = sum(o.count for o in outs)
    assert fl_rows[1] + gd_rows[1] <= LANES
    cast_specs, cast_shapes = _cast_specs(casts)
    kern = functools.partial(_proj_kernel, lives=tuple(c.live for c in casts), outs=outs,
                             n_fl=fl_rows[1], rank=gd_rows[1])
    rows = lambda start, size: pl.BlockSpec(
        (None, pl.Element(size), pl.Element(d)), lambda i, j: (layer, start, 0))

    def out_spec(o):
        return pl.BlockSpec((tm, tn), lambda i, j: (i, jnp.clip(j - o.first, 0, o.count - 1)))

    return pl.pallas_call(
        kern,
        grid=(n // tm, n_steps),
        in_specs=[
            pl.BlockSpec((tm, d), lambda i, j: (i, 0)),
            pl.BlockSpec((1, d), lambda i, j: (0, 0)),
            pl.BlockSpec((None, pl.Element(tn), pl.Element(d)),
                         lambda i, j: (layer, row_start(j), 0)),
            rows(*fl_rows),
            rows(*gd_rows),
        ] + cast_specs,
        out_specs=[out_spec(o) for o in outs]
        + [pl.BlockSpec((tm, LANES), lambda i, j: (i, 0))] + cast_specs,
        out_shape=[jax.ShapeDtypeStruct((n, o.count * tn), o.dtype) for o in outs]
        + [jax.ShapeDtypeStruct((n, LANES), F32)] + cast_shapes,
        scratch_shapes=[pltpu.VMEM((tm, d), BF16)],
        compiler_params=_params("parallel", "arbitrary"),
        name="proj",
    )(h, g, w_in_t, w_in_t, w_in_t, *[c.src for c in casts])


def _fcum_kernel(small_ref, bias_ref, f_ref):
    s = small_ref.shape[0]
    f_ref[0] = _log_sigmoid(small_ref[...] + bias_ref[...])
    row = lax.broadcasted_iota(jnp.int32, (LANES, LANES), 0)
    col = lax.broadcasted_iota(jnp.int32, (LANES, LANES), 1)
    tril = (row >= col).astype(F32)

    def body(r, carry):
        rows = pl.ds(pl.multiple_of(r * LANES, LANES), LANES)
        c = jnp.dot(tril, f_ref[0, rows, :], precision=lax.Precision.HIGHEST,
                    preferred_element_type=F32) + carry
        f_ref[0, rows, :] = c
        return c[LANES - 1:LANES, :]

    lax.fori_loop(0, s // LANES, body, jnp.zeros((1, LANES), F32))


def _fcum(small, bias_row, *, batch, seq):
    return pl.pallas_call(
        _fcum_kernel,
        grid=(batch,),
        in_specs=[
            pl.BlockSpec((seq, LANES), lambda b: (b, 0)),
            pl.BlockSpec((1, LANES), lambda b: (0, 0)),
        ],
        out_specs=pl.BlockSpec((1, seq, LANES), lambda b: (b, 0, 0)),
        out_shape=jax.ShapeDtypeStruct((batch, seq, LANES), F32),
        compiler_params=_params("parallel"),
        name="fcum",
    )(small, bias_row)


def _fox_kernel(q_ref, k_ref, v_ref, f_ref, o_ref, frep_ref, vt_ref, m_ref, l_ref, acc_ref,
                s0_ref, s1_ref, mc0_ref, mc1_ref, *, t, c2, hp, dh):
    g = pl.program_id(1)
    i = pl.program_id(2)
    s_refs = (s0_ref, s1_ref)
    mc_refs = (mc0_ref, mc1_ref)

    @pl.when(i == 0)
    def _():
        row = lax.broadcasted_iota(jnp.int32, (LANES, LANES), 0)
        for h in range(hp):
            onehot = (row == g * hp + h).astype(F32)
            frep_ref[h] = jnp.dot(f_ref[0], onehot, precision=lax.Precision.HIGHEST,
                                  preferred_element_type=F32)
            vt_ref[h] = v_ref[0, :, h * dh:(h + 1) * dh].astype(F32).T.astype(BF16)

    q0 = pl.multiple_of(i * t, t)
    m_ref[...] = jnp.full(m_ref.shape, -jnp.inf, F32)
    l_ref[...] = jnp.zeros(l_ref.shape, F32)
    acc_ref[...] = jnp.zeros(acc_ref.shape, F32)
    reps = t // LANES

    def scores(kk, slot, masked):
        k0 = pl.multiple_of(kk * t, t)
        for h in range(hp):
            hs = slice(h * dh, (h + 1) * dh)
            f_base = frep_ref[h, pl.ds(q0, 8), :][0:1, :]
            bias = (f_base - frep_ref[h, pl.ds(k0, t), :]) * LOG2E
            s = lax.dot_general(k_ref[0, pl.ds(k0, t), hs], q_ref[0, :, hs],
                                (((1,), (1,)), ((), ())), preferred_element_type=F32)
            s = s * c2 + jnp.tile(bias, (1, reps))
            if masked:
                key = lax.broadcasted_iota(jnp.int32, (t, t), 0)
                qry = lax.broadcasted_iota(jnp.int32, (t, t), 1)
                s = jnp.where(key <= qry, s, -jnp.inf)
            s_refs[slot][h] = s
            mc_refs[slot][h] = jnp.max(s, axis=0, keepdims=True)

    def absorb(kk, slot):
        k0 = pl.multiple_of(kk * t, t)
        for h in range(hp):
            m_old = m_ref[h]
            m_new = jnp.maximum(m_old, mc_refs[slot][h])
            alpha = jnp.exp2(m_old - m_new)
            p = jnp.exp2(s_refs[slot][h] - m_new)
            l_ref[h] = alpha * l_ref[h] + jnp.sum(p, axis=0, keepdims=True)
            acc_ref[h] = alpha * acc_ref[h] + _dot(vt_ref[h, :, pl.ds(k0, t)], p.astype(BF16))
            m_ref[h] = m_new

    scores(i, 0, True)

    def body(kk, carry):
        prev = jnp.where(kk == 0, i, kk - 1)
        for par in range(2):
            @pl.when(kk % 2 == par)
            def _():
                absorb(prev, par)
                scores(kk, 1 - par, False)
        return carry

    lax.fori_loop(0, i, body, 0)
    last = jnp.where(i == 0, i, i - 1)
    for par in range(2):
        @pl.when(i % 2 == par)
        def _():
            absorb(last, par)
    for h in range(hp):
        o_ref[0, :, h * dh:(h + 1) * dh] = (acc_ref[h] / l_ref[h]).T.astype(o_ref.dtype)


def _fox(qkv, fcum, *, batch, seq, n_heads, head_dim, t, hp):
    assert head_dim == LANES and n_heads % hp == 0
    c2 = (head_dim ** -0.5) * LOG2E
    ng = n_heads // hp
    w = hp * head_dim
    return pl.pallas_call(
        functools.partial(_fox_kernel, t=t, c2=c2, hp=hp, dh=head_dim),
        grid=(batch, ng, seq // t),
        in_specs=[
            pl.BlockSpec((1, t, w), lambda b, g, i: (b, i, g)),
            pl.BlockSpec((1, seq, w), lambda b, g, i: (b, 0, ng + g)),
            pl.BlockSpec((1, seq, w), lambda b, g, i: (b, 0, 2 * ng + g)),
            pl.BlockSpec((1, seq, LANES), lambda b, g, i: (b, 0, 0)),
        ],
        out_specs=pl.BlockSpec((1, t, w), lambda b, g, i: (b, i, g)),
        out_shape=jax.ShapeDtypeStruct((batch, seq, n_heads * head_dim), BF16),
        scratch_shapes=[pltpu.VMEM((hp, seq, LANES), F32), pltpu.VMEM((hp, head_dim, seq), BF16),
                        pltpu.VMEM((hp, 1, t), F32), pltpu.VMEM((hp, 1, t), F32),
                        pltpu.VMEM((hp, head_dim, t), F32),
                        pltpu.VMEM((hp, t, t), F32), pltpu.VMEM((hp, t, t), F32),
                        pltpu.VMEM((hp, 1, t), F32), pltpu.VMEM((hp, 1, t), F32)],
        compiler_params=_params("parallel", "arbitrary", "arbitrary"),
        name="fox",
    )(qkv, qkv, qkv, fcum)


def _gla_kernel(q_ref, k_ref, v_ref, r_ref, small_ref, wgu_ref, gb_ref, gain_ref, o_ref,
                st_ref, la_ref, kd_ref, ea_ref, *, n_heads, dk, dv):
    tb = q_ref.shape[1]
    n_chunks = tb // CHUNK

    @pl.when(pl.program_id(1) == 0)
    def _():
        st_ref[...] = jnp.zeros(st_ref.shape, F32)

    pre = _dot(small_ref[0].astype(BF16), wgu_ref[...]) + gb_ref[...]
    la_ref[...] = _log_sigmoid(pre) * (1.0 / GLA_GATE_TAU)

    row = lax.broadcasted_iota(jnp.int32, (CHUNK, CHUNK), 0)
    col = lax.broadcasted_iota(jnp.int32, (CHUNK, CHUNK), 1)
    after = (col > row).astype(F32)

    def prep(c, carry):
        rows = pl.ds(pl.multiple_of(c * CHUNK, CHUNK), CHUNK)
        la = la_ref[rows, :]
        rev = jnp.dot(after, la, precision=lax.Precision.HIGHEST, preferred_element_type=F32)
        kd_ref[rows, :] = (k_ref[0, rows, :] * jnp.exp(rev)).astype(BF16)
        ea_ref[c] = jnp.exp(jnp.sum(la, axis=0, keepdims=True))
        return carry

    lax.fori_loop(0, n_chunks, prep, 0)

    def step(c, carry):
        rows = pl.ds(pl.multiple_of(c * CHUNK, CHUNK), CHUNK)
        ea = ea_ref[c]
        for h in range(n_heads):
            ks = slice(h * dk, (h + 1) * dk)
            vs = slice(h * dv, (h + 1) * dv)
            kv_t = lax.dot_general(v_ref[0, rows, vs], kd_ref[rows, ks],
                                   (((0,), (0,)), ((), ())), preferred_element_type=F32)
            st = st_ref[h] * ea[:, ks] + kv_t
            st_ref[h] = st
            o = lax.dot_general(q_ref[0, rows, ks], st.astype(BF16),
                                (((1,), (1,)), ((), ())), preferred_element_type=F32)
            on = _rms(o, gain_ref[...])
            r = r_ref[0, rows, vs]
            o_ref[0, rows, vs] = (on * (r * jax.nn.sigmoid(r))).astype(o_ref.dtype)
        return carry

    lax.fori_loop(0, n_chunks, step, 0)


def _gla(q, k, v, r, small, wgu, gate_bias, gain, *, batch, seq, n_heads, dk, dv, tb):
    kw, vw = n_heads * dk, n_heads * dv
    return pl.pallas_call(
        functools.partial(_gla_kernel, n_heads=n_heads, dk=dk, dv=dv),
        grid=(batch, seq // tb),
        in_specs=[
            pl.BlockSpec((1, tb, kw), lambda b, t: (b, t, 0)),
            pl.BlockSpec((1, tb, kw), lambda b, t: (b, t, 0)),
            pl.BlockSpec((1, tb, vw), lambda b, t: (b, t, 0)),
            pl.BlockSpec((1, tb, vw), lambda b, t: (b, t, 0)),
            pl.BlockSpec((1, tb, LANES), lambda b, t: (b, t, 0)),
            pl.BlockSpec((LANES, kw), lambda b, t: (0, 0)),
            pl.BlockSpec((1, kw), lambda b, t: (0, 0)),
            pl.BlockSpec((1, dv), lambda b, t: (0, 0)),
        ],
        out_specs=pl.BlockSpec((1, tb, vw), lambda b, t: (b, t, 0)),
        out_shape=jax.ShapeDtypeStruct((batch, seq, vw), BF16),
        scratch_shapes=[
            pltpu.VMEM((n_heads, dv, dk), F32),
            pltpu.VMEM((tb, kw), F32),
            pltpu.VMEM((tb, kw), BF16),
            pltpu.VMEM((tb // CHUNK, 1, kw), F32),
        ],
        compiler_params=_params("parallel", "arbitrary"),
        name="gla",
    )(q, k, v, r, small, wgu, gate_bias, gain)


def _merge_kernel(*refs, lives):
    nc = len(lives)
    (h_ref, g_ref, yf_ref, yg_ref, wgf_ref, wgg_ref, bgf_ref, bgg_ref,
     wa_ref, wb_ref, wo_ref) = refs[:11]
    o_ref, un_ref = refs[11 + nc], refs[12 + 2 * nc]
    _run_casts(lives, refs[11:11 + nc], refs[12 + nc:12 + 2 * nc])

    @pl.when(pl.program_id(1) == 0)
    def _():
        h = h_ref[...]
        un_ref[...] = _rms(h, g_ref[...]).astype(BF16)
        o_ref[...] = h

    un = un_ref[...]
    g_fox = jax.nn.sigmoid(_dot(un, wgf_ref[...]) + bgf_ref[...])
    g_gla = jax.nn.sigmoid(_dot(un, wgg_ref[...]) + bgg_ref[...])
    merged = g_fox * _dot(yf_ref[...], wa_ref[...]) + g_gla * _dot(yg_ref[...], wb_ref[...])
    o_ref[...] += _dot(merged.astype(BF16), wo_ref[...])


def _merge(h, g, y_fox, y_gla, w_gate, b_gate, w_a, w_b, w_o, *, tm, tn, casts=()):
    n, d = h.shape
    nj = d // tn
    cast_specs, cast_shapes = _cast_specs(casts)
    return pl.pallas_call(
        functools.partial(_merge_kernel, lives=tuple(c.live for c in casts)),
        grid=(n // tm, nj),
        in_specs=[
            pl.BlockSpec((tm, d), lambda i, j: (i, 0)),
            pl.BlockSpec((1, d), lambda i, j: (0, 0)),
            pl.BlockSpec((tm, y_fox.shape[1]), lambda i, j: (i, 0)),
            pl.BlockSpec((tm, y_gla.shape[1]), lambda i, j: (i, 0)),
            pl.BlockSpec((d, tn), lambda i, j: (0, j)),
            pl.BlockSpec((d, tn), lambda i, j: (0, nj + j)),
            pl.BlockSpec((1, tn), lambda i, j: (0, j)),
            pl.BlockSpec((1, tn), lambda i, j: (0, nj + j)),
            pl.BlockSpec((w_a.shape[0], tn), lambda i, j: (0, j)),
            pl.BlockSpec((w_b.shape[0], tn), lambda i, j: (0, j)),
            pl.BlockSpec((tn, d), lambda i, j: (j, 0)),
        ] + cast_specs,
        out_specs=[pl.BlockSpec((tm, d), lambda i, j: (i, 0))] + cast_specs,
        out_shape=[jax.ShapeDtypeStruct((n, d), F32)] + cast_shapes,
        scratch_shapes=[pltpu.VMEM((tm, d), BF16)],
        compiler_params=_params("parallel", "arbitrary"),
        name="merge",
    )(h, g, y_fox, y_gla, w_gate, w_gate, b_gate, b_gate, w_a, w_b, w_o, *[c.src for c in casts])


def _ple_kernel(h_ref, p_ref, gp_ref, gf_ref, wpg_ref, wpe_ref, o_ref, *, final):
    h = h_ref[...]
    hn = _rms(h, gp_ref[...]).astype(BF16)
    gate = jax.nn.sigmoid(_dot(hn, wpg_ref[...]))
    out = h + gate * _dot(p_ref[...].astype(BF16), wpe_ref[...])
    if final:
        out = _rms(out, gf_ref[...])
    o_ref[...] = out


def _ple(h, p, g_ple, g_final, w_gate, w_proj, *, tm, final):
    n, d = h.shape
    dp = p.shape[1]
    return pl.pallas_call(
        functools.partial(_ple_kernel, final=final),
        grid=(n // tm,),
        in_specs=[
            pl.BlockSpec((tm, d), lambda i: (i, 0)),
            pl.BlockSpec((tm, dp), lambda i: (i, 0)),
            pl.BlockSpec((1, d), lambda i: (0, 0)),
            pl.BlockSpec((1, d), lambda i: (0, 0)),
            pl.BlockSpec((d, d), lambda i: (0, 0)),
            pl.BlockSpec((dp, d), lambda i: (0, 0)),
        ],
        out_specs=pl.BlockSpec((tm, d), lambda i: (i, 0)),
        out_shape=jax.ShapeDtypeStruct((n, d), F32),
        compiler_params=_params("parallel"),
        name="ple",
    )(h, p, g_ple, g_final, w_gate, w_proj)


def _tile(n, want):
    t = min(n, want)
    assert n % t == 0, (n, want)
    return t


def kernel(x, p, ffn1_norm, ffn1_w_gate, ffn1_w_up, ffn1_w_down, mix_norm, w_in, fox_forget_bias, gla_gate_up, gla_gate_bias, gla_head_norm, w_branch_fox, w_branch_gla, w_merge_gate, b_merge_gate, w_out, ffn2_norm, ffn2_w_gate, ffn2_w_up, ffn2_w_down, ple_norm, w_ple_proj, w_ple_gate, final_norm):
    batch, seq, d = x.shape
    depth = p.shape[0]
    n = batch * seq
    fox_heads = fox_forget_bias.shape[-1]
    fox_dim = w_branch_fox.shape[1] // fox_heads
    fox_w = fox_heads * fox_dim
    rank, gla_kw = gla_gate_up.shape[1:]
    gla_dv = gla_head_norm.shape[-1]
    gla_vw = w_branch_gla.shape[1]
    gla_heads = gla_vw // gla_dv
    gla_dk = gla_kw // gla_heads
    assert fox_heads + rank <= LANES

    tn = 512
    assert w_in.shape[-1] == 3 * fox_w + fox_heads + 2 * gla_kw + 2 * gla_vw + rank
    assert (3 * fox_w) % tn == 0 and gla_kw % tn == 0 and gla_vw % tn == 0
    assert fox_heads % 8 == 0 and rank % 8 == 0
    lead = 3 * fox_w // tn
    kt, vt = gla_kw // tn, gla_vw // tn
    proj_outs = (_ProjOut(0, lead, 1.0, BF16),
                 _ProjOut(lead, kt, gla_dk ** -0.5, BF16),
                 _ProjOut(lead + kt, kt, 1.0, F32),
                 _ProjOut(lead + 2 * kt, vt, 1.0, BF16),
                 _ProjOut(lead + 2 * kt + vt, vt, 1.0, F32))
    proj_row = lambda j: pl.multiple_of(j * tn + jnp.where(j < lead, 0, fox_heads), 8)
    fl_rows = (3 * fox_w, fox_heads)
    gd_rows = (w_in.shape[-1] - rank, rank)

    row = lambda v: v.reshape(1, -1).astype(F32)
    h = x.reshape(n, d)
    tm_ffn = _tile(n, 1024)
    tf = _tile(ffn1_w_gate.shape[-1], 512)
    tm_proj = _tile(n, 1024)
    tm_merge = _tile(n, 512)
    tm_ple = _tile(n, 512)
    t_fox = _tile(seq, 512)
    fox_hp = 4
    tb_gla = _tile(seq, 1024)

    ni_ffn, ni_proj, ni_merge = n // tm_ffn, n // tm_proj, n // tm_merge
    nj_ffn = ffn1_w_gate.shape[-1] // tf

    for i in range(depth):
        fbias = jnp.zeros((1, LANES), F32).at[0, :fox_heads].set(fox_forget_bias[i])
        wgu = jnp.zeros((LANES, gla_kw), F32).at[fox_heads:fox_heads + rank].set(gla_gate_up[i]).astype(BF16)

        h, w2g, w2u, w2d = _ffn(
            h, row(ffn1_norm[i]), ffn1_w_gate[i].astype(BF16), ffn1_w_up[i].astype(BF16),
            ffn1_w_down[i].astype(BF16), tm=tm_ffn, tf=tf,
            casts=(_cast_cols(ffn2_w_gate[i], ni_ffn, tf, 0, nj_ffn),
                   _cast_cols(ffn2_w_up[i], ni_ffn, tf, 0, nj_ffn),
                   _cast_rows(ffn2_w_down[i], ni_ffn, tf, 0, nj_ffn)))

        nd = d // tn
        fox_qkv, gq, gk, gv, gr, small, wmg, wbf, wbg, wo = _proj(
            h, row(mix_norm[i]), jnp.swapaxes(w_in, 1, 2), i, tm=tm_proj, tn=tn, outs=proj_outs,
            row_start=proj_row, fl_rows=fl_rows, gd_rows=gd_rows,
            casts=(_cast_cols(w_merge_gate[i], ni_proj, tn, 0, 2 * nd),
                   _cast_cols(w_branch_fox[i], ni_proj, tn, 0, nd),
                   _cast_cols(w_branch_gla[i], ni_proj, tn, nd, nd),
                   _cast_cols(w_out[i], ni_proj, tn, 2 * nd, nd)))

        fcum = _fcum(small, fbias, batch=batch, seq=seq)
        y_fox = _fox(fox_qkv.reshape(batch, seq, -1), fcum,
                     batch=batch, seq=seq, n_heads=fox_heads, head_dim=fox_dim, t=t_fox, hp=fox_hp)

        bsd = lambda a: a.reshape(batch, seq, -1)
        y_gla = _gla(bsd(gq), bsd(gk), bsd(gv), bsd(gr), bsd(small), wgu, row(gla_gate_bias[i]),
                     row(gla_head_norm[i]), batch=batch, seq=seq, n_heads=gla_heads,
                     dk=gla_dk, dv=gla_dv, tb=tb_gla)

        h, wpg, wpe = _merge(
            h, row(mix_norm[i]), y_fox.reshape(n, fox_w), y_gla.reshape(n, gla_vw),
            wmg, row(b_merge_gate[i]), wbf, wbg, wo, tm=tm_merge, tn=tn,
            casts=(_cast_cols(w_ple_gate[i], ni_merge, tn, 0, nd),
                   _cast_cols(w_ple_proj[i], ni_merge, tn, 0, nd)))

        h, = _ffn(h, row(ffn2_norm[i]), w2g, w2u, w2d, tm=tm_ffn, tf=tf)

        h = _ple(h, p[i].reshape(n, -1), row(ple_norm[i]), row(final_norm), wpg, wpe,
                 tm=tm_ple, final=(i == depth - 1))

    return h.reshape(batch, seq, d)
```

```python
import functools
import math
from typing import Callable, NamedTuple

import jax
import jax.numpy as jnp
from jax import lax
from jax.experimental import pallas as pl
from jax.experimental.pallas import tpu as pltpu

EPS = 1e-6
CHUNK = 64
GLA_GATE_TAU = 16.0
LANES = 128
V7X_VMEM_LIMIT_BYTES = 62 * 1024 * 1024
LOG2E = math.log2(math.e)

F32 = jnp.float32
BF16 = jnp.bfloat16


def _rms(x, g):
    return x * lax.rsqrt(jnp.mean(x * x, axis=-1, keepdims=True) + EPS) * g


def _log_sigmoid(z):
    return jnp.minimum(z, 0.0) - jnp.log1p(jnp.exp(-jnp.abs(z)))


def _dot(a, b):
    return lax.dot_general(a, b, (((1,), (0,)), ((), ())), preferred_element_type=F32)


def _params(*sem):
    return pltpu.CompilerParams(dimension_semantics=sem,
                                vmem_limit_bytes=V7X_VMEM_LIMIT_BYTES)


class _Cast(NamedTuple):
    src: jax.Array
    src_spec: pl.BlockSpec
    dst_spec: pl.BlockSpec
    dst_shape: tuple
    live: Callable


def _cast_cols(src, ni, tn, j0, nj):
    r, c = src.shape
    assert r % ni == 0 and (r // ni) % 16 == 0 and c == nj * tn, (src.shape, ni, tn, nj)
    spec = pl.BlockSpec((r // ni, tn), lambda i, j: (i, jnp.clip(j - j0, 0, nj - 1)))
    return _Cast(src, spec, spec, src.shape, lambda i, j: (j >= j0) & (j < j0 + nj))


def _cast_rows(src, ni, tn, j0, nj):
    r, c = src.shape
    assert c % ni == 0 and (c // ni) % LANES == 0 and r == nj * tn, (src.shape, ni, tn, nj)
    spec = pl.BlockSpec((tn, c // ni), lambda i, j: (jnp.clip(j - j0, 0, nj - 1), i))
    return _Cast(src, spec, spec, src.shape, lambda i, j: (j >= j0) & (j < j0 + nj))


def _cast_compact_rows(src, layer, ni, nj, lead_rows, skip, total_rows):
    d = src.shape[2]
    ok = lambda c: (total_rows % (ni * c) == 0 and (total_rows // (ni * c)) % 16 == 0
                    and lead_rows % (total_rows // (ni * c)) == 0)
    nju = max(c for c in range(1, nj + 1) if ok(c))
    rb = total_rows // (ni * nju)
    blk = lambda i, j: i * nju + jnp.minimum(j, nju - 1)
    src_row = lambda i, j: pl.multiple_of(
        blk(i, j) * rb + jnp.where(blk(i, j) * rb < lead_rows, 0, skip), 8)
    return _Cast(src,
                 pl.BlockSpec((None, pl.Element(rb), pl.Element(d)),
                              lambda i, j: (layer, src_row(i, j), 0)),
                 pl.BlockSpec((rb, d), lambda i, j: (blk(i, j), 0)),
                 (total_rows, d),
                 lambda i, j: j < nju)


def _cast_specs(casts):
    shapes = [jax.ShapeDtypeStruct(c.dst_shape, BF16) for c in casts]
    return [c.src_spec for c in casts], [c.dst_spec for c in casts], shapes


def _run_casts(lives, src_refs, dst_refs):
    i, j = pl.program_id(0), pl.program_id(1)
    for live, src, dst in zip(lives, src_refs, dst_refs):
        @pl.when(live(i, j))
        def _():
            dst[...] = src[...].astype(BF16)


def _ffn_kernel(*refs, lives):
    nc = len(lives)
    x_ref, g_ref, wg_ref, wu_ref, wd_ref = refs[:5]
    o_ref, xn_ref = refs[5 + nc], refs[6 + 2 * nc]

    @pl.when(pl.program_id(1) == 0)
    def _():
        x = x_ref[...]
        xn_ref[...] = _rms(x, g_ref[...]).astype(BF16)
        o_ref[...] = x

    xn = xn_ref[...]
    gate = _dot(xn, wg_ref[...])
    up = _dot(xn, wu_ref[...])
    hid = (0.5 * (gate * jax.nn.sigmoid(gate)) * up).astype(BF16)
    o_ref[...] += _dot(hid, wd_ref[...])
    _run_casts(lives, refs[5:5 + nc], refs[6 + nc:6 + 2 * nc])


def _ffn(x, g, wg, wu, wd, *, tm, tf, casts=()):
    n, d = x.shape
    f = wg.shape[1]
    cast_in_specs, cast_out_specs, cast_shapes = _cast_specs(casts)
    return pl.pallas_call(
        functools.partial(_ffn_kernel, lives=tuple(c.live for c in casts)),
        grid=(n // tm, f // tf),
        in_specs=[
            pl.BlockSpec((tm, d), lambda i, j: (i, 0)),
            pl.BlockSpec((1, d), lambda i, j: (0, 0)),
            pl.BlockSpec((d, tf), lambda i, j: (0, j)),
            pl.BlockSpec((d, tf), lambda i, j: (0, j)),
            pl.BlockSpec((tf, d), lambda i, j: (j, 0)),
        ] + cast_in_specs,
        out_specs=[pl.BlockSpec((tm, d), lambda i, j: (i, 0))] + cast_out_specs,
        out_shape=[jax.ShapeDtypeStruct((n, d), F32)] + cast_shapes,
        scratch_shapes=[pltpu.VMEM((tm, d), BF16)],
        compiler_params=_params("parallel", "arbitrary"),
        name="ffn",
    )(x, g, wg, wu, wd, *[c.src for c in casts])


class _ProjOut(NamedTuple):
    first: int
    count: int
    scale: float
    dtype: type


def _dot_nt(a, b):
    return lax.dot_general(a, b, (((1,), (1,)), ((), ())), preferred_element_type=F32)


def _proj_kernel(*refs, lives, outs, n_fl, rank):
    nc = len(lives)
    h_ref, g_ref, w_ref, wfl_ref, wgd_ref = refs[:5]
    out_refs = refs[5 + nc:5 + nc + len(outs)]
    small_ref = refs[5 + nc + len(outs)]
    un_ref = refs[-1]
    j = pl.program_id(1)
    _run_casts(lives, refs[5:5 + nc], refs[6 + nc + len(outs):6 + 2 * nc + len(outs)])

    @pl.when(j == 0)
    def _():
        un = _rms(h_ref[...], g_ref[...]).astype(BF16)
        un_ref[...] = un
        small_ref[...] = jnp.zeros(small_ref.shape, F32)
        small_ref[:, :n_fl] = _dot_nt(un, wfl_ref[...])
        small_ref[:, n_fl:n_fl + rank] = _dot_nt(un, wgd_ref[...])

    for o_ref, o in zip(out_refs, outs):
        @pl.when((j >= o.first) & (j < o.first + o.count))
        def _():
            tile = _dot_nt(un_ref[...], w_ref[...])
            if o.scale != 1.0:
                tile = tile * o.scale
            o_ref[...] = tile.astype(o_ref.dtype)


def _proj(h, g, w_main, w_in_t, layer, *, tm, tn, outs, fl_rows, gd_rows, casts=()):
    n, d = h.shape
    n_steps = sum(o.count for o in outs)
    assert fl_rows[1] + gd_rows[1] <= LANES
    cast_in_specs, cast_out_specs, cast_shapes = _cast_specs(casts)
    kern = functools.partial(_proj_kernel, lives=tuple(c.live for c in casts), outs=outs,
                             n_fl=fl_rows[1], rank=gd_rows[1])
    rows = lambda start, size: pl.BlockSpec(
        (None, pl.Element(size), pl.Element(d)), lambda i, j: (layer, start, 0))

    def out_spec(o):
        return pl.BlockSpec((tm, tn), lambda i, j: (i, jnp.clip(j - o.first, 0, o.count - 1)))

    return pl.pallas_call(
        kern,
        grid=(n // tm, n_steps),
        in_specs=[
            pl.BlockSpec((tm, d), lambda i, j: (i, 0)),
            pl.BlockSpec((1, d), lambda i, j: (0, 0)),
            pl.BlockSpec((tn, d), lambda i, j: (j, 0)),
            rows(*fl_rows),
            rows(*gd_rows),
        ] + cast_in_specs,
        out_specs=[out_spec(o) for o in outs]
        + [pl.BlockSpec((tm, LANES), lambda i, j: (i, 0))] + cast_out_specs,
        out_shape=[jax.ShapeDtypeStruct((n, o.count * tn), o.dtype) for o in outs]
        + [jax.ShapeDtypeStruct((n, LANES), F32)] + cast_shapes,
        scratch_shapes=[pltpu.VMEM((tm, d), BF16)],
        compiler_params=_params("parallel", "arbitrary"),
        name="proj",
    )(h, g, w_main, w_in_t, w_in_t, *[c.src for c in casts])


def _fcum_kernel(small_ref, bias_ref, f_ref):
    s = small_ref.shape[0]
    f_ref[0] = _log_sigmoid(small_ref[...] + bias_ref[...])
    row = lax.broadcasted_iota(jnp.int32, (LANES, LANES), 0)
    col = lax.broadcasted_iota(jnp.int32, (LANES, LANES), 1)
    tril = (row >= col).astype(F32)

    def body(r, carry):
        rows = pl.ds(pl.multiple_of(r * LANES, LANES), LANES)
        c = jnp.dot(tril, f_ref[0, rows, :], precision=lax.Precision.HIGHEST,
                    preferred_element_type=F32) + carry
        f_ref[0, rows, :] = c
        return c[LANES - 1:LANES, :]

    lax.fori_loop(0, s // LANES, body, jnp.zeros((1, LANES), F32))


def _fcum(small, bias_row, *, batch, seq):
    return pl.pallas_call(
        _fcum_kernel,
        grid=(batch,),
        in_specs=[
            pl.BlockSpec((seq, LANES), lambda b: (b, 0)),
            pl.BlockSpec((1, LANES), lambda b: (0, 0)),
        ],
        out_specs=pl.BlockSpec((1, seq, LANES), lambda b: (b, 0, 0)),
        out_shape=jax.ShapeDtypeStruct((batch, seq, LANES), F32),
        compiler_params=_params("parallel"),
        name="fcum",
    )(small, bias_row)


def _fox_kernel(q_ref, k_ref, v_ref, f_ref, o_ref, frep_ref, vt_ref, m_ref, l_ref, acc_ref,
                s0_ref, s1_ref, mc0_ref, mc1_ref, *, t, c2, hp, dh):
    g = pl.program_id(1)
    i = pl.program_id(2)
    s_refs = (s0_ref, s1_ref)
    mc_refs = (mc0_ref, mc1_ref)

    @pl.when(i == 0)
    def _():
        row = lax.broadcasted_iota(jnp.int32, (LANES, LANES), 0)
        for h in range(hp):
            onehot = (row == g * hp + h).astype(F32)
            frep_ref[h] = jnp.dot(f_ref[0], onehot, precision=lax.Precision.HIGHEST,
                                  preferred_element_type=F32)
            vt_ref[h] = v_ref[0, :, h * dh:(h + 1) * dh].astype(F32).T.astype(BF16)

    q0 = pl.multiple_of(i * t, t)
    m_ref[...] = jnp.full(m_ref.shape, -jnp.inf, F32)
    l_ref[...] = jnp.zeros(l_ref.shape, F32)
    acc_ref[...] = jnp.zeros(acc_ref.shape, F32)
    reps = t // LANES

    def scores(kk, slot, masked):
        k0 = pl.multiple_of(kk * t, t)
        for h in range(hp):
            hs = slice(h * dh, (h + 1) * dh)
            f_base = frep_ref[h, pl.ds(q0, 8), :][0:1, :]
            bias = (f_base - frep_ref[h, pl.ds(k0, t), :]) * LOG2E
            s = lax.dot_general(k_ref[0, pl.ds(k0, t), hs], q_ref[0, :, hs],
                                (((1,), (1,)), ((), ())), preferred_element_type=F32)
            s = s * c2 + jnp.tile(bias, (1, reps))
            if masked:
                key = lax.broadcasted_iota(jnp.int32, (t, t), 0)
                qry = lax.broadcasted_iota(jnp.int32, (t, t), 1)
                s = jnp.where(key <= qry, s, -jnp.inf)
            s_refs[slot][h] = s
            mc_refs[slot][h] = jnp.max(s, axis=0, keepdims=True)

    def absorb(kk, slot):
        k0 = pl.multiple_of(kk * t, t)
        for h in range(hp):
            m_old = m_ref[h]
            m_new = jnp.maximum(m_old, mc_refs[slot][h])
            alpha = jnp.exp2(m_old - m_new)
            p = jnp.exp2(s_refs[slot][h] - m_new)
            l_ref[h] = alpha * l_ref[h] + jnp.sum(p, axis=0, keepdims=True)
            acc_ref[h] = alpha * acc_ref[h] + _dot(vt_ref[h, :, pl.ds(k0, t)], p.astype(BF16))
            m_ref[h] = m_new

    scores(i, 0, True)

    def body(kk, carry):
        prev = jnp.where(kk == 0, i, kk - 1)
        for par in range(2):
            @pl.when(kk % 2 == par)
            def _():
                absorb(prev, par)
                scores(kk, 1 - par, False)
        return carry

    lax.fori_loop(0, i, body, 0)
    last = jnp.where(i == 0, i, i - 1)
    for par in range(2):
        @pl.when(i % 2 == par)
        def _():
            absorb(last, par)
    for h in range(hp):
        o_ref[0, :, h * dh:(h + 1) * dh] = (acc_ref[h] / l_ref[h]).T.astype(o_ref.dtype)


def _fox(qkv, fcum, *, batch, seq, n_heads, head_dim, t, hp):
    assert head_dim == LANES and n_heads % hp == 0
    c2 = (head_dim ** -0.5) * LOG2E
    ng = n_heads // hp
    w = hp * head_dim
    return pl.pallas_call(
        functools.partial(_fox_kernel, t=t, c2=c2, hp=hp, dh=head_dim),
        grid=(batch, ng, seq // t),
        in_specs=[
            pl.BlockSpec((1, t, w), lambda b, g, i: (b, i, g)),
            pl.BlockSpec((1, seq, w), lambda b, g, i: (b, 0, ng + g)),
            pl.BlockSpec((1, seq, w), lambda b, g, i: (b, 0, 2 * ng + g)),
            pl.BlockSpec((1, seq, LANES), lambda b, g, i: (b, 0, 0)),
        ],
        out_specs=pl.BlockSpec((1, t, w), lambda b, g, i: (b, i, g)),
        out_shape=jax.ShapeDtypeStruct((batch, seq, n_heads * head_dim), BF16),
        scratch_shapes=[pltpu.VMEM((hp, seq, LANES), F32), pltpu.VMEM((hp, head_dim, seq), BF16),
                        pltpu.VMEM((hp, 1, t), F32), pltpu.VMEM((hp, 1, t), F32),
                        pltpu.VMEM((hp, head_dim, t), F32),
                        pltpu.VMEM((hp, t, t), F32), pltpu.VMEM((hp, t, t), F32),
                        pltpu.VMEM((hp, 1, t), F32), pltpu.VMEM((hp, 1, t), F32)],
        compiler_params=_params("parallel", "arbitrary", "arbitrary"),
        name="fox",
    )(qkv, qkv, qkv, fcum)


def _gla_kernel(q_ref, k_ref, v_ref, r_ref, small_ref, wgu_ref, gb_ref, gain_ref, o_ref,
                st_ref, la_ref, kd_ref, ea_ref, *, n_heads, dk, dv):
    tb = q_ref.shape[1]
    n_chunks = tb // CHUNK

    @pl.when(pl.program_id(1) == 0)
    def _():
        st_ref[...] = jnp.zeros(st_ref.shape, F32)

    pre = _dot(small_ref[0].astype(BF16), wgu_ref[...]) + gb_ref[...]
    la_ref[...] = _log_sigmoid(pre) * (1.0 / GLA_GATE_TAU)

    row = lax.broadcasted_iota(jnp.int32, (CHUNK, CHUNK), 0)
    col = lax.broadcasted_iota(jnp.int32, (CHUNK, CHUNK), 1)
    after = (col > row).astype(F32)

    def prep(c, carry):
        rows = pl.ds(pl.multiple_of(c * CHUNK, CHUNK), CHUNK)
        la = la_ref[rows, :]
        rev = jnp.dot(after, la, precision=lax.Precision.HIGHEST, preferred_element_type=F32)
        kd_ref[rows, :] = (k_ref[0, rows, :] * jnp.exp(rev)).astype(BF16)
        ea_ref[c] = jnp.exp(jnp.sum(la, axis=0, keepdims=True))
        return carry

    lax.fori_loop(0, n_chunks, prep, 0)

    def step(c, carry):
        rows = pl.ds(pl.multiple_of(c * CHUNK, CHUNK), CHUNK)
        ea = ea_ref[c]
        for h in range(n_heads):
            ks = slice(h * dk, (h + 1) * dk)
            vs = slice(h * dv, (h + 1) * dv)
            kv_t = lax.dot_general(v_ref[0, rows, vs], kd_ref[rows, ks],
                                   (((0,), (0,)), ((), ())), preferred_element_type=F32)
            st = st_ref[h] * ea[:, ks] + kv_t
            st_ref[h] = st
            o = lax.dot_general(q_ref[0, rows, ks], st.astype(BF16),
                                (((1,), (1,)), ((), ())), preferred_element_type=F32)
            on = _rms(o, gain_ref[...])
            r = r_ref[0, rows, vs]
            o_ref[0, rows, vs] = (on * (r * jax.nn.sigmoid(r))).astype(o_ref.dtype)
        return carry

    lax.fori_loop(0, n_chunks, step, 0)


def _gla(q, k, v, r, small, wgu, gate_bias, gain, *, batch, seq, n_heads, dk, dv, tb):
    kw, vw = n_heads * dk, n_heads * dv
    return pl.pallas_call(
        functools.partial(_gla_kernel, n_heads=n_heads, dk=dk, dv=dv),
        grid=(batch, seq // tb),
        in_specs=[
            pl.BlockSpec((1, tb, kw), lambda b, t: (b, t, 0)),
            pl.BlockSpec((1, tb, kw), lambda b, t: (b, t, 0)),
            pl.BlockSpec((1, tb, vw), lambda b, t: (b, t, 0)),
            pl.BlockSpec((1, tb, vw), lambda b, t: (b, t, 0)),
            pl.BlockSpec((1, tb, LANES), lambda b, t: (b, t, 0)),
            pl.BlockSpec((LANES, kw), lambda b, t: (0, 0)),
            pl.BlockSpec((1, kw), lambda b, t: (0, 0)),
            pl.BlockSpec((1, dv), lambda b, t: (0, 0)),
        ],
        out_specs=pl.BlockSpec((1, tb, vw), lambda b, t: (b, t, 0)),
        out_shape=jax.ShapeDtypeStruct((batch, seq, vw), BF16),
        scratch_shapes=[
            pltpu.VMEM((n_heads, dv, dk), F32),
            pltpu.VMEM((tb, kw), F32),
            pltpu.VMEM((tb, kw), BF16),
            pltpu.VMEM((tb // CHUNK, 1, kw), F32),
        ],
        compiler_params=_params("parallel", "arbitrary"),
        name="gla",
    )(q, k, v, r, small, wgu, gate_bias, gain)


def _merge_kernel(*refs, lives):
    nc = len(lives)
    (h_ref, g_ref, yf_ref, yg_ref, wgf_ref, wgg_ref, bgf_ref, bgg_ref,
     wa_ref, wb_ref, wo_ref) = refs[:11]
    o_ref, un_ref = refs[11 + nc], refs[12 + 2 * nc]
    _run_casts(lives, refs[11:11 + nc], refs[12 + nc:12 + 2 * nc])

    @pl.when(pl.program_id(1) == 0)
    def _():
        h = h_ref[...]
        un_ref[...] = _rms(h, g_ref[...]).astype(BF16)
        o_ref[...] = h

    un = un_ref[...]
    g_fox = jax.nn.sigmoid(_dot(un, wgf_ref[...]) + bgf_ref[...])
    g_gla = jax.nn.sigmoid(_dot(un, wgg_ref[...]) + bgg_ref[...])
    merged = g_fox * _dot(yf_ref[...], wa_ref[...]) + g_gla * _dot(yg_ref[...], wb_ref[...])
    o_ref[...] += _dot(merged.astype(BF16), wo_ref[...])


def _merge(h, g, y_fox, y_gla, w_gate, b_gate, w_a, w_b, w_o, *, tm, tn, casts=()):
    n, d = h.shape
    nj = d // tn
    cast_in_specs, cast_out_specs, cast_shapes = _cast_specs(casts)
    return pl.pallas_call(
        functools.partial(_merge_kernel, lives=tuple(c.live for c in casts)),
        grid=(n // tm, nj),
        in_specs=[
            pl.BlockSpec((tm, d), lambda i, j: (i, 0)),
            pl.BlockSpec((1, d), lambda i, j: (0, 0)),
            pl.BlockSpec((tm, y_fox.shape[1]), lambda i, j: (i, 0)),
            pl.BlockSpec((tm, y_gla.shape[1]), lambda i, j: (i, 0)),
            pl.BlockSpec((d, tn), lambda i, j: (0, j)),
            pl.BlockSpec((d, tn), lambda i, j: (0, nj + j)),
            pl.BlockSpec((1, tn), lambda i, j: (0, j)),
            pl.BlockSpec((1, tn), lambda i, j: (0, nj + j)),
            pl.BlockSpec((w_a.shape[0], tn), lambda i, j: (0, j)),
            pl.BlockSpec((w_b.shape[0], tn), lambda i, j: (0, j)),
            pl.BlockSpec((tn, d), lambda i, j: (j, 0)),
        ] + cast_in_specs,
        out_specs=[pl.BlockSpec((tm, d), lambda i, j: (i, 0))] + cast_out_specs,
        out_shape=[jax.ShapeDtypeStruct((n, d), F32)] + cast_shapes,
        scratch_shapes=[pltpu.VMEM((tm, d), BF16)],
        compiler_params=_params("parallel", "arbitrary"),
        name="merge",
    )(h, g, y_fox, y_gla, w_gate, w_gate, b_gate, b_gate, w_a, w_b, w_o, *[c.src for c in casts])


def _ple_kernel(h_ref, p_ref, gp_ref, gf_ref, wpg_ref, wpe_ref, o_ref, *, final):
    h = h_ref[...]
    hn = _rms(h, gp_ref[...]).astype(BF16)
    gate = jax.nn.sigmoid(_dot(hn, wpg_ref[...]))
    out = h + gate * _dot(p_ref[...].astype(BF16), wpe_ref[...])
    if final:
        out = _rms(out, gf_ref[...])
    o_ref[...] = out


def _ple(h, p, g_ple, g_final, w_gate, w_proj, *, tm, final):
    n, d = h.shape
    dp = p.shape[1]
    return pl.pallas_call(
        functools.partial(_ple_kernel, final=final),
        grid=(n // tm,),
        in_specs=[
            pl.BlockSpec((tm, d), lambda i: (i, 0)),
            pl.BlockSpec((tm, dp), lambda i: (i, 0)),
            pl.BlockSpec((1, d), lambda i: (0, 0)),
            pl.BlockSpec((1, d), lambda i: (0, 0)),
            pl.BlockSpec((d, d), lambda i: (0, 0)),
            pl.BlockSpec((dp, d), lambda i: (0, 0)),
        ],
        out_specs=pl.BlockSpec((tm, d), lambda i: (i, 0)),
        out_shape=jax.ShapeDtypeStruct((n, d), F32),
        compiler_params=_params("parallel"),
        name="ple",
    )(h, p, g_ple, g_final, w_gate, w_proj)


def _tile(n, want):
    t = min(n, want)
    assert n % t == 0, (n, want)
    return t


def kernel(x, p, ffn1_norm, ffn1_w_gate, ffn1_w_up, ffn1_w_down, mix_norm, w_in, fox_forget_bias, gla_gate_up, gla_gate_bias, gla_head_norm, w_branch_fox, w_branch_gla, w_merge_gate, b_merge_gate, w_out, ffn2_norm, ffn2_w_gate, ffn2_w_up, ffn2_w_down, ple_norm, w_ple_proj, w_ple_gate, final_norm):
    batch, seq, d = x.shape
    depth = p.shape[0]
    n = batch * seq
    fox_heads = fox_forget_bias.shape[-1]
    fox_dim = w_branch_fox.shape[1] // fox_heads
    fox_w = fox_heads * fox_dim
    rank, gla_kw = gla_gate_up.shape[1:]
    gla_dv = gla_head_norm.shape[-1]
    gla_vw = w_branch_gla.shape[1]
    gla_heads = gla_vw // gla_dv
    gla_dk = gla_kw // gla_heads
    assert fox_heads + rank <= LANES

    tn = 512
    assert w_in.shape[-1] == 3 * fox_w + fox_heads + 2 * gla_kw + 2 * gla_vw + rank
    assert (3 * fox_w) % tn == 0 and gla_kw % tn == 0 and gla_vw % tn == 0
    assert fox_heads % 8 == 0 and rank % 8 == 0
    lead = 3 * fox_w // tn
    kt, vt = gla_kw // tn, gla_vw // tn
    proj_outs = (_ProjOut(0, lead, 1.0, BF16),
                 _ProjOut(lead, kt, gla_dk ** -0.5, BF16),
                 _ProjOut(lead + kt, kt, 1.0, F32),
                 _ProjOut(lead + 2 * kt, vt, 1.0, BF16),
                 _ProjOut(lead + 2 * kt + vt, vt, 1.0, F32))
    proj_rows = (lead + 2 * kt + 2 * vt) * tn
    fl_rows = (3 * fox_w, fox_heads)
    gd_rows = (w_in.shape[-1] - rank, rank)

    row = lambda v: v.reshape(1, -1).astype(F32)
    h = x.reshape(n, d)
    tm_ffn = _tile(n, 1024)
    tf = _tile(ffn1_w_gate.shape[-1], 512)
    tm_proj = _tile(n, 1024)
    tm_merge = _tile(n, 512)
    tm_ple = _tile(n, 512)
    t_fox = _tile(seq, 512)
    fox_hp = 4
    tb_gla = _tile(seq, 1024)

    ni_ffn, ni_proj, ni_merge = n // tm_ffn, n // tm_proj, n // tm_merge
    nj_ffn = ffn1_w_gate.shape[-1] // tf

    for i in range(depth):
        fbias = jnp.zeros((1, LANES), F32).at[0, :fox_heads].set(fox_forget_bias[i])
        wgu = jnp.zeros((LANES, gla_kw), F32).at[fox_heads:fox_heads + rank].set(gla_gate_up[i]).astype(BF16)

        w_in_t = jnp.swapaxes(w_in, 1, 2)

        h, w2g, w2u, w2d, w_proj = _ffn(
            h, row(ffn1_norm[i]), ffn1_w_gate[i].astype(BF16), ffn1_w_up[i].astype(BF16),
            ffn1_w_down[i].astype(BF16), tm=tm_ffn, tf=tf,
            casts=(_cast_cols(ffn2_w_gate[i], ni_ffn, tf, 0, nj_ffn),
                   _cast_cols(ffn2_w_up[i], ni_ffn, tf, 0, nj_ffn),
                   _cast_rows(ffn2_w_down[i], ni_ffn, tf, 0, nj_ffn),
                   _cast_compact_rows(w_in_t, i, ni_ffn, nj_ffn, 3 * fox_w, fox_heads, proj_rows)))

        nd = d // tn
        fox_qkv, gq, gk, gv, gr, small, wmg, wbf, wbg, wo = _proj(
            h, row(mix_norm[i]), w_proj, w_in_t, i, tm=tm_proj, tn=tn, outs=proj_outs,
            fl_rows=fl_rows, gd_rows=gd_rows,
            casts=(_cast_cols(w_merge_gate[i], ni_proj, tn, 0, 2 * nd),
                   _cast_cols(w_branch_fox[i], ni_proj, tn, 0, nd),
                   _cast_cols(w_branch_gla[i], ni_proj, tn, nd, nd),
                   _cast_cols(w_out[i], ni_proj, tn, 2 * nd, nd)))

        fcum = _fcum(small, fbias, batch=batch, seq=seq)
        y_fox = _fox(fox_qkv.reshape(batch, seq, -1), fcum,
                     batch=batch, seq=seq, n_heads=fox_heads, head_dim=fox_dim, t=t_fox, hp=fox_hp)

        bsd = lambda a: a.reshape(batch, seq, -1)
        y_gla = _gla(bsd(gq), bsd(gk), bsd(gv), bsd(gr), bsd(small), wgu, row(gla_gate_bias[i]),
                     row(gla_head_norm[i]), batch=batch, seq=seq, n_heads=gla_heads,
                     dk=gla_dk, dv=gla_dv, tb=tb_gla)

        h, wpg, wpe = _merge(
            h, row(mix_norm[i]), y_fox.reshape(n, fox_w), y_gla.reshape(n, gla_vw),
            wmg, row(b_merge_gate[i]), wbf, wbg, wo, tm=tm_merge, tn=tn,
            casts=(_cast_cols(w_ple_gate[i], ni_merge, tn, 0, nd),
                   _cast_cols(w_ple_proj[i], ni_merge, tn, 0, nd)))

        h, = _ffn(h, row(ffn2_norm[i]), w2g, w2u, w2d, tm=tm_ffn, tf=tf)

        h = _ple(h, p[i].reshape(n, -1), row(ple_norm[i]), row(final_norm), wpg, wpe,
                 tm=tm_ple, final=(i == depth - 1))

    return h.reshape(batch, seq, d)
```

```python
import functools
import math
from typing import Callable, NamedTuple

import jax
import jax.numpy as jnp
from jax import lax
from jax.experimental import pallas as pl
from jax.experimental.pallas import tpu as pltpu

EPS = 1e-6
CHUNK = 64
GLA_GATE_TAU = 16.0
LANES = 128
V7X_VMEM_LIMIT_BYTES = 62 * 1024 * 1024
LOG2E = math.log2(math.e)

F32 = jnp.float32
BF16 = jnp.bfloat16


def _rms(x, g):
    return x * lax.rsqrt(jnp.mean(x * x, axis=-1, keepdims=True) + EPS) * g


def _log_sigmoid(z):
    return jnp.minimum(z, 0.0) - jnp.log1p(jnp.exp(-jnp.abs(z)))


def _dot(a, b):
    return lax.dot_general(a, b, (((1,), (0,)), ((), ())), preferred_element_type=F32)


def _params(*sem):
    return pltpu.CompilerParams(dimension_semantics=sem,
                                vmem_limit_bytes=V7X_VMEM_LIMIT_BYTES)


class _Cast(NamedTuple):
    src: jax.Array
    src_spec: pl.BlockSpec
    dst_spec: pl.BlockSpec
    dst_shape: tuple
    live: Callable


def _flat_step(grid):
    def step(*ids):
        s = ids[0]
        for size, idx in zip(grid[1:], ids[1:]):
            s = s * size + idx
        return s
    return step


def _row_block_count(rows, n_steps, extra=lambda rb: True):
    ok = lambda k: rows % k == 0 and (rows // k) % 16 == 0 and extra(rows // k)
    return max(k for k in range(1, n_steps + 1) if ok(k))


def _cast_row_blocks(src, grid):
    r, c = src.shape
    step = _flat_step(grid)
    nb = _row_block_count(r, math.prod(grid))
    spec = pl.BlockSpec((r // nb, c), lambda *ids: (jnp.minimum(step(*ids), nb - 1), 0))
    return _Cast(src, spec, spec, src.shape, lambda *ids: step(*ids) < nb)


def _cast_compact_rows(src, layer, grid, lead_rows, skip, total_rows):
    d = src.shape[2]
    step = _flat_step(grid)
    nb = _row_block_count(total_rows, math.prod(grid), lambda rb: lead_rows % rb == 0)
    rb = total_rows // nb
    blk = lambda *ids: jnp.minimum(step(*ids), nb - 1)
    src_row = lambda *ids: pl.multiple_of(
        blk(*ids) * rb + jnp.where(blk(*ids) * rb < lead_rows, 0, skip), 8)
    return _Cast(src,
                 pl.BlockSpec((None, pl.Element(rb), pl.Element(d)),
                              lambda *ids: (layer, src_row(*ids), 0)),
                 pl.BlockSpec((rb, d), lambda *ids: (blk(*ids), 0)),
                 (total_rows, d),
                 lambda *ids: step(*ids) < nb)


def _cast_specs(casts):
    shapes = [jax.ShapeDtypeStruct(c.dst_shape, BF16) for c in casts]
    return [c.src_spec for c in casts], [c.dst_spec for c in casts], shapes


def _run_casts(lives, src_refs, dst_refs, n_axes):
    ids = [pl.program_id(a) for a in range(n_axes)]
    for live, src, dst in zip(lives, src_refs, dst_refs):
        @pl.when(live(*ids))
        def _():
            dst[...] = src[...].astype(BF16)


def _ffn_step(x_ref, g_ref, o_ref, xn_ref, wg, wu, wd):
    @pl.when(pl.program_id(1) == 0)
    def _():
        x = x_ref[...]
        xn_ref[...] = _rms(x, g_ref[...]).astype(BF16)
        o_ref[...] = x

    xn = xn_ref[...]
    gate = _dot(xn, wg)
    up = _dot(xn, wu)
    hid = (0.5 * (gate * jax.nn.sigmoid(gate)) * up).astype(BF16)
    o_ref[...] += _dot(hid, wd)


def _ffn_kernel(*refs, lives, n_extra):
    nc = len(lives)
    x_ref, g_ref, wg_ref, wu_ref, wd_ref = refs[:5]
    o_ref, xn_ref = refs[5 + nc + n_extra], refs[-1]
    _ffn_step(x_ref, g_ref, o_ref, xn_ref, wg_ref[...], wu_ref[...], wd_ref[...])
    _run_casts(lives, refs[5:5 + nc], refs[6 + nc + n_extra:6 + 2 * nc + n_extra], 2)


def _ffn(x, g, wg, wu, wd, *, tm, tf, casts=(), first_tile=0, into=None):
    n, d = x.shape
    f = wg.shape[1]
    cast_in_specs, cast_out_specs, cast_shapes = _cast_specs(casts)
    extra = [] if into is None else [into]
    tok = lambda i, j: (i + first_tile, 0)
    return pl.pallas_call(
        functools.partial(_ffn_kernel, lives=tuple(c.live for c in casts), n_extra=len(extra)),
        grid=(n // tm - first_tile, f // tf),
        in_specs=[
            pl.BlockSpec((tm, d), tok),
            pl.BlockSpec((1, d), lambda i, j: (0, 0)),
            pl.BlockSpec((d, tf), lambda i, j: (0, j)),
            pl.BlockSpec((d, tf), lambda i, j: (0, j)),
            pl.BlockSpec((tf, d), lambda i, j: (j, 0)),
        ] + cast_in_specs + [pl.BlockSpec(memory_space=pl.ANY)] * len(extra),
        out_specs=[pl.BlockSpec((tm, d), tok)] + cast_out_specs,
        out_shape=[jax.ShapeDtypeStruct((n, d), F32)] + cast_shapes,
        input_output_aliases={5 + len(casts): 0} if extra else {},
        scratch_shapes=[pltpu.VMEM((tm, d), BF16)],
        compiler_params=_params("parallel", "arbitrary"),
        name="ffn",
    )(x, g, wg, wu, wd, *[c.src for c in casts], *extra)


def _ffn_first_kernel(x_ref, g_ref, wg_ref, wu_ref, wd_ref, o_ref, wgb_ref, wub_ref, wdb_ref,
                      xn_ref):
    wg, wu, wd = (w[...].astype(BF16) for w in (wg_ref, wu_ref, wd_ref))
    wgb_ref[...], wub_ref[...], wdb_ref[...] = wg, wu, wd
    _ffn_step(x_ref, g_ref, o_ref, xn_ref, wg, wu, wd)


def _ffn_first(x, g, wg, wu, wd, *, tm, tf):
    n, d = x.shape
    f = wg.shape[1]
    w_specs = [pl.BlockSpec((d, tf), lambda i, j: (0, j)), pl.BlockSpec((d, tf), lambda i, j: (0, j)),
               pl.BlockSpec((tf, d), lambda i, j: (j, 0))]
    return pl.pallas_call(
        _ffn_first_kernel,
        grid=(1, f // tf),
        in_specs=[pl.BlockSpec((tm, d), lambda i, j: (0, 0)),
                  pl.BlockSpec((1, d), lambda i, j: (0, 0))] + w_specs,
        out_specs=[pl.BlockSpec((tm, d), lambda i, j: (0, 0))] + w_specs,
        out_shape=[jax.ShapeDtypeStruct((n, d), F32)]
        + [jax.ShapeDtypeStruct(w.shape, BF16) for w in (wg, wu, wd)],
        scratch_shapes=[pltpu.VMEM((tm, d), BF16)],
        compiler_params=_params("arbitrary", "arbitrary"),
        name="ffn_first",
    )(x, g, wg, wu, wd)


class _ProjOut(NamedTuple):
    first: int
    count: int
    scale: float
    dtype: type


def _dot_nt(a, b):
    return lax.dot_general(a, b, (((1,), (1,)), ((), ())), preferred_element_type=F32)


def _proj_kernel(*refs, lives, outs, n_fl, rank):
    nc = len(lives)
    h_ref, g_ref, w_ref, wfl_ref, wgd_ref = refs[:5]
    out_refs = refs[5 + nc:5 + nc + len(outs)]
    small_ref = refs[5 + nc + len(outs)]
    un_ref = refs[-1]
    j = pl.program_id(1)
    _run_casts(lives, refs[5:5 + nc], refs[6 + nc + len(outs):6 + 2 * nc + len(outs)], 2)

    @pl.when(j == 0)
    def _():
        un = _rms(h_ref[...], g_ref[...]).astype(BF16)
        un_ref[...] = un
        small_ref[...] = jnp.zeros(small_ref.shape, F32)
        small_ref[:, :n_fl] = _dot_nt(un, wfl_ref[...])
        small_ref[:, n_fl:n_fl + rank] = _dot_nt(un, wgd_ref[...])

    for o_ref, o in zip(out_refs, outs):
        @pl.when((j >= o.first) & (j < o.first + o.count))
        def _():
            tile = _dot_nt(un_ref[...], w_ref[...])
            if o.scale != 1.0:
                tile = tile * o.scale
            o_ref[...] = tile.astype(o_ref.dtype)


def _proj(h, g, w_main, w_in_t, layer, *, tm, tn, outs, fl_rows, gd_rows, casts=()):
    n, d = h.shape
    n_steps = sum(o.count for o in outs)
    assert fl_rows[1] + gd_rows[1] <= LANES
    cast_in_specs, cast_out_specs, cast_shapes = _cast_specs(casts)
    kern = functools.partial(_proj_kernel, lives=tuple(c.live for c in casts), outs=outs,
                             n_fl=fl_rows[1], rank=gd_rows[1])
    rows = lambda start, size: pl.BlockSpec(
        (None, pl.Element(size), pl.Element(d)), lambda i, j: (layer, start, 0))

    def out_spec(o):
        return pl.BlockSpec((tm, tn), lambda i, j: (i, jnp.clip(j - o.first, 0, o.count - 1)))

    return pl.pallas_call(
        kern,
        grid=(n // tm, n_steps),
        in_specs=[
            pl.BlockSpec((tm, d), lambda i, j: (i, 0)),
            pl.BlockSpec((1, d), lambda i, j: (0, 0)),
            pl.BlockSpec((tn, d), lambda i, j: (j, 0)),
            rows(*fl_rows),
            rows(*gd_rows),
        ] + cast_in_specs,
        out_specs=[out_spec(o) for o in outs]
        + [pl.BlockSpec((tm, LANES), lambda i, j: (i, 0))] + cast_out_specs,
        out_shape=[jax.ShapeDtypeStruct((n, o.count * tn), o.dtype) for o in outs]
        + [jax.ShapeDtypeStruct((n, LANES), F32)] + cast_shapes,
        scratch_shapes=[pltpu.VMEM((tm, d), BF16)],
        compiler_params=_params("parallel", "arbitrary"),
        name="proj",
    )(h, g, w_main, w_in_t, w_in_t, *[c.src for c in casts])


def _fcum_kernel(small_ref, bias_ref, f_ref):
    s = small_ref.shape[0]
    f_ref[0] = _log_sigmoid(small_ref[...] + bias_ref[...])
    row = lax.broadcasted_iota(jnp.int32, (LANES, LANES), 0)
    col = lax.broadcasted_iota(jnp.int32, (LANES, LANES), 1)
    tril = (row >= col).astype(F32)

    def body(r, carry):
        rows = pl.ds(pl.multiple_of(r * LANES, LANES), LANES)
        c = jnp.dot(tril, f_ref[0, rows, :], precision=lax.Precision.HIGHEST,
                    preferred_element_type=F32) + carry
        f_ref[0, rows, :] = c
        return c[LANES - 1:LANES, :]

    lax.fori_loop(0, s // LANES, body, jnp.zeros((1, LANES), F32))


def _fcum(small, bias_row, *, batch, seq):
    return pl.pallas_call(
        _fcum_kernel,
        grid=(batch,),
        in_specs=[
            pl.BlockSpec((seq, LANES), lambda b: (b, 0)),
            pl.BlockSpec((1, LANES), lambda b: (0, 0)),
        ],
        out_specs=pl.BlockSpec((1, seq, LANES), lambda b: (b, 0, 0)),
        out_shape=jax.ShapeDtypeStruct((batch, seq, LANES), F32),
        compiler_params=_params("parallel"),
        name="fcum",
    )(small, bias_row)


def _fox_kernel(*refs, lives, t, c2, hp, dh):
    nc = len(lives)
    q_ref, k_ref, v_ref, f_ref = refs[:4]
    o_ref = refs[4 + nc]
    (frep_ref, vt_ref, m_ref, l_ref, acc_ref, s0_ref, s1_ref, mc0_ref, mc1_ref) = refs[5 + 2 * nc:]
    _run_casts(lives, refs[4:4 + nc], refs[5 + nc:5 + 2 * nc], 3)
    g = pl.program_id(1)
    i = pl.program_id(2)
    s_refs = (s0_ref, s1_ref)
    mc_refs = (mc0_ref, mc1_ref)

    @pl.when(i == 0)
    def _():
        row = lax.broadcasted_iota(jnp.int32, (LANES, LANES), 0)
        for h in range(hp):
            onehot = (row == g * hp + h).astype(F32)
            frep_ref[h] = jnp.dot(f_ref[0], onehot, precision=lax.Precision.HIGHEST,
                                  preferred_element_type=F32)
            vt_ref[h] = v_ref[0, :, h * dh:(h + 1) * dh].astype(F32).T.astype(BF16)

    q0 = pl.multiple_of(i * t, t)
    m_ref[...] = jnp.full(m_ref.shape, -jnp.inf, F32)
    l_ref[...] = jnp.zeros(l_ref.shape, F32)
    acc_ref[...] = jnp.zeros(acc_ref.shape, F32)
    reps = t // LANES

    def scores(kk, slot, masked):
        k0 = pl.multiple_of(kk * t, t)
        for h in range(hp):
            hs = slice(h * dh, (h + 1) * dh)
            f_base = frep_ref[h, pl.ds(q0, 8), :][0:1, :]
            bias = (f_base - frep_ref[h, pl.ds(k0, t), :]) * LOG2E
            s = lax.dot_general(k_ref[0, pl.ds(k0, t), hs], q_ref[0, :, hs],
                                (((1,), (1,)), ((), ())), preferred_element_type=F32)
            s = s * c2 + jnp.tile(bias, (1, reps))
            if masked:
                key = lax.broadcasted_iota(jnp.int32, (t, t), 0)
                qry = lax.broadcasted_iota(jnp.int32, (t, t), 1)
                s = jnp.where(key <= qry, s, -jnp.inf)
            s_refs[slot][h] = s
            mc_refs[slot][h] = jnp.max(s, axis=0, keepdims=True)

    def absorb(kk, slot):
        k0 = pl.multiple_of(kk * t, t)
        for h in range(hp):
            m_old = m_ref[h]
            m_new = jnp.maximum(m_old, mc_refs[slot][h])
            alpha = jnp.exp2(m_old - m_new)
            p = jnp.exp2(s_refs[slot][h] - m_new)
            l_ref[h] = alpha * l_ref[h] + jnp.sum(p, axis=0, keepdims=True)
            acc_ref[h] = alpha * acc_ref[h] + _dot(vt_ref[h, :, pl.ds(k0, t)], p.astype(BF16))
            m_ref[h] = m_new

    scores(i, 0, True)

    def body(kk, carry):
        prev = jnp.where(kk == 0, i, kk - 1)
        for par in range(2):
            @pl.when(kk % 2 == par)
            def _():
                absorb(prev, par)
                scores(kk, 1 - par, False)
        return carry

    lax.fori_loop(0, i, body, 0)
    last = jnp.where(i == 0, i, i - 1)
    for par in range(2):
        @pl.when(i % 2 == par)
        def _():
            absorb(last, par)
    for h in range(hp):
        o_ref[0, :, h * dh:(h + 1) * dh] = (acc_ref[h] / l_ref[h]).T.astype(o_ref.dtype)


def _fox_grid(batch, seq, n_heads, t, hp):
    return (batch, n_heads // hp, seq // t)


def _fox(qkv, fcum, *, batch, seq, n_heads, head_dim, t, hp, casts=()):
    assert head_dim == LANES and n_heads % hp == 0
    c2 = (head_dim ** -0.5) * LOG2E
    ng = n_heads // hp
    w = hp * head_dim
    cast_in_specs, cast_out_specs, cast_shapes = _cast_specs(casts)
    return pl.pallas_call(
        functools.partial(_fox_kernel, lives=tuple(c.live for c in casts), t=t, c2=c2, hp=hp,
                          dh=head_dim),
        grid=_fox_grid(batch, seq, n_heads, t, hp),
        in_specs=[
            pl.BlockSpec((1, t, w), lambda b, g, i: (b, i, g)),
            pl.BlockSpec((1, seq, w), lambda b, g, i: (b, 0, ng + g)),
            pl.BlockSpec((1, seq, w), lambda b, g, i: (b, 0, 2 * ng + g),
                         pipeline_mode=pl.Buffered(1)),
            pl.BlockSpec((1, seq, LANES), lambda b, g, i: (b, 0, 0), pipeline_mode=pl.Buffered(1)),
        ] + cast_in_specs,
        out_specs=[pl.BlockSpec((1, t, w), lambda b, g, i: (b, i, g))] + cast_out_specs,
        out_shape=[jax.ShapeDtypeStruct((batch, seq, n_heads * head_dim), BF16)] + cast_shapes,
        scratch_shapes=[pltpu.VMEM((hp, seq, LANES), F32), pltpu.VMEM((hp, head_dim, seq), BF16),
                        pltpu.VMEM((hp, 1, t), F32), pltpu.VMEM((hp, 1, t), F32),
                        pltpu.VMEM((hp, head_dim, t), F32),
                        pltpu.VMEM((hp, t, t), F32), pltpu.VMEM((hp, t, t), F32),
                        pltpu.VMEM((hp, 1, t), F32), pltpu.VMEM((hp, 1, t), F32)],
        compiler_params=_params("parallel", "arbitrary", "arbitrary"),
        name="fox",
    )(qkv, qkv, qkv, fcum, *[c.src for c in casts])


def _gla_kernel(*refs, lives, n_heads, dk, dv):
    nc = len(lives)
    q_ref, k_ref, v_ref, r_ref, small_ref, wgu_ref, gb_ref, gain_ref = refs[:8]
    o_ref = refs[8 + nc]
    st_ref, la_ref, kd_ref, ea_ref = refs[9 + 2 * nc:]
    _run_casts(lives, refs[8:8 + nc], refs[9 + nc:9 + 2 * nc], 2)
    tb = q_ref.shape[1]
    n_chunks = tb // CHUNK

    @pl.when(pl.program_id(1) == 0)
    def _():
        st_ref[...] = jnp.zeros(st_ref.shape, F32)

    pre = _dot(small_ref[0].astype(BF16), wgu_ref[...]) + gb_ref[...]
    la_ref[...] = _log_sigmoid(pre) * (1.0 / GLA_GATE_TAU)

    row = lax.broadcasted_iota(jnp.int32, (CHUNK, CHUNK), 0)
    col = lax.broadcasted_iota(jnp.int32, (CHUNK, CHUNK), 1)
    after = (col > row).astype(F32)

    def prep(c, carry):
        rows = pl.ds(pl.multiple_of(c * CHUNK, CHUNK), CHUNK)
        la = la_ref[rows, :]
        rev = jnp.dot(after, la, precision=lax.Precision.HIGHEST, preferred_element_type=F32)
        kd_ref[rows, :] = (k_ref[0, rows, :] * jnp.exp(rev)).astype(BF16)
        ea_ref[c] = jnp.exp(jnp.sum(la, axis=0, keepdims=True))
        return carry

    lax.fori_loop(0, n_chunks, prep, 0)

    def step(c, carry):
        rows = pl.ds(pl.multiple_of(c * CHUNK, CHUNK), CHUNK)
        ea = ea_ref[c]
        for h in range(n_heads):
            ks = slice(h * dk, (h + 1) * dk)
            vs = slice(h * dv, (h + 1) * dv)
            kv_t = lax.dot_general(v_ref[0, rows, vs], kd_ref[rows, ks],
                                   (((0,), (0,)), ((), ())), preferred_element_type=F32)
            st = st_ref[h] * ea[:, ks] + kv_t
            st_ref[h] = st
            o = lax.dot_general(q_ref[0, rows, ks], st.astype(BF16),
                                (((1,), (1,)), ((), ())), preferred_element_type=F32)
            on = _rms(o, gain_ref[...])
            r = r_ref[0, rows, vs]
            o_ref[0, rows, vs] = (on * (r * jax.nn.sigmoid(r))).astype(o_ref.dtype)
        return carry

    lax.fori_loop(0, n_chunks, step, 0)


def _gla(q, k, v, r, small, wgu, gate_bias, gain, *, batch, seq, n_heads, dk, dv, tb, casts=()):
    kw, vw = n_heads * dk, n_heads * dv
    cast_in_specs, cast_out_specs, cast_shapes = _cast_specs(casts)
    return pl.pallas_call(
        functools.partial(_gla_kernel, lives=tuple(c.live for c in casts), n_heads=n_heads,
                          dk=dk, dv=dv),
        grid=(batch, seq // tb),
        in_specs=[
            pl.BlockSpec((1, tb, kw), lambda b, t: (b, t, 0)),
            pl.BlockSpec((1, tb, kw), lambda b, t: (b, t, 0)),
            pl.BlockSpec((1, tb, vw), lambda b, t: (b, t, 0)),
            pl.BlockSpec((1, tb, vw), lambda b, t: (b, t, 0)),
            pl.BlockSpec((1, tb, LANES), lambda b, t: (b, t, 0)),
            pl.BlockSpec((LANES, kw), lambda b, t: (0, 0)),
            pl.BlockSpec((1, kw), lambda b, t: (0, 0)),
            pl.BlockSpec((1, dv), lambda b, t: (0, 0)),
        ] + cast_in_specs,
        out_specs=[pl.BlockSpec((1, tb, vw), lambda b, t: (b, t, 0))] + cast_out_specs,
        out_shape=[jax.ShapeDtypeStruct((batch, seq, vw), BF16)] + cast_shapes,
        scratch_shapes=[
            pltpu.VMEM((n_heads, dv, dk), F32),
            pltpu.VMEM((tb, kw), F32),
            pltpu.VMEM((tb, kw), BF16),
            pltpu.VMEM((tb // CHUNK, 1, kw), F32),
        ],
        compiler_params=_params("parallel", "arbitrary"),
        name="gla",
    )(q, k, v, r, small, wgu, gate_bias, gain, *[c.src for c in casts])


def _merge_kernel(*refs, lives):
    nc = len(lives)
    (h_ref, g_ref, yf_ref, yg_ref, wgf_ref, wgg_ref, bgf_ref, bgg_ref,
     wa_ref, wb_ref, wo_ref) = refs[:11]
    o_ref, un_ref = refs[11 + nc], refs[12 + 2 * nc]
    _run_casts(lives, refs[11:11 + nc], refs[12 + nc:12 + 2 * nc], 2)

    @pl.when(pl.program_id(1) == 0)
    def _():
        h = h_ref[...]
        un_ref[...] = _rms(h, g_ref[...]).astype(BF16)
        o_ref[...] = h

    un = un_ref[...]
    g_fox = jax.nn.sigmoid(_dot(un, wgf_ref[...]) + bgf_ref[...])
    g_gla = jax.nn.sigmoid(_dot(un, wgg_ref[...]) + bgg_ref[...])
    merged = g_fox * _dot(yf_ref[...], wa_ref[...]) + g_gla * _dot(yg_ref[...], wb_ref[...])
    o_ref[...] += _dot(merged.astype(BF16), wo_ref[...])


def _merge(h, g, y_fox, y_gla, w_gate, b_gate, w_a, w_b, w_o, *, tm, tn, casts=()):
    n, d = h.shape
    nj = d // tn
    cast_in_specs, cast_out_specs, cast_shapes = _cast_specs(casts)
    return pl.pallas_call(
        functools.partial(_merge_kernel, lives=tuple(c.live for c in casts)),
        grid=(n // tm, nj),
        in_specs=[
            pl.BlockSpec((tm, d), lambda i, j: (i, 0)),
            pl.BlockSpec((1, d), lambda i, j: (0, 0)),
            pl.BlockSpec((tm, y_fox.shape[1]), lambda i, j: (i, 0)),
            pl.BlockSpec((tm, y_gla.shape[1]), lambda i, j: (i, 0)),
            pl.BlockSpec((d, tn), lambda i, j: (0, j)),
            pl.BlockSpec((d, tn), lambda i, j: (0, nj + j)),
            pl.BlockSpec((1, tn), lambda i, j: (0, j)),
            pl.BlockSpec((1, tn), lambda i, j: (0, nj + j)),
            pl.BlockSpec((w_a.shape[0], tn), lambda i, j: (0, j)),
            pl.BlockSpec((w_b.shape[0], tn), lambda i, j: (0, j)),
            pl.BlockSpec((tn, d), lambda i, j: (j, 0)),
        ] + cast_in_specs,
        out_specs=[pl.BlockSpec((tm, d), lambda i, j: (i, 0))] + cast_out_specs,
        out_shape=[jax.ShapeDtypeStruct((n, d), F32)] + cast_shapes,
        scratch_shapes=[pltpu.VMEM((tm, d), BF16)],
        compiler_params=_params("parallel", "arbitrary"),
        name="merge",
    )(h, g, y_fox, y_gla, w_gate, w_gate, b_gate, b_gate, w_a, w_b, w_o, *[c.src for c in casts])


def _ple_kernel(h_ref, p_ref, gp_ref, gf_ref, wpg_ref, wpe_ref, o_ref, *, final):
    h = h_ref[...]
    hn = _rms(h, gp_ref[...]).astype(BF16)
    gate = jax.nn.sigmoid(_dot(hn, wpg_ref[...]))
    out = h + gate * _dot(p_ref[...].astype(BF16), wpe_ref[...])
    if final:
        out = _rms(out, gf_ref[...])
    o_ref[...] = out


def _ple(h, p, g_ple, g_final, w_gate, w_proj, *, tm, final):
    n, d = h.shape
    dp = p.shape[1]
    return pl.pallas_call(
        functools.partial(_ple_kernel, final=final),
        grid=(n // tm,),
        in_specs=[
            pl.BlockSpec((tm, d), lambda i: (i, 0)),
            pl.BlockSpec((tm, dp), lambda i: (i, 0)),
            pl.BlockSpec((1, d), lambda i: (0, 0)),
            pl.BlockSpec((1, d), lambda i: (0, 0)),
            pl.BlockSpec((d, d), lambda i: (0, 0)),
            pl.BlockSpec((dp, d), lambda i: (0, 0)),
        ],
        out_specs=pl.BlockSpec((tm, d), lambda i: (i, 0)),
        out_shape=jax.ShapeDtypeStruct((n, d), F32),
        compiler_params=_params("parallel"),
        name="ple",
    )(h, p, g_ple, g_final, w_gate, w_proj)


def _tile(n, want):
    t = min(n, want)
    assert n % t == 0, (n, want)
    return t


def kernel(x, p, ffn1_norm, ffn1_w_gate, ffn1_w_up, ffn1_w_down, mix_norm, w_in, fox_forget_bias, gla_gate_up, gla_gate_bias, gla_head_norm, w_branch_fox, w_branch_gla, w_merge_gate, b_merge_gate, w_out, ffn2_norm, ffn2_w_gate, ffn2_w_up, ffn2_w_down, ple_norm, w_ple_proj, w_ple_gate, final_norm):
    batch, seq, d = x.shape
    depth = p.shape[0]
    n = batch * seq
    fox_heads = fox_forget_bias.shape[-1]
    fox_dim = w_branch_fox.shape[1] // fox_heads
    fox_w = fox_heads * fox_dim
    rank, gla_kw = gla_gate_up.shape[1:]
    gla_dv = gla_head_norm.shape[-1]
    gla_vw = w_branch_gla.shape[1]
    gla_heads = gla_vw // gla_dv
    gla_dk = gla_kw // gla_heads
    assert fox_heads + rank <= LANES

    tn = 512
    assert w_in.shape[-1] == 3 * fox_w + fox_heads + 2 * gla_kw + 2 * gla_vw + rank
    assert (3 * fox_w) % tn == 0 and gla_kw % tn == 0 and gla_vw % tn == 0
    assert fox_heads % 8 == 0 and rank % 8 == 0
    lead = 3 * fox_w // tn
    kt, vt = gla_kw // tn, gla_vw // tn
    proj_outs = (_ProjOut(0, lead, 1.0, BF16),
                 _ProjOut(lead, kt, gla_dk ** -0.5, BF16),
                 _ProjOut(lead + kt, kt, 1.0, F32),
                 _ProjOut(lead + 2 * kt, vt, 1.0, BF16),
                 _ProjOut(lead + 2 * kt + vt, vt, 1.0, F32))
    proj_rows = (lead + 2 * kt + 2 * vt) * tn
    fl_rows = (3 * fox_w, fox_heads)
    gd_rows = (w_in.shape[-1] - rank, rank)

    row = lambda v: v.reshape(1, -1).astype(F32)
    h = x.reshape(n, d)
    tm_ffn = _tile(n, 1024)
    tf = _tile(ffn1_w_gate.shape[-1], 512)
    tm_proj = _tile(n, 1024)
    tm_merge = _tile(n, 512)
    tm_ple = _tile(n, 512)
    t_fox = _tile(seq, 512)
    fox_hp = 4
    tb_gla = _tile(seq, 512)

    tf_first = _tile(ffn1_w_gate.shape[-1], 256)
    ffn_rest_grid = (n // tm_ffn - 1, ffn1_w_gate.shape[-1] // tf)
    fox_grid = _fox_grid(batch, seq, fox_heads, t_fox, fox_hp)
    gla_grid = (batch, seq // tb_gla)

    for i in range(depth):
        fbias = jnp.zeros((1, LANES), F32).at[0, :fox_heads].set(fox_forget_bias[i])
        wgu = jnp.zeros((LANES, gla_kw), F32).at[fox_heads:fox_heads + rank].set(gla_gate_up[i]).astype(BF16)

        w_in_t = jnp.swapaxes(w_in, 1, 2)

        h1, w1g, w1u, w1d = _ffn_first(h, row(ffn1_norm[i]), ffn1_w_gate[i], ffn1_w_up[i],
                                       ffn1_w_down[i], tm=tm_ffn, tf=tf_first)
        if ffn_rest_grid[0]:
            h, w_proj = _ffn(h, row(ffn1_norm[i]), w1g, w1u, w1d, tm=tm_ffn, tf=tf, first_tile=1,
                             into=h1, casts=(_cast_compact_rows(
                                 w_in_t, i, ffn_rest_grid, 3 * fox_w, fox_heads, proj_rows),))
        else:
            h, w_proj = h1, jnp.concatenate(
                [w_in_t[i, :3 * fox_w], w_in_t[i, 3 * fox_w + fox_heads:][:proj_rows - 3 * fox_w]]
            ).astype(BF16)

        fox_qkv, gq, gk, gv, gr, small = _proj(
            h, row(mix_norm[i]), w_proj, w_in_t, i, tm=tm_proj, tn=tn, outs=proj_outs,
            fl_rows=fl_rows, gd_rows=gd_rows)

        fcum = _fcum(small, fbias, batch=batch, seq=seq)
        y_fox, w2g, w2u, wpg, wpe = _fox(
            fox_qkv.reshape(batch, seq, -1), fcum, batch=batch, seq=seq, n_heads=fox_heads,
            head_dim=fox_dim, t=t_fox, hp=fox_hp,
            casts=tuple(_cast_row_blocks(w, fox_grid) for w in (
                ffn2_w_gate[i], ffn2_w_up[i], w_ple_gate[i], w_ple_proj[i])))

        bsd = lambda a: a.reshape(batch, seq, -1)
        y_gla, wmg, wbf, wbg, wo, w2d = _gla(
            bsd(gq), bsd(gk), bsd(gv), bsd(gr), bsd(small), wgu, row(gla_gate_bias[i]),
            row(gla_head_norm[i]), batch=batch, seq=seq, n_heads=gla_heads, dk=gla_dk, dv=gla_dv,
            tb=tb_gla,
            casts=tuple(_cast_row_blocks(w, gla_grid) for w in (
                w_merge_gate[i], w_branch_fox[i], w_branch_gla[i], w_out[i], ffn2_w_down[i])))

        h, = _merge(h, row(mix_norm[i]), y_fox.reshape(n, fox_w), y_gla.reshape(n, gla_vw),
                    wmg, row(b_merge_gate[i]), wbf, wbg, wo, tm=tm_merge, tn=tn)

        h, = _ffn(h, row(ffn2_norm[i]), w2g, w2u, w2d, tm=tm_ffn, tf=tf)

        h = _ple(h, p[i].reshape(n, -1), row(ple_norm[i]), row(final_norm), wpg, wpe,
                 tm=tm_ple, final=(i == depth - 1))

    return h.reshape(batch, seq, d)
```

```python
import functools
import math
from typing import Callable, NamedTuple

import jax
import jax.numpy as jnp
from jax import lax
from jax.experimental import pallas as pl
from jax.experimental.pallas import tpu as pltpu

EPS = 1e-6
CHUNK = 64
GLA_GATE_TAU = 16.0
LANES = 128
V7X_VMEM_LIMIT_BYTES = 62 * 1024 * 1024
LOG2E = math.log2(math.e)

F32 = jnp.float32
BF16 = jnp.bfloat16


def _rms(x, g):
    return x * lax.rsqrt(jnp.mean(x * x, axis=-1, keepdims=True) + EPS) * g


def _log_sigmoid(z):
    return jnp.minimum(z, 0.0) - jnp.log1p(jnp.exp(-jnp.abs(z)))


def _dot(a, b):
    return lax.dot_general(a, b, (((1,), (0,)), ((), ())), preferred_element_type=F32)


def _params(*sem):
    return pltpu.CompilerParams(dimension_semantics=sem,
                                vmem_limit_bytes=V7X_VMEM_LIMIT_BYTES)


class _Cast(NamedTuple):
    src: jax.Array
    src_spec: pl.BlockSpec
    dst_spec: pl.BlockSpec
    dst_shape: tuple
    live: Callable


def _flat_step(grid):
    def step(*ids):
        s = ids[0]
        for size, idx in zip(grid[1:], ids[1:]):
            s = s * size + idx
        return s
    return step


def _row_block_count(rows, n_steps, extra=lambda rb: True):
    ok = lambda k: rows % k == 0 and (rows // k) % 16 == 0 and extra(rows // k)
    return max(k for k in range(1, n_steps + 1) if ok(k))


def _cast_row_blocks(src, grid):
    r, c = src.shape
    step = _flat_step(grid)
    nb = _row_block_count(r, math.prod(grid))
    spec = pl.BlockSpec((r // nb, c), lambda *ids: (jnp.minimum(step(*ids), nb - 1), 0))
    return _Cast(src, spec, spec, src.shape, lambda *ids: step(*ids) < nb)


def _cast_compact_rows(src, layer, grid, lead_rows, skip, total_rows):
    d = src.shape[2]
    step = _flat_step(grid)
    nb = _row_block_count(total_rows, math.prod(grid), lambda rb: lead_rows % rb == 0)
    rb = total_rows // nb
    blk = lambda *ids: jnp.minimum(step(*ids), nb - 1)
    src_row = lambda *ids: pl.multiple_of(
        blk(*ids) * rb + jnp.where(blk(*ids) * rb < lead_rows, 0, skip), 8)
    return _Cast(src,
                 pl.BlockSpec((None, pl.Element(rb), pl.Element(d)),
                              lambda *ids: (layer, src_row(*ids), 0)),
                 pl.BlockSpec((rb, d), lambda *ids: (blk(*ids), 0)),
                 (total_rows, d),
                 lambda *ids: step(*ids) < nb)


def _cast_specs(casts):
    shapes = [jax.ShapeDtypeStruct(c.dst_shape, BF16) for c in casts]
    return [c.src_spec for c in casts], [c.dst_spec for c in casts], shapes


def _run_casts(lives, src_refs, dst_refs, n_axes):
    ids = [pl.program_id(a) for a in range(n_axes)]
    for live, src, dst in zip(lives, src_refs, dst_refs):
        @pl.when(live(*ids))
        def _():
            dst[...] = src[...].astype(BF16)


def _ffn_step(x_ref, g_ref, o_ref, xn_ref, wg, wu, wd):
    @pl.when(pl.program_id(1) == 0)
    def _():
        x = x_ref[...]
        xn_ref[...] = _rms(x, g_ref[...]).astype(BF16)
        o_ref[...] = x

    xn = xn_ref[...]
    gate = _dot(xn, wg())
    up = _dot(xn, wu())
    hid = (0.5 * (gate * jax.nn.sigmoid(gate)) * up).astype(BF16)
    o_ref[...] += _dot(hid, wd())


def _ffn_kernel(*refs, lives, n_extra):
    nc = len(lives)
    x_ref, g_ref, wg_ref, wu_ref, wd_ref = refs[:5]
    o_ref, xn_ref = refs[5 + nc + n_extra], refs[-1]
    _ffn_step(x_ref, g_ref, o_ref, xn_ref,
              lambda: wg_ref[...], lambda: wu_ref[...], lambda: wd_ref[...])
    _run_casts(lives, refs[5:5 + nc], refs[6 + nc + n_extra:6 + 2 * nc + n_extra], 2)


def _ffn(x, g, wg, wu, wd, *, tm, tf, casts=(), first_tile=0, into=None):
    n, d = x.shape
    f = wg.shape[1]
    cast_in_specs, cast_out_specs, cast_shapes = _cast_specs(casts)
    extra = [] if into is None else [into]
    tok = lambda i, j: (i + first_tile, 0)
    return pl.pallas_call(
        functools.partial(_ffn_kernel, lives=tuple(c.live for c in casts), n_extra=len(extra)),
        grid=(n // tm - first_tile, f // tf),
        in_specs=[
            pl.BlockSpec((tm, d), tok),
            pl.BlockSpec((1, d), lambda i, j: (0, 0)),
            pl.BlockSpec((d, tf), lambda i, j: (0, j)),
            pl.BlockSpec((d, tf), lambda i, j: (0, j)),
            pl.BlockSpec((tf, d), lambda i, j: (j, 0)),
        ] + cast_in_specs + [pl.BlockSpec(memory_space=pl.ANY)] * len(extra),
        out_specs=[pl.BlockSpec((tm, d), tok)] + cast_out_specs,
        out_shape=[jax.ShapeDtypeStruct((n, d), F32)] + cast_shapes,
        input_output_aliases={5 + len(casts): 0} if extra else {},
        scratch_shapes=[pltpu.VMEM((tm, d), BF16)],
        compiler_params=_params("parallel", "arbitrary"),
        name="ffn",
    )(x, g, wg, wu, wd, *[c.src for c in casts], *extra)


def _ffn_first_kernel(x_ref, g_ref, wg_ref, wu_ref, wd_ref, o_ref, wgb_ref, wub_ref, wdb_ref,
                      xn_ref):
    def cast(src, dst):
        def thunk():
            w = src[...].astype(BF16)
            dst[...] = w
            return w
        return thunk

    _ffn_step(x_ref, g_ref, o_ref, xn_ref, cast(wg_ref, wgb_ref), cast(wu_ref, wub_ref),
              cast(wd_ref, wdb_ref))


def _ffn_first(x, g, wg, wu, wd, *, tm, tf):
    n, d = x.shape
    f = wg.shape[1]
    w_specs = [pl.BlockSpec((d, tf), lambda i, j: (0, j)), pl.BlockSpec((d, tf), lambda i, j: (0, j)),
               pl.BlockSpec((tf, d), lambda i, j: (j, 0))]
    return pl.pallas_call(
        _ffn_first_kernel,
        grid=(1, f // tf),
        in_specs=[pl.BlockSpec((tm, d), lambda i, j: (0, 0)),
                  pl.BlockSpec((1, d), lambda i, j: (0, 0))] + w_specs,
        out_specs=[pl.BlockSpec((tm, d), lambda i, j: (0, 0))] + w_specs,
        out_shape=[jax.ShapeDtypeStruct((n, d), F32)]
        + [jax.ShapeDtypeStruct(w.shape, BF16) for w in (wg, wu, wd)],
        scratch_shapes=[pltpu.VMEM((tm, d), BF16)],
        compiler_params=_params("arbitrary", "arbitrary"),
        name="ffn_first",
    )(x, g, wg, wu, wd)


class _ProjOut(NamedTuple):
    first: int
    count: int
    scale: float
    dtype: type


def _dot_nt(a, b):
    return lax.dot_general(a, b, (((1,), (1,)), ((), ())), preferred_element_type=F32)


def _proj_kernel(*refs, lives, outs, n_fl, rank):
    nc = len(lives)
    h_ref, g_ref, w_ref, wfl_ref, wgd_ref = refs[:5]
    out_refs = refs[5 + nc:5 + nc + len(outs)]
    small_ref = refs[5 + nc + len(outs)]
    un_ref = refs[-1]
    j = pl.program_id(1)
    _run_casts(lives, refs[5:5 + nc], refs[6 + nc + len(outs):6 + 2 * nc + len(outs)], 2)

    @pl.when(j == 0)
    def _():
        un = _rms(h_ref[...], g_ref[...]).astype(BF16)
        un_ref[...] = un
        small_ref[...] = jnp.zeros(small_ref.shape, F32)
        small_ref[:, :n_fl] = _dot_nt(un, wfl_ref[...])
        small_ref[:, n_fl:n_fl + rank] = _dot_nt(un, wgd_ref[...])

    for o_ref, o in zip(out_refs, outs):
        @pl.when((j >= o.first) & (j < o.first + o.count))
        def _():
            tile = _dot_nt(un_ref[...], w_ref[...])
            if o.scale != 1.0:
                tile = tile * o.scale
            o_ref[...] = tile.astype(o_ref.dtype)


def _proj(h, g, w_main, w_in_t, layer, *, tm, tn, outs, fl_rows, gd_rows, casts=()):
    n, d = h.shape
    n_steps = sum(o.count for o in outs)
    assert fl_rows[1] + gd_rows[1] <= LANES
    cast_in_specs, cast_out_specs, cast_shapes = _cast_specs(casts)
    kern = functools.partial(_proj_kernel, lives=tuple(c.live for c in casts), outs=outs,
                             n_fl=fl_rows[1], rank=gd_rows[1])
    rows = lambda start, size: pl.BlockSpec(
        (None, pl.Element(size), pl.Element(d)), lambda i, j: (layer, start, 0))

    def out_spec(o):
        return pl.BlockSpec((None, tm, tn),
                            lambda i, j: (jnp.clip(j - o.first, 0, o.count - 1), i, 0))

    return pl.pallas_call(
        kern,
        grid=(n // tm, n_steps),
        in_specs=[
            pl.BlockSpec((tm, d), lambda i, j: (i, 0)),
            pl.BlockSpec((1, d), lambda i, j: (0, 0)),
            pl.BlockSpec((tn, d), lambda i, j: (j, 0)),
            rows(*fl_rows),
            rows(*gd_rows),
        ] + cast_in_specs,
        out_specs=[out_spec(o) for o in outs]
        + [pl.BlockSpec((tm, LANES), lambda i, j: (i, 0))] + cast_out_specs,
        out_shape=[jax.ShapeDtypeStruct((o.count, n, tn), o.dtype) for o in outs]
        + [jax.ShapeDtypeStruct((n, LANES), F32)] + cast_shapes,
        scratch_shapes=[pltpu.VMEM((tm, d), BF16)],
        compiler_params=_params("parallel", "arbitrary"),
        name="proj",
    )(h, g, w_main, w_in_t, w_in_t, *[c.src for c in casts])


def _fcum_kernel(small_ref, bias_ref, f_ref):
    s = small_ref.shape[0]
    f_ref[0] = _log_sigmoid(small_ref[...] + bias_ref[...])
    row = lax.broadcasted_iota(jnp.int32, (LANES, LANES), 0)
    col = lax.broadcasted_iota(jnp.int32, (LANES, LANES), 1)
    tril = (row >= col).astype(F32)

    def body(r, carry):
        rows = pl.ds(pl.multiple_of(r * LANES, LANES), LANES)
        c = jnp.dot(tril, f_ref[0, rows, :], precision=lax.Precision.HIGHEST,
                    preferred_element_type=F32) + carry
        f_ref[0, rows, :] = c
        return c[LANES - 1:LANES, :]

    lax.fori_loop(0, s // LANES, body, jnp.zeros((1, LANES), F32))


def _fcum(small, bias_row, *, batch, seq):
    return pl.pallas_call(
        _fcum_kernel,
        grid=(batch,),
        in_specs=[
            pl.BlockSpec((seq, LANES), lambda b: (b, 0)),
            pl.BlockSpec((1, LANES), lambda b: (0, 0)),
        ],
        out_specs=pl.BlockSpec((1, seq, LANES), lambda b: (b, 0, 0)),
        out_shape=jax.ShapeDtypeStruct((batch, seq, LANES), F32),
        compiler_params=_params("parallel"),
        name="fcum",
    )(small, bias_row)


def _fox_kernel(*refs, lives, t, c2, hp, dh):
    nc = len(lives)
    q_ref, k_ref, v_ref, f_ref = refs[:4]
    o_ref = refs[4 + nc]
    (frep_ref, vt_ref, m_ref, l_ref, acc_ref, s0_ref, s1_ref, mc0_ref, mc1_ref) = refs[5 + 2 * nc:]
    _run_casts(lives, refs[4:4 + nc], refs[5 + nc:5 + 2 * nc], 3)
    g = pl.program_id(1)
    i = pl.program_id(2)
    s_refs = (s0_ref, s1_ref)
    mc_refs = (mc0_ref, mc1_ref)

    @pl.when(i == 0)
    def _():
        row = lax.broadcasted_iota(jnp.int32, (LANES, LANES), 0)
        for h in range(hp):
            onehot = (row == g * hp + h).astype(F32)
            frep_ref[h] = jnp.dot(f_ref[0], onehot, precision=lax.Precision.HIGHEST,
                                  preferred_element_type=F32)
            vt_ref[h] = v_ref[0, :, h * dh:(h + 1) * dh].astype(F32).T.astype(BF16)

    q0 = pl.multiple_of(i * t, t)
    m_ref[...] = jnp.full(m_ref.shape, -jnp.inf, F32)
    l_ref[...] = jnp.zeros(l_ref.shape, F32)
    acc_ref[...] = jnp.zeros(acc_ref.shape, F32)
    reps = t // LANES

    def scores(kk, slot, masked):
        k0 = pl.multiple_of(kk * t, t)
        for h in range(hp):
            hs = slice(h * dh, (h + 1) * dh)
            f_base = frep_ref[h, pl.ds(q0, 8), :][0:1, :]
            bias = (f_base - frep_ref[h, pl.ds(k0, t), :]) * LOG2E
            s = lax.dot_general(k_ref[0, pl.ds(k0, t), hs], q_ref[0, :, hs],
                                (((1,), (1,)), ((), ())), preferred_element_type=F32)
            s = s * c2 + jnp.tile(bias, (1, reps))
            if masked:
                key = lax.broadcasted_iota(jnp.int32, (t, t), 0)
                qry = lax.broadcasted_iota(jnp.int32, (t, t), 1)
                s = jnp.where(key <= qry, s, -jnp.inf)
            s_refs[slot][h] = s
            mc_refs[slot][h] = jnp.max(s, axis=0, keepdims=True)

    def absorb(kk, slot):
        k0 = pl.multiple_of(kk * t, t)
        for h in range(hp):
            m_old = m_ref[h]
            m_new = jnp.maximum(m_old, mc_refs[slot][h])
            alpha = jnp.exp2(m_old - m_new)
            p = jnp.exp2(s_refs[slot][h] - m_new)
            l_ref[h] = alpha * l_ref[h] + jnp.sum(p, axis=0, keepdims=True)
            acc_ref[h] = alpha * acc_ref[h] + _dot(vt_ref[h, :, pl.ds(k0, t)], p.astype(BF16))
            m_ref[h] = m_new

    scores(i, 0, True)

    def body(kk, carry):
        prev = jnp.where(kk == 0, i, kk - 1)
        for par in range(2):
            @pl.when(kk % 2 == par)
            def _():
                absorb(prev, par)
                scores(kk, 1 - par, False)
        return carry

    lax.fori_loop(0, i, body, 0)
    last = jnp.where(i == 0, i, i - 1)
    for par in range(2):
        @pl.when(i % 2 == par)
        def _():
            absorb(last, par)
    for h in range(hp):
        o_ref[0, :, h * dh:(h + 1) * dh] = (acc_ref[h] / l_ref[h]).T.astype(o_ref.dtype)


def _fox_grid(batch, seq, n_heads, t, hp):
    return (batch, n_heads // hp, seq // t)


def _fox(qkv, fcum, *, batch, seq, n_heads, head_dim, t, hp, casts=()):
    assert head_dim == LANES and n_heads % hp == 0
    c2 = (head_dim ** -0.5) * LOG2E
    ng = n_heads // hp
    w = hp * head_dim
    assert qkv.shape == (3 * ng, batch, seq, w)
    cast_in_specs, cast_out_specs, cast_shapes = _cast_specs(casts)
    return pl.pallas_call(
        functools.partial(_fox_kernel, lives=tuple(c.live for c in casts), t=t, c2=c2, hp=hp,
                          dh=head_dim),
        grid=_fox_grid(batch, seq, n_heads, t, hp),
        in_specs=[
            pl.BlockSpec((None, 1, t, w), lambda b, g, i: (g, b, i, 0)),
            pl.BlockSpec((None, 1, seq, w), lambda b, g, i: (ng + g, b, 0, 0)),
            pl.BlockSpec((None, 1, seq, w), lambda b, g, i: (2 * ng + g, b, 0, 0),
                         pipeline_mode=pl.Buffered(1)),
            pl.BlockSpec((1, seq, LANES), lambda b, g, i: (b, 0, 0), pipeline_mode=pl.Buffered(1)),
        ] + cast_in_specs,
        out_specs=[pl.BlockSpec((1, t, w), lambda b, g, i: (b, i, g))] + cast_out_specs,
        out_shape=[jax.ShapeDtypeStruct((batch, seq, n_heads * head_dim), BF16)] + cast_shapes,
        scratch_shapes=[pltpu.VMEM((hp, seq, LANES), F32), pltpu.VMEM((hp, head_dim, seq), BF16),
                        pltpu.VMEM((hp, 1, t), F32), pltpu.VMEM((hp, 1, t), F32),
                        pltpu.VMEM((hp, head_dim, t), F32),
                        pltpu.VMEM((hp, t, t), F32), pltpu.VMEM((hp, t, t), F32),
                        pltpu.VMEM((hp, 1, t), F32), pltpu.VMEM((hp, 1, t), F32)],
        compiler_params=_params("parallel", "arbitrary", "arbitrary"),
        name="fox",
    )(qkv, qkv, qkv, fcum, *[c.src for c in casts])


def _gla_kernel(*refs, lives, n_heads, dk, dv):
    nc = len(lives)
    q_ref, k_ref, v_ref, r_ref, small_ref, wgu_ref, gb_ref, gain_ref = refs[:8]
    o_ref = refs[8 + nc]
    st_ref, la_ref, kd_ref, ea_ref = refs[9 + 2 * nc:]
    _run_casts(lives, refs[8:8 + nc], refs[9 + nc:9 + 2 * nc], 2)
    tb, tn = q_ref.shape[2:]
    n_chunks = tb // CHUNK

    def cols(ref, rows, start, size):
        return ref[start // tn, 0, rows, start % tn:start % tn + size]

    @pl.when(pl.program_id(1) == 0)
    def _():
        st_ref[...] = jnp.zeros(st_ref.shape, F32)

    pre = _dot(small_ref[0].astype(BF16), wgu_ref[...]) + gb_ref[...]
    la_ref[...] = _log_sigmoid(pre) * (1.0 / GLA_GATE_TAU)

    row = lax.broadcasted_iota(jnp.int32, (CHUNK, CHUNK), 0)
    col = lax.broadcasted_iota(jnp.int32, (CHUNK, CHUNK), 1)
    after = (col > row).astype(F32)

    def prep(c, carry):
        rows = pl.ds(pl.multiple_of(c * CHUNK, CHUNK), CHUNK)
        la = la_ref[rows, :]
        rev = jnp.dot(after, la, precision=lax.Precision.HIGHEST, preferred_element_type=F32)
        for tt in range(k_ref.shape[0]):
            ts = slice(tt * tn, (tt + 1) * tn)
            kd_ref[rows, ts] = (k_ref[tt, 0, rows, :] * jnp.exp(rev[:, ts])).astype(BF16)
        ea_ref[c] = jnp.exp(jnp.sum(la, axis=0, keepdims=True))
        return carry

    lax.fori_loop(0, n_chunks, prep, 0)

    def step(c, carry):
        rows = pl.ds(pl.multiple_of(c * CHUNK, CHUNK), CHUNK)
        ea = ea_ref[c]
        for h in range(n_heads):
            ks = slice(h * dk, (h + 1) * dk)
            vs = slice(h * dv, (h + 1) * dv)
            kv_t = lax.dot_general(cols(v_ref, rows, h * dv, dv), kd_ref[rows, ks],
                                   (((0,), (0,)), ((), ())), preferred_element_type=F32)
            st = st_ref[h] * ea[:, ks] + kv_t
            st_ref[h] = st
            o = lax.dot_general(cols(q_ref, rows, h * dk, dk), st.astype(BF16),
                                (((1,), (1,)), ((), ())), preferred_element_type=F32)
            on = _rms(o, gain_ref[...])
            r = cols(r_ref, rows, h * dv, dv)
            o_ref[0, rows, vs] = (on * (r * jax.nn.sigmoid(r))).astype(o_ref.dtype)
        return carry

    lax.fori_loop(0, n_chunks, step, 0)


def _gla(q, k, v, r, small, wgu, gate_bias, gain, *, batch, seq, n_heads, dk, dv, tb, casts=()):
    kw, vw = n_heads * dk, n_heads * dv
    tn = q.shape[-1]
    assert tn % dk == 0 and tn % dv == 0
    tiles = lambda a: pl.BlockSpec((a.shape[0], 1, tb, tn), lambda b, t: (0, b, t, 0))
    cast_in_specs, cast_out_specs, cast_shapes = _cast_specs(casts)
    return pl.pallas_call(
        functools.partial(_gla_kernel, lives=tuple(c.live for c in casts), n_heads=n_heads,
                          dk=dk, dv=dv),
        grid=(batch, seq // tb),
        in_specs=[
            tiles(q), tiles(k), tiles(v), tiles(r),
            pl.BlockSpec((1, tb, LANES), lambda b, t: (b, t, 0)),
            pl.BlockSpec((LANES, kw), lambda b, t: (0, 0)),
            pl.BlockSpec((1, kw), lambda b, t: (0, 0)),
            pl.BlockSpec((1, dv), lambda b, t: (0, 0)),
        ] + cast_in_specs,
        out_specs=[pl.BlockSpec((1, tb, vw), lambda b, t: (b, t, 0))] + cast_out_specs,
        out_shape=[jax.ShapeDtypeStruct((batch, seq, vw), BF16)] + cast_shapes,
        scratch_shapes=[
            pltpu.VMEM((n_heads, dv, dk), F32),
            pltpu.VMEM((tb, kw), F32),
            pltpu.VMEM((tb, kw), BF16),
            pltpu.VMEM((tb // CHUNK, 1, kw), F32),
        ],
        compiler_params=_params("parallel", "arbitrary"),
        name="gla",
    )(q, k, v, r, small, wgu, gate_bias, gain, *[c.src for c in casts])


def _merge_kernel(*refs, lives):
    nc = len(lives)
    (h_ref, g_ref, yf_ref, yg_ref, wgf_ref, wgg_ref, bgf_ref, bgg_ref,
     wa_ref, wb_ref, wo_ref) = refs[:11]
    o_ref, un_ref = refs[11 + nc], refs[12 + 2 * nc]
    _run_casts(lives, refs[11:11 + nc], refs[12 + nc:12 + 2 * nc], 2)

    @pl.when(pl.program_id(1) == 0)
    def _():
        h = h_ref[...]
        un_ref[...] = _rms(h, g_ref[...]).astype(BF16)
        o_ref[...] = h

    un = un_ref[...]
    g_fox = jax.nn.sigmoid(_dot(un, wgf_ref[...]) + bgf_ref[...])
    g_gla = jax.nn.sigmoid(_dot(un, wgg_ref[...]) + bgg_ref[...])
    merged = g_fox * _dot(yf_ref[...], wa_ref[...]) + g_gla * _dot(yg_ref[...], wb_ref[...])
    o_ref[...] += _dot(merged.astype(BF16), wo_ref[...])


def _merge(h, g, y_fox, y_gla, w_gate, b_gate, w_a, w_b, w_o, *, tm, tn, casts=()):
    n, d = h.shape
    nj = d // tn
    cast_in_specs, cast_out_specs, cast_shapes = _cast_specs(casts)
    return pl.pallas_call(
        functools.partial(_merge_kernel, lives=tuple(c.live for c in casts)),
        grid=(n // tm, nj),
        in_specs=[
            pl.BlockSpec((tm, d), lambda i, j: (i, 0)),
            pl.BlockSpec((1, d), lambda i, j: (0, 0)),
            pl.BlockSpec((tm, y_fox.shape[1]), lambda i, j: (i, 0)),
            pl.BlockSpec((tm, y_gla.shape[1]), lambda i, j: (i, 0)),
            pl.BlockSpec((d, tn), lambda i, j: (0, j)),
            pl.BlockSpec((d, tn), lambda i, j: (0, nj + j)),
            pl.BlockSpec((1, tn), lambda i, j: (0, j)),
            pl.BlockSpec((1, tn), lambda i, j: (0, nj + j)),
            pl.BlockSpec((w_a.shape[0], tn), lambda i, j: (0, j)),
            pl.BlockSpec((w_b.shape[0], tn), lambda i, j: (0, j)),
            pl.BlockSpec((tn, d), lambda i, j: (j, 0)),
        ] + cast_in_specs,
        out_specs=[pl.BlockSpec((tm, d), lambda i, j: (i, 0))] + cast_out_specs,
        out_shape=[jax.ShapeDtypeStruct((n, d), F32)] + cast_shapes,
        scratch_shapes=[pltpu.VMEM((tm, d), BF16)],
        compiler_params=_params("parallel", "arbitrary"),
        name="merge",
    )(h, g, y_fox, y_gla, w_gate, w_gate, b_gate, b_gate, w_a, w_b, w_o, *[c.src for c in casts])


def _ple_kernel(h_ref, p_ref, gp_ref, gf_ref, wpg_ref, wpe_ref, o_ref, *, final):
    h = h_ref[...]
    hn = _rms(h, gp_ref[...]).astype(BF16)
    gate = jax.nn.sigmoid(_dot(hn, wpg_ref[...]))
    out = h + gate * _dot(p_ref[...].astype(BF16), wpe_ref[...])
    if final:
        out = _rms(out, gf_ref[...])
    o_ref[...] = out


def _ple(h, p, g_ple, g_final, w_gate, w_proj, *, tm, final):
    n, d = h.shape
    dp = p.shape[1]
    return pl.pallas_call(
        functools.partial(_ple_kernel, final=final),
        grid=(n // tm,),
        in_specs=[
            pl.BlockSpec((tm, d), lambda i: (i, 0)),
            pl.BlockSpec((tm, dp), lambda i: (i, 0)),
            pl.BlockSpec((1, d), lambda i: (0, 0)),
            pl.BlockSpec((1, d), lambda i: (0, 0)),
            pl.BlockSpec((d, d), lambda i: (0, 0)),
            pl.BlockSpec((dp, d), lambda i: (0, 0)),
        ],
        out_specs=pl.BlockSpec((tm, d), lambda i: (i, 0)),
        out_shape=jax.ShapeDtypeStruct((n, d), F32),
        compiler_params=_params("parallel"),
        name="ple",
    )(h, p, g_ple, g_final, w_gate, w_proj)


def _tile(n, want):
    t = min(n, want)
    assert n % t == 0, (n, want)
    return t


def kernel(x, p, ffn1_norm, ffn1_w_gate, ffn1_w_up, ffn1_w_down, mix_norm, w_in, fox_forget_bias, gla_gate_up, gla_gate_bias, gla_head_norm, w_branch_fox, w_branch_gla, w_merge_gate, b_merge_gate, w_out, ffn2_norm, ffn2_w_gate, ffn2_w_up, ffn2_w_down, ple_norm, w_ple_proj, w_ple_gate, final_norm):
    batch, seq, d = x.shape
    depth = p.shape[0]
    n = batch * seq
    fox_heads = fox_forget_bias.shape[-1]
    fox_dim = w_branch_fox.shape[1] // fox_heads
    fox_w = fox_heads * fox_dim
    rank, gla_kw = gla_gate_up.shape[1:]
    gla_dv = gla_head_norm.shape[-1]
    gla_vw = w_branch_gla.shape[1]
    gla_heads = gla_vw // gla_dv
    gla_dk = gla_kw // gla_heads
    assert fox_heads + rank <= LANES

    tn = 512
    assert w_in.shape[-1] == 3 * fox_w + fox_heads + 2 * gla_kw + 2 * gla_vw + rank
    assert (3 * fox_w) % tn == 0 and gla_kw % tn == 0 and gla_vw % tn == 0
    assert fox_heads % 8 == 0 and rank % 8 == 0
    lead = 3 * fox_w // tn
    kt, vt = gla_kw // tn, gla_vw // tn
    proj_outs = (_ProjOut(0, lead, 1.0, BF16),
                 _ProjOut(lead, kt, gla_dk ** -0.5, BF16),
                 _ProjOut(lead + kt, kt, 1.0, F32),
                 _ProjOut(lead + 2 * kt, vt, 1.0, BF16),
                 _ProjOut(lead + 2 * kt + vt, vt, 1.0, F32))
    proj_rows = (lead + 2 * kt + 2 * vt) * tn
    fl_rows = (3 * fox_w, fox_heads)
    gd_rows = (w_in.shape[-1] - rank, rank)

    row = lambda v: v.reshape(1, -1).astype(F32)
    h = x.reshape(n, d)
    tm_ffn = _tile(n, 1024)
    tf = _tile(ffn1_w_gate.shape[-1], 512)
    tm_proj = _tile(n, 1024)
    tm_merge = _tile(n, 512)
    tm_ple = _tile(n, 512)
    t_fox = _tile(seq, 512)
    fox_hp = 4
    tb_gla = _tile(seq, 512)

    tf_first = _tile(ffn1_w_gate.shape[-1], 256)
    ffn_rest_grid = (n // tm_ffn - 1, ffn1_w_gate.shape[-1] // tf)
    fox_grid = _fox_grid(batch, seq, fox_heads, t_fox, fox_hp)
    gla_grid = (batch, seq // tb_gla)

    for i in range(depth):
        fbias = jnp.zeros((1, LANES), F32).at[0, :fox_heads].set(fox_forget_bias[i])
        wgu = jnp.zeros((LANES, gla_kw), F32).at[fox_heads:fox_heads + rank].set(gla_gate_up[i]).astype(BF16)

        w_in_t = jnp.swapaxes(w_in, 1, 2)

        h1, w1g, w1u, w1d = _ffn_first(h, row(ffn1_norm[i]), ffn1_w_gate[i], ffn1_w_up[i],
                                       ffn1_w_down[i], tm=tm_ffn, tf=tf_first)
        if ffn_rest_grid[0]:
            h, w_proj = _ffn(h, row(ffn1_norm[i]), w1g, w1u, w1d, tm=tm_ffn, tf=tf, first_tile=1,
                             into=h1, casts=(_cast_compact_rows(
                                 w_in_t, i, ffn_rest_grid, 3 * fox_w, fox_heads, proj_rows),))
        else:
            h, w_proj = h1, jnp.concatenate(
                [w_in_t[i, :3 * fox_w], w_in_t[i, 3 * fox_w + fox_heads:][:proj_rows - 3 * fox_w]]
            ).astype(BF16)

        tiled = lambda a: a.reshape(a.shape[0], batch, seq, a.shape[-1])
        fox_qkv, gq, gk, gv, gr, small = _proj(
            h, row(mix_norm[i]), w_proj, w_in_t, i, tm=tm_proj, tn=tn, outs=proj_outs,
            fl_rows=fl_rows, gd_rows=gd_rows)

        fcum = _fcum(small, fbias, batch=batch, seq=seq)
        y_fox, w2g, w2u, wpg, wpe = _fox(
            tiled(fox_qkv), fcum, batch=batch, seq=seq, n_heads=fox_heads,
            head_dim=fox_dim, t=t_fox, hp=fox_hp,
            casts=tuple(_cast_row_blocks(w, fox_grid) for w in (
                ffn2_w_gate[i], ffn2_w_up[i], w_ple_gate[i], w_ple_proj[i])))

        y_gla, wmg, wbf, wbg, wo, w2d = _gla(
            tiled(gq), tiled(gk), tiled(gv), tiled(gr), small.reshape(batch, seq, LANES), wgu,
            row(gla_gate_bias[i]),
            row(gla_head_norm[i]), batch=batch, seq=seq, n_heads=gla_heads, dk=gla_dk, dv=gla_dv,
            tb=tb_gla,
            casts=tuple(_cast_row_blocks(w, gla_grid) for w in (
                w_merge_gate[i], w_branch_fox[i], w_branch_gla[i], w_out[i], ffn2_w_down[i])))

        h, = _merge(h, row(mix_norm[i]), y_fox.reshape(n, fox_w), y_gla.reshape(n, gla_vw),
                    wmg, row(b_merge_gate[i]), wbf, wbg, wo, tm=tm_merge, tn=tn)

        h, = _ffn(h, row(ffn2_norm[i]), w2g, w2u, w2d, tm=tm_ffn, tf=tf)

        h = _ple(h, p[i].reshape(n, -1), row(ple_norm[i]), row(final_norm), wpg, wpe,
                 tm=tm_ple, final=(i == depth - 1))

    return h.reshape(batch, seq, d)
```

```python
import functools
import math
from typing import Callable, NamedTuple

import jax
import jax.numpy as jnp
from jax import lax
from jax.experimental import pallas as pl
from jax.experimental.pallas import tpu as pltpu

EPS = 1e-6
CHUNK = 64
GLA_GATE_TAU = 16.0
LANES = 128
V7X_VMEM_LIMIT_BYTES = 62 * 1024 * 1024
LOG2E = math.log2(math.e)

F32 = jnp.float32
BF16 = jnp.bfloat16


def _rms(x, g):
    return x * lax.rsqrt(jnp.mean(x * x, axis=-1, keepdims=True) + EPS) * g


def _log_sigmoid(z):
    return jnp.minimum(z, 0.0) - jnp.log1p(jnp.exp(-jnp.abs(z)))


def _dot(a, b):
    return lax.dot_general(a, b, (((1,), (0,)), ((), ())), preferred_element_type=F32)


def _params(*sem):
    return pltpu.CompilerParams(dimension_semantics=sem,
                                vmem_limit_bytes=V7X_VMEM_LIMIT_BYTES)


class _Cast(NamedTuple):
    src: jax.Array
    src_spec: pl.BlockSpec
    dst_spec: pl.BlockSpec
    dst_shape: tuple
    live: Callable


def _flat_step(grid):
    def step(*ids):
        s = ids[0]
        for size, idx in zip(grid[1:], ids[1:]):
            s = s * size + idx
        return s
    return step


def _row_block_count(rows, n_steps, extra=lambda rb: True):
    ok = lambda k: rows % k == 0 and (rows // k) % 16 == 0 and extra(rows // k)
    return max(k for k in range(1, n_steps + 1) if ok(k))


def _cast_row_blocks(src, grid):
    r, c = src.shape
    step = _flat_step(grid)
    nb = _row_block_count(r, math.prod(grid))
    spec = pl.BlockSpec((r // nb, c), lambda *ids: (jnp.minimum(step(*ids), nb - 1), 0))
    return _Cast(src, spec, spec, src.shape, lambda *ids: step(*ids) < nb)


def _cast_compact_rows(src, layer, grid, lead_rows, skip, total_rows):
    d = src.shape[2]
    step = _flat_step(grid)
    nb = _row_block_count(total_rows, math.prod(grid), lambda rb: lead_rows % rb == 0)
    rb = total_rows // nb
    blk = lambda *ids: jnp.minimum(step(*ids), nb - 1)
    src_row = lambda *ids: pl.multiple_of(
        blk(*ids) * rb + jnp.where(blk(*ids) * rb < lead_rows, 0, skip), 8)
    return _Cast(src,
                 pl.BlockSpec((None, pl.Element(rb), pl.Element(d)),
                              lambda *ids: (layer, src_row(*ids), 0)),
                 pl.BlockSpec((rb, d), lambda *ids: (blk(*ids), 0)),
                 (total_rows, d),
                 lambda *ids: step(*ids) < nb)


def _cast_specs(casts):
    shapes = [jax.ShapeDtypeStruct(c.dst_shape, BF16) for c in casts]
    return [c.src_spec for c in casts], [c.dst_spec for c in casts], shapes


def _run_casts(lives, src_refs, dst_refs, n_axes):
    ids = [pl.program_id(a) for a in range(n_axes)]
    for live, src, dst in zip(lives, src_refs, dst_refs):
        @pl.when(live(*ids))
        def _():
            dst[...] = src[...].astype(BF16)


def _ffn_step(x_ref, g_ref, o_ref, xn_ref, wg, wu, wd):
    @pl.when(pl.program_id(1) == 0)
    def _():
        x = x_ref[...]
        xn_ref[...] = _rms(x, g_ref[...]).astype(BF16)
        o_ref[...] = x

    xn = xn_ref[...]
    gate = _dot(xn, wg())
    up = _dot(xn, wu())
    hid = (0.5 * (gate * jax.nn.sigmoid(gate)) * up).astype(BF16)
    o_ref[...] += _dot(hid, wd())


def _ffn_kernel(*refs, lives):
    nc = len(lives)
    x_ref, g_ref, wg_ref, wu_ref, wd_ref = refs[:5]
    o_ref, xn_ref = refs[5 + nc], refs[-1]
    _ffn_step(x_ref, g_ref, o_ref, xn_ref,
              lambda: wg_ref[...], lambda: wu_ref[...], lambda: wd_ref[...])
    _run_casts(lives, refs[5:5 + nc], refs[6 + nc:6 + 2 * nc], 2)


def _ffn(x, g, wg, wu, wd, *, tm, tf, casts=(), first_tile=0):
    n, d = x.shape
    f = wg.shape[1]
    cast_in_specs, cast_out_specs, cast_shapes = _cast_specs(casts)
    return pl.pallas_call(
        functools.partial(_ffn_kernel, lives=tuple(c.live for c in casts)),
        grid=(n // tm - first_tile, f // tf),
        in_specs=[
            pl.BlockSpec((tm, d), lambda i, j: (i + first_tile, 0)),
            pl.BlockSpec((1, d), lambda i, j: (0, 0)),
            pl.BlockSpec((d, tf), lambda i, j: (0, j)),
            pl.BlockSpec((d, tf), lambda i, j: (0, j)),
            pl.BlockSpec((tf, d), lambda i, j: (j, 0)),
        ] + cast_in_specs,
        out_specs=[pl.BlockSpec((tm, d), lambda i, j: (i, 0))] + cast_out_specs,
        out_shape=[jax.ShapeDtypeStruct((n - first_tile * tm, d), F32)] + cast_shapes,
        scratch_shapes=[pltpu.VMEM((tm, d), BF16)],
        compiler_params=_params("parallel", "arbitrary"),
        name="ffn",
    )(x, g, wg, wu, wd, *[c.src for c in casts])


def _ffn_first_kernel(x_ref, g_ref, wg_ref, wu_ref, wd_ref, o_ref, wgb_ref, wub_ref, wdb_ref,
                      xn_ref):
    def cast(src, dst):
        def thunk():
            w = src[...].astype(BF16)
            dst[...] = w
            return w
        return thunk

    _ffn_step(x_ref, g_ref, o_ref, xn_ref, cast(wg_ref, wgb_ref), cast(wu_ref, wub_ref),
              cast(wd_ref, wdb_ref))


def _ffn_first(x, g, wg, wu, wd, *, tm, tf):
    n, d = x.shape
    f = wg.shape[1]
    w_specs = [pl.BlockSpec((d, tf), lambda i, j: (0, j)), pl.BlockSpec((d, tf), lambda i, j: (0, j)),
               pl.BlockSpec((tf, d), lambda i, j: (j, 0))]
    return pl.pallas_call(
        _ffn_first_kernel,
        grid=(1, f // tf),
        in_specs=[pl.BlockSpec((tm, d), lambda i, j: (0, 0)),
                  pl.BlockSpec((1, d), lambda i, j: (0, 0))] + w_specs,
        out_specs=[pl.BlockSpec((tm, d), lambda i, j: (0, 0))] + w_specs,
        out_shape=[jax.ShapeDtypeStruct((tm, d), F32)]
        + [jax.ShapeDtypeStruct(w.shape, BF16) for w in (wg, wu, wd)],
        scratch_shapes=[pltpu.VMEM((tm, d), BF16)],
        compiler_params=_params("arbitrary", "arbitrary"),
        name="ffn_first",
    )(x, g, wg, wu, wd)


class _ProjOut(NamedTuple):
    first: int
    count: int
    scale: float
    dtype: type


def _dot_nt(a, b):
    return lax.dot_general(a, b, (((1,), (1,)), ((), ())), preferred_element_type=F32)


def _row_parts_specs(parts, tm, **kw):
    d = parts[0].shape[1]
    nf = parts[0].shape[0] // tm
    specs = [pl.BlockSpec((tm, d), lambda i, j: (jnp.minimum(i, nf - 1), 0), **kw)]
    if len(parts) == 2:
        specs.append(pl.BlockSpec((tm, d), lambda i, j: (jnp.maximum(i - nf, 0), 0)))
    return specs, nf


def _for_row_part(part_refs, nf, fn):
    if len(part_refs) == 1:
        return fn(part_refs[0])
    i = pl.program_id(0)
    pl.when(i < nf)(lambda: fn(part_refs[0]))
    pl.when(i >= nf)(lambda: fn(part_refs[1]))


def _proj_kernel(*refs, lives, outs, n_fl, rank, n_parts, nf):
    nc = len(lives)
    h_refs = refs[:n_parts]
    g_ref, w_ref, wfl_ref, wgd_ref = refs[n_parts:n_parts + 4]
    n_in = n_parts + 4
    out_refs = refs[n_in + nc:n_in + nc + len(outs)]
    small_ref = refs[n_in + nc + len(outs)]
    un_ref = refs[-1]
    j = pl.program_id(1)
    _run_casts(lives, refs[n_in:n_in + nc], refs[n_in + 1 + nc + len(outs):-1], 2)

    def prologue(h_ref):
        un = _rms(h_ref[...], g_ref[...]).astype(BF16)
        un_ref[...] = un
        small_ref[...] = jnp.zeros(small_ref.shape, F32)
        small_ref[:, :n_fl] = _dot_nt(un, wfl_ref[...])
        small_ref[:, n_fl:n_fl + rank] = _dot_nt(un, wgd_ref[...])

    @pl.when(j == 0)
    def _():
        _for_row_part(h_refs, nf, prologue)

    for o_ref, o in zip(out_refs, outs):
        @pl.when((j >= o.first) & (j < o.first + o.count))
        def _():
            tile = _dot_nt(un_ref[...], w_ref[...])
            if o.scale != 1.0:
                tile = tile * o.scale
            o_ref[...] = tile.astype(o_ref.dtype)


def _proj(h_parts, g, w_main, w_in_t, layer, *, tm, tn, outs, fl_rows, gd_rows, casts=()):
    n, d = sum(a.shape[0] for a in h_parts), h_parts[0].shape[1]
    n_steps = sum(o.count for o in outs)
    assert fl_rows[1] + gd_rows[1] <= LANES
    cast_in_specs, cast_out_specs, cast_shapes = _cast_specs(casts)
    single = len(h_parts) == 2 and h_parts[0].shape[0] == tm
    h_specs, nf = _row_parts_specs(
        h_parts, tm, **({"pipeline_mode": pl.Buffered(1)} if single else {}))
    kern = functools.partial(_proj_kernel, lives=tuple(c.live for c in casts), outs=outs,
                             n_fl=fl_rows[1], rank=gd_rows[1], n_parts=len(h_parts), nf=nf)
    rows = lambda start, size: pl.BlockSpec(
        (None, pl.Element(size), pl.Element(d)), lambda i, j: (layer, start, 0))

    def out_spec(o):
        return pl.BlockSpec((None, tm, tn),
                            lambda i, j: (jnp.clip(j - o.first, 0, o.count - 1), i, 0))

    return pl.pallas_call(
        kern,
        grid=(n // tm, n_steps),
        in_specs=h_specs + [
            pl.BlockSpec((1, d), lambda i, j: (0, 0)),
            pl.BlockSpec((tn, d), lambda i, j: (j, 0)),
            rows(*fl_rows),
            rows(*gd_rows),
        ] + cast_in_specs,
        out_specs=[out_spec(o) for o in outs]
        + [pl.BlockSpec((tm, LANES), lambda i, j: (i, 0))] + cast_out_specs,
        out_shape=[jax.ShapeDtypeStruct((o.count, n, tn), o.dtype) for o in outs]
        + [jax.ShapeDtypeStruct((n, LANES), F32)] + cast_shapes,
        scratch_shapes=[pltpu.VMEM((tm, d), BF16)],
        compiler_params=_params("parallel", "arbitrary"),
        name="proj",
    )(*h_parts, g, w_main, w_in_t, w_in_t, *[c.src for c in casts])


def _fcum_kernel(small_ref, bias_ref, f_ref):
    s = small_ref.shape[0]
    f_ref[0] = _log_sigmoid(small_ref[...] + bias_ref[...])
    row = lax.broadcasted_iota(jnp.int32, (LANES, LANES), 0)
    col = lax.broadcasted_iota(jnp.int32, (LANES, LANES), 1)
    tril = (row >= col).astype(F32)

    def body(r, carry):
        rows = pl.ds(pl.multiple_of(r * LANES, LANES), LANES)
        c = jnp.dot(tril, f_ref[0, rows, :], precision=lax.Precision.HIGHEST,
                    preferred_element_type=F32) + carry
        f_ref[0, rows, :] = c
        return c[LANES - 1:LANES, :]

    lax.fori_loop(0, s // LANES, body, jnp.zeros((1, LANES), F32))


def _fcum(small, bias_row, *, batch, seq):
    return pl.pallas_call(
        _fcum_kernel,
        grid=(batch,),
        in_specs=[
            pl.BlockSpec((seq, LANES), lambda b: (b, 0)),
            pl.BlockSpec((1, LANES), lambda b: (0, 0)),
        ],
        out_specs=pl.BlockSpec((1, seq, LANES), lambda b: (b, 0, 0)),
        out_shape=jax.ShapeDtypeStruct((batch, seq, LANES), F32),
        compiler_params=_params("parallel"),
        name="fcum",
    )(small, bias_row)


def _fox_kernel(*refs, lives, t, c2, hp, dh):
    nc = len(lives)
    q_ref, k_ref, v_ref, f_ref = refs[:4]
    o_ref = refs[4 + nc]
    (frep_ref, vt_ref, m_ref, l_ref, acc_ref, s0_ref, s1_ref, mc0_ref, mc1_ref) = refs[5 + 2 * nc:]
    _run_casts(lives, refs[4:4 + nc], refs[5 + nc:5 + 2 * nc], 3)
    g = pl.program_id(1)
    i = pl.program_id(2)
    s_refs = (s0_ref, s1_ref)
    mc_refs = (mc0_ref, mc1_ref)

    @pl.when(i == 0)
    def _():
        row = lax.broadcasted_iota(jnp.int32, (LANES, LANES), 0)
        for h in range(hp):
            onehot = (row == g * hp + h).astype(F32)
            frep_ref[h] = jnp.dot(f_ref[0], onehot, precision=lax.Precision.HIGHEST,
                                  preferred_element_type=F32)
            vt_ref[h] = v_ref[0, :, h * dh:(h + 1) * dh].astype(F32).T.astype(BF16)

    q0 = pl.multiple_of(i * t, t)
    m_ref[...] = jnp.full(m_ref.shape, -jnp.inf, F32)
    l_ref[...] = jnp.zeros(l_ref.shape, F32)
    acc_ref[...] = jnp.zeros(acc_ref.shape, F32)
    reps = t // LANES

    def scores(kk, slot, masked):
        k0 = pl.multiple_of(kk * t, t)
        for h in range(hp):
            hs = slice(h * dh, (h + 1) * dh)
            f_base = frep_ref[h, pl.ds(q0, 8), :][0:1, :]
            bias = (f_base - frep_ref[h, pl.ds(k0, t), :]) * LOG2E
            s = lax.dot_general(k_ref[0, pl.ds(k0, t), hs], q_ref[0, :, hs],
                                (((1,), (1,)), ((), ())), preferred_element_type=F32)
            s = s * c2 + jnp.tile(bias, (1, reps))
            if masked:
                key = lax.broadcasted_iota(jnp.int32, (t, t), 0)
                qry = lax.broadcasted_iota(jnp.int32, (t, t), 1)
                s = jnp.where(key <= qry, s, -jnp.inf)
            s_refs[slot][h] = s
            mc_refs[slot][h] = jnp.max(s, axis=0, keepdims=True)

    def absorb(kk, slot):
        k0 = pl.multiple_of(kk * t, t)
        for h in range(hp):
            m_old = m_ref[h]
            m_new = jnp.maximum(m_old, mc_refs[slot][h])
            alpha = jnp.exp2(m_old - m_new)
            p = jnp.exp2(s_refs[slot][h] - m_new)
            l_ref[h] = alpha * l_ref[h] + jnp.sum(p, axis=0, keepdims=True)
            acc_ref[h] = alpha * acc_ref[h] + _dot(vt_ref[h, :, pl.ds(k0, t)], p.astype(BF16))
            m_ref[h] = m_new

    scores(i, 0, True)

    def body(kk, carry):
        prev = jnp.where(kk == 0, i, kk - 1)
        for par in range(2):
            @pl.when(kk % 2 == par)
            def _():
                absorb(prev, par)
                scores(kk, 1 - par, False)
        return carry

    lax.fori_loop(0, i, body, 0)
    last = jnp.where(i == 0, i, i - 1)
    for par in range(2):
        @pl.when(i % 2 == par)
        def _():
            absorb(last, par)
    for h in range(hp):
        o_ref[0, :, h * dh:(h + 1) * dh] = (acc_ref[h] / l_ref[h]).T.astype(o_ref.dtype)


def _fox_grid(batch, seq, n_heads, t, hp):
    return (batch, n_heads // hp, seq // t)


def _fox(qkv, fcum, *, batch, seq, n_heads, head_dim, t, hp, casts=()):
    assert head_dim == LANES and n_heads % hp == 0
    c2 = (head_dim ** -0.5) * LOG2E
    ng = n_heads // hp
    w = hp * head_dim
    assert qkv.shape == (3 * ng, batch, seq, w)
    cast_in_specs, cast_out_specs, cast_shapes = _cast_specs(casts)
    return pl.pallas_call(
        functools.partial(_fox_kernel, lives=tuple(c.live for c in casts), t=t, c2=c2, hp=hp,
                          dh=head_dim),
        grid=_fox_grid(batch, seq, n_heads, t, hp),
        in_specs=[
            pl.BlockSpec((None, 1, t, w), lambda b, g, i: (g, b, i, 0)),
            pl.BlockSpec((None, 1, seq, w), lambda b, g, i: (ng + g, b, 0, 0)),
            pl.BlockSpec((None, 1, seq, w), lambda b, g, i: (2 * ng + g, b, 0, 0),
                         pipeline_mode=pl.Buffered(1)),
            pl.BlockSpec((1, seq, LANES), lambda b, g, i: (b, 0, 0), pipeline_mode=pl.Buffered(1)),
        ] + cast_in_specs,
        out_specs=[pl.BlockSpec((1, t, w), lambda b, g, i: (b, i, g))] + cast_out_specs,
        out_shape=[jax.ShapeDtypeStruct((batch, seq, n_heads * head_dim), BF16)] + cast_shapes,
        scratch_shapes=[pltpu.VMEM((hp, seq, LANES), F32), pltpu.VMEM((hp, head_dim, seq), BF16),
                        pltpu.VMEM((hp, 1, t), F32), pltpu.VMEM((hp, 1, t), F32),
                        pltpu.VMEM((hp, head_dim, t), F32),
                        pltpu.VMEM((hp, t, t), F32), pltpu.VMEM((hp, t, t), F32),
                        pltpu.VMEM((hp, 1, t), F32), pltpu.VMEM((hp, 1, t), F32)],
        compiler_params=_params("parallel", "arbitrary", "arbitrary"),
        name="fox",
    )(qkv, qkv, qkv, fcum, *[c.src for c in casts])


def _gla_kernel(*refs, lives, n_heads, dk, dv):
    nc = len(lives)
    q_ref, k_ref, v_ref, r_ref, small_ref, wgu_ref, gb_ref, gain_ref = refs[:8]
    o_ref = refs[8 + nc]
    st_ref, la_ref, kd_ref, ea_ref = refs[9 + 2 * nc:]
    _run_casts(lives, refs[8:8 + nc], refs[9 + nc:9 + 2 * nc], 1)
    nb, tb, tn = q_ref.shape[1:]
    n_chunks = tb // CHUNK

    def cols(ref, b, rows, start, size):
        return ref[start // tn, b, rows, start % tn:start % tn + size]

    @pl.when(pl.program_id(0) == 0)
    def _():
        st_ref[...] = jnp.zeros(st_ref.shape, F32)

    for b in range(nb):
        pre = _dot(small_ref[b].astype(BF16), wgu_ref[...]) + gb_ref[...]
        la_ref[b] = _log_sigmoid(pre) * (1.0 / GLA_GATE_TAU)

    row = lax.broadcasted_iota(jnp.int32, (CHUNK, CHUNK), 0)
    col = lax.broadcasted_iota(jnp.int32, (CHUNK, CHUNK), 1)
    after = (col > row).astype(F32)

    def prep(c, carry):
        rows = pl.ds(pl.multiple_of(c * CHUNK, CHUNK), CHUNK)
        for b in range(nb):
            la = la_ref[b, rows, :]
            rev = jnp.dot(after, la, precision=lax.Precision.HIGHEST, preferred_element_type=F32)
            for tt in range(k_ref.shape[0]):
                ts = slice(tt * tn, (tt + 1) * tn)
                kd_ref[b, rows, ts] = (k_ref[tt, b, rows, :] * jnp.exp(rev[:, ts])).astype(BF16)
            ea_ref[c, b:b + 1, :] = jnp.exp(jnp.sum(la, axis=0, keepdims=True))
        return carry

    lax.fori_loop(0, n_chunks, prep, 0)

    def step(c, carry):
        rows = pl.ds(pl.multiple_of(c * CHUNK, CHUNK), CHUNK)
        ea = ea_ref[c]
        for b in range(nb):
            for h in range(n_heads):
                ks = slice(h * dk, (h + 1) * dk)
                vs = slice(h * dv, (h + 1) * dv)
                kv_t = lax.dot_general(cols(v_ref, b, rows, h * dv, dv), kd_ref[b, rows, ks],
                                       (((0,), (0,)), ((), ())), preferred_element_type=F32)
                st = st_ref[b * n_heads + h] * ea[b:b + 1, ks] + kv_t
                st_ref[b * n_heads + h] = st
                o = lax.dot_general(cols(q_ref, b, rows, h * dk, dk), st.astype(BF16),
                                    (((1,), (1,)), ((), ())), preferred_element_type=F32)
                on = _rms(o, gain_ref[...])
                r = cols(r_ref, b, rows, h * dv, dv)
                o_ref[b, rows, vs] = (on * (r * jax.nn.sigmoid(r))).astype(o_ref.dtype)
        return carry

    lax.fori_loop(0, n_chunks, step, 0)


def _gla(q, k, v, r, small, wgu, gate_bias, gain, *, batch, seq, n_heads, dk, dv, tb, casts=()):
    kw, vw = n_heads * dk, n_heads * dv
    tn = q.shape[-1]
    assert tn % dk == 0 and tn % dv == 0
    tiles = lambda a: pl.BlockSpec((a.shape[0], batch, tb, tn), lambda t: (0, 0, t, 0))
    cast_in_specs, cast_out_specs, cast_shapes = _cast_specs(casts)
    return pl.pallas_call(
        functools.partial(_gla_kernel, lives=tuple(c.live for c in casts), n_heads=n_heads,
                          dk=dk, dv=dv),
        grid=(seq // tb,),
        in_specs=[
            tiles(q), tiles(k), tiles(v), tiles(r),
            pl.BlockSpec((batch, tb, LANES), lambda t: (0, t, 0)),
            pl.BlockSpec((LANES, kw), lambda t: (0, 0)),
            pl.BlockSpec((1, kw), lambda t: (0, 0)),
            pl.BlockSpec((1, dv), lambda t: (0, 0)),
        ] + cast_in_specs,
        out_specs=[pl.BlockSpec((batch, tb, vw), lambda t: (0, t, 0))] + cast_out_specs,
        out_shape=[jax.ShapeDtypeStruct((batch, seq, vw), BF16)] + cast_shapes,
        scratch_shapes=[
            pltpu.VMEM((batch * n_heads, dv, dk), F32),
            pltpu.VMEM((batch, tb, kw), F32),
            pltpu.VMEM((batch, tb, kw), BF16),
            pltpu.VMEM((tb // CHUNK, batch, kw), F32),
        ],
        compiler_params=_params("arbitrary"),
        name="gla",
    )(q, k, v, r, small, wgu, gate_bias, gain, *[c.src for c in casts])


def _merge_kernel(*refs, lives, n_parts, nf):
    nc = len(lives)
    h_refs = refs[:n_parts]
    (g_ref, yf_ref, yg_ref, wgf_ref, wgg_ref, bgf_ref, bgg_ref,
     wa_ref, wb_ref, wo_ref) = refs[n_parts:n_parts + 10]
    n_in = n_parts + 10
    o_ref, un_ref = refs[n_in + nc], refs[-1]
    _run_casts(lives, refs[n_in:n_in + nc], refs[n_in + 1 + nc:-1], 2)

    def prologue(h_ref):
        h = h_ref[...]
        un_ref[...] = _rms(h, g_ref[...]).astype(BF16)
        o_ref[...] = h

    @pl.when(pl.program_id(1) == 0)
    def _():
        _for_row_part(h_refs, nf, prologue)

    un = un_ref[...]
    g_fox = jax.nn.sigmoid(_dot(un, wgf_ref[...]) + bgf_ref[...])
    g_gla = jax.nn.sigmoid(_dot(un, wgg_ref[...]) + bgg_ref[...])
    merged = g_fox * _dot(yf_ref[...], wa_ref[...]) + g_gla * _dot(yg_ref[...], wb_ref[...])
    o_ref[...] += _dot(merged.astype(BF16), wo_ref[...])


def _merge(h_parts, g, y_fox, y_gla, w_gate, b_gate, w_a, w_b, w_o, *, tm, tn, casts=()):
    n, d = sum(a.shape[0] for a in h_parts), h_parts[0].shape[1]
    nj = d // tn
    cast_in_specs, cast_out_specs, cast_shapes = _cast_specs(casts)
    h_specs, nf = _row_parts_specs(h_parts, tm)
    return pl.pallas_call(
        functools.partial(_merge_kernel, lives=tuple(c.live for c in casts),
                          n_parts=len(h_parts), nf=nf),
        grid=(n // tm, nj),
        in_specs=h_specs + [
            pl.BlockSpec((1, d), lambda i, j: (0, 0)),
            pl.BlockSpec((tm, y_fox.shape[1]), lambda i, j: (i, 0)),
            pl.BlockSpec((tm, y_gla.shape[1]), lambda i, j: (i, 0)),
            pl.BlockSpec((d, tn), lambda i, j: (0, j)),
            pl.BlockSpec((d, tn), lambda i, j: (0, nj + j)),
            pl.BlockSpec((1, tn), lambda i, j: (0, j)),
            pl.BlockSpec((1, tn), lambda i, j: (0, nj + j)),
            pl.BlockSpec((w_a.shape[0], tn), lambda i, j: (0, j)),
            pl.BlockSpec((w_b.shape[0], tn), lambda i, j: (0, j)),
            pl.BlockSpec((tn, d), lambda i, j: (j, 0)),
        ] + cast_in_specs,
        out_specs=[pl.BlockSpec((tm, d), lambda i, j: (i, 0))] + cast_out_specs,
        out_shape=[jax.ShapeDtypeStruct((n, d), F32)] + cast_shapes,
        scratch_shapes=[pltpu.VMEM((tm, d), BF16)],
        compiler_params=_params("parallel", "arbitrary"),
        name="merge",
    )(*h_parts, g, y_fox, y_gla, w_gate, w_gate, b_gate, b_gate, w_a, w_b, w_o,
      *[c.src for c in casts])


def _ple_kernel(h_ref, p_ref, gp_ref, gf_ref, wpg_ref, wpe_ref, o_ref, *, final):
    h = h_ref[...]
    hn = _rms(h, gp_ref[...]).astype(BF16)
    gate = jax.nn.sigmoid(_dot(hn, wpg_ref[...]))
    out = h + gate * _dot(p_ref[...].astype(BF16), wpe_ref[...])
    if final:
        out = _rms(out, gf_ref[...])
    o_ref[...] = out


def _ple(h, p, g_ple, g_final, w_gate, w_proj, *, tm, final):
    n, d = h.shape
    dp = p.shape[1]
    return pl.pallas_call(
        functools.partial(_ple_kernel, final=final),
        grid=(n // tm,),
        in_specs=[
            pl.BlockSpec((tm, d), lambda i: (i, 0)),
            pl.BlockSpec((tm, dp), lambda i: (i, 0)),
            pl.BlockSpec((1, d), lambda i: (0, 0)),
            pl.BlockSpec((1, d), lambda i: (0, 0)),
            pl.BlockSpec((d, d), lambda i: (0, 0)),
            pl.BlockSpec((dp, d), lambda i: (0, 0)),
        ],
        out_specs=pl.BlockSpec((tm, d), lambda i: (i, 0)),
        out_shape=jax.ShapeDtypeStruct((n, d), F32),
        compiler_params=_params("parallel"),
        name="ple",
    )(h, p, g_ple, g_final, w_gate, w_proj)


def _tile(n, want):
    t = min(n, want)
    assert n % t == 0, (n, want)
    return t


def kernel(x, p, ffn1_norm, ffn1_w_gate, ffn1_w_up, ffn1_w_down, mix_norm, w_in, fox_forget_bias, gla_gate_up, gla_gate_bias, gla_head_norm, w_branch_fox, w_branch_gla, w_merge_gate, b_merge_gate, w_out, ffn2_norm, ffn2_w_gate, ffn2_w_up, ffn2_w_down, ple_norm, w_ple_proj, w_ple_gate, final_norm):
    batch, seq, d = x.shape
    depth = p.shape[0]
    n = batch * seq
    fox_heads = fox_forget_bias.shape[-1]
    fox_dim = w_branch_fox.shape[1] // fox_heads
    fox_w = fox_heads * fox_dim
    rank, gla_kw = gla_gate_up.shape[1:]
    gla_dv = gla_head_norm.shape[-1]
    gla_vw = w_branch_gla.shape[1]
    gla_heads = gla_vw // gla_dv
    gla_dk = gla_kw // gla_heads
    assert fox_heads + rank <= LANES

    tn = 512
    assert w_in.shape[-1] == 3 * fox_w + fox_heads + 2 * gla_kw + 2 * gla_vw + rank
    assert (3 * fox_w) % tn == 0 and gla_kw % tn == 0 and gla_vw % tn == 0
    assert fox_heads % 8 == 0 and rank % 8 == 0
    lead = 3 * fox_w // tn
    kt, vt = gla_kw // tn, gla_vw // tn
    proj_outs = (_ProjOut(0, lead, 1.0, BF16),
                 _ProjOut(lead, kt, gla_dk ** -0.5, BF16),
                 _ProjOut(lead + kt, kt, 1.0, F32),
                 _ProjOut(lead + 2 * kt, vt, 1.0, BF16),
                 _ProjOut(lead + 2 * kt + vt, vt, 1.0, F32))
    proj_rows = (lead + 2 * kt + 2 * vt) * tn
    fl_rows = (3 * fox_w, fox_heads)
    gd_rows = (w_in.shape[-1] - rank, rank)

    row = lambda v: v.reshape(1, -1).astype(F32)
    h = x.reshape(n, d)
    tm_ffn = _tile(n, 1024)
    tf = _tile(ffn1_w_gate.shape[-1], 512)
    tm_proj = _tile(n, 1024)
    tm_merge = _tile(n, 512)
    tm_ple = _tile(n, 512)
    t_fox = _tile(seq, 512)
    fox_hp = 4
    tb_gla = _tile(seq, 256)

    tf_first = _tile(ffn1_w_gate.shape[-1], 256)
    ffn_rest_grid = (n // tm_ffn - 1, ffn1_w_gate.shape[-1] // tf)
    fox_grid = _fox_grid(batch, seq, fox_heads, t_fox, fox_hp)
    gla_grid = (seq // tb_gla,)

    for i in range(depth):
        fbias = jnp.zeros((1, LANES), F32).at[0, :fox_heads].set(fox_forget_bias[i])
        wgu = jnp.zeros((LANES, gla_kw), F32).at[fox_heads:fox_heads + rank].set(gla_gate_up[i]).astype(BF16)

        w_in_t = jnp.swapaxes(w_in, 1, 2)

        h1, w1g, w1u, w1d = _ffn_first(h, row(ffn1_norm[i]), ffn1_w_gate[i], ffn1_w_up[i],
                                       ffn1_w_down[i], tm=tm_ffn, tf=tf_first)
        if ffn_rest_grid[0]:
            h_rest, w_proj = _ffn(h, row(ffn1_norm[i]), w1g, w1u, w1d, tm=tm_ffn, tf=tf,
                                  first_tile=1, casts=(_cast_compact_rows(
                                      w_in_t, i, ffn_rest_grid, 3 * fox_w, fox_heads, proj_rows),))
            h_parts = (h1, h_rest)
        else:
            h_parts, w_proj = (h1,), jnp.concatenate(
                [w_in_t[i, :3 * fox_w], w_in_t[i, 3 * fox_w + fox_heads:][:proj_rows - 3 * fox_w]]
            ).astype(BF16)

        tiled = lambda a: a.reshape(a.shape[0], batch, seq, a.shape[-1])
        fox_qkv, gq, gk, gv, gr, small = _proj(
            h_parts, row(mix_norm[i]), w_proj, w_in_t, i, tm=tm_proj, tn=tn, outs=proj_outs,
            fl_rows=fl_rows, gd_rows=gd_rows)

        fcum = _fcum(small, fbias, batch=batch, seq=seq)
        y_fox, w2g, w2u, wpg, wpe = _fox(
            tiled(fox_qkv), fcum, batch=batch, seq=seq, n_heads=fox_heads,
            head_dim=fox_dim, t=t_fox, hp=fox_hp,
            casts=tuple(_cast_row_blocks(w, fox_grid) for w in (
                ffn2_w_gate[i], ffn2_w_up[i], w_ple_gate[i], w_ple_proj[i])))

        y_gla, wmg, wbf, wbg, wo, w2d = _gla(
            tiled(gq), tiled(gk), tiled(gv), tiled(gr), small.reshape(batch, seq, LANES), wgu,
            row(gla_gate_bias[i]),
            row(gla_head_norm[i]), batch=batch, seq=seq, n_heads=gla_heads, dk=gla_dk, dv=gla_dv,
            tb=tb_gla,
            casts=tuple(_cast_row_blocks(w, gla_grid) for w in (
                w_merge_gate[i], w_branch_fox[i], w_branch_gla[i], w_out[i], ffn2_w_down[i])))

        h, = _merge(h_parts, row(mix_norm[i]), y_fox.reshape(n, fox_w), y_gla.reshape(n, gla_vw),
                    wmg, row(b_merge_gate[i]), wbf, wbg, wo, tm=tm_merge, tn=tn)

        h, = _ffn(h, row(ffn2_norm[i]), w2g, w2u, w2d, tm=tm_ffn, tf=tf)

        h = _ple(h, p[i].reshape(n, -1), row(ple_norm[i]), row(final_norm), wpg, wpe,
                 tm=tm_ple, final=(i == depth - 1))

    return h.reshape(batch, seq, d)
```

```python
import functools
import math
from typing import Callable, NamedTuple

import jax
import jax.numpy as jnp
from jax import lax
from jax.experimental import pallas as pl
from jax.experimental.pallas import tpu as pltpu

EPS = 1e-6
CHUNK = 64
GLA_GATE_TAU = 16.0
LANES = 128
V7X_VMEM_LIMIT_BYTES = 62 * 1024 * 1024
LOG2E = math.log2(math.e)

F32 = jnp.float32
BF16 = jnp.bfloat16


def _rms(x, g):
    return x * lax.rsqrt(jnp.mean(x * x, axis=-1, keepdims=True) + EPS) * g


def _log_sigmoid(z):
    return jnp.minimum(z, 0.0) - jnp.log1p(jnp.exp(-jnp.abs(z)))


def _dot(a, b):
    return lax.dot_general(a, b, (((1,), (0,)), ((), ())), preferred_element_type=F32)


def _params(*sem):
    return pltpu.CompilerParams(dimension_semantics=sem,
                                vmem_limit_bytes=V7X_VMEM_LIMIT_BYTES)


class _Cast(NamedTuple):
    src: jax.Array
    src_spec: pl.BlockSpec
    dst_spec: pl.BlockSpec
    dst_shape: tuple
    live: Callable
    transpose: bool = False


def _flat_step(grid):
    def step(*ids):
        s = ids[0]
        for size, idx in zip(grid[1:], ids[1:]):
            s = s * size + idx
        return s
    return step


def _row_block_count(rows, n_steps, extra=lambda rb: True):
    ok = lambda k: rows % k == 0 and (rows // k) % 16 == 0 and extra(rows // k)
    return max(k for k in range(1, n_steps + 1) if ok(k))


def _cast_row_blocks(src, grid):
    r, c = src.shape
    step = _flat_step(grid)
    nb = _row_block_count(r, math.prod(grid))
    spec = pl.BlockSpec((r // nb, c), lambda *ids: (jnp.minimum(step(*ids), nb - 1), 0))
    return _Cast(src, spec, spec, src.shape, lambda *ids: step(*ids) < nb)


def _cast_compact_rows_t(src, layer, grid, lead_rows, skip, total_rows):
    d = src.shape[2]
    step = _flat_step(grid)
    nb = _row_block_count(total_rows, math.prod(grid),
                          lambda rb: lead_rows % rb == 0 and rb % LANES == 0)
    rb = total_rows // nb
    blk = lambda *ids: jnp.minimum(step(*ids), nb - 1)
    src_row = lambda *ids: pl.multiple_of(
        blk(*ids) * rb + jnp.where(blk(*ids) * rb < lead_rows, 0, skip), 8)
    return _Cast(src,
                 pl.BlockSpec((None, pl.Element(rb), pl.Element(d)),
                              lambda *ids: (layer, src_row(*ids), 0)),
                 pl.BlockSpec((d, rb), lambda *ids: (0, blk(*ids))),
                 (d, total_rows),
                 lambda *ids: step(*ids) < nb,
                 transpose=True)


def _cast_specs(casts):
    shapes = [jax.ShapeDtypeStruct(c.dst_shape, BF16) for c in casts]
    return [c.src_spec for c in casts], [c.dst_spec for c in casts], shapes


def _run_casts(lives, src_refs, dst_refs, n_axes):
    ids = [pl.program_id(a) for a in range(n_axes)]
    for (live, transpose), src, dst in zip(lives, src_refs, dst_refs):
        @pl.when(live(*ids))
        def _():
            blk = src[...]
            dst[...] = (blk.T if transpose else blk).astype(BF16)


def _ffn_step(x_ref, g_ref, o_ref, xn_ref, wg, wu, wd):
    @pl.when(pl.program_id(1) == 0)
    def _():
        x = x_ref[...]
        xn_ref[...] = _rms(x, g_ref[...]).astype(BF16)
        o_ref[...] = x

    xn = xn_ref[...]
    gate = _dot(xn, wg())
    up = _dot(xn, wu())
    hid = (0.5 * (gate * jax.nn.sigmoid(gate)) * up).astype(BF16)
    o_ref[...] += _dot(hid, wd())


def _ffn_kernel(*refs, lives):
    nc = len(lives)
    x_ref, g_ref, wg_ref, wu_ref, wd_ref = refs[:5]
    o_ref, xn_ref = refs[5 + nc], refs[-1]
    _ffn_step(x_ref, g_ref, o_ref, xn_ref,
              lambda: wg_ref[...], lambda: wu_ref[...], lambda: wd_ref[...])
    _run_casts(lives, refs[5:5 + nc], refs[6 + nc:6 + 2 * nc], 2)


def _ffn(x, g, wg, wu, wd, *, tm, tf, casts=(), first_tile=0):
    n, d = x.shape
    f = wg.shape[1]
    cast_in_specs, cast_out_specs, cast_shapes = _cast_specs(casts)
    return pl.pallas_call(
        functools.partial(_ffn_kernel, lives=tuple((c.live, c.transpose) for c in casts)),
        grid=(n // tm - first_tile, f // tf),
        in_specs=[
            pl.BlockSpec((tm, d), lambda i, j: (i + first_tile, 0)),
            pl.BlockSpec((1, d), lambda i, j: (0, 0)),
            pl.BlockSpec((d, tf), lambda i, j: (0, j)),
            pl.BlockSpec((d, tf), lambda i, j: (0, j)),
            pl.BlockSpec((tf, d), lambda i, j: (j, 0)),
        ] + cast_in_specs,
        out_specs=[pl.BlockSpec((tm, d), lambda i, j: (i, 0))] + cast_out_specs,
        out_shape=[jax.ShapeDtypeStruct((n - first_tile * tm, d), F32)] + cast_shapes,
        scratch_shapes=[pltpu.VMEM((tm, d), BF16)],
        compiler_params=_params("parallel", "arbitrary"),
        name="ffn",
    )(x, g, wg, wu, wd, *[c.src for c in casts])


def _ffn_first_kernel(x_ref, g_ref, wg_ref, wu_ref, wd_ref, o_ref, wgb_ref, wub_ref, wdb_ref,
                      xn_ref):
    def cast(src, dst):
        def thunk():
            w = src[...].astype(BF16)
            dst[...] = w
            return w
        return thunk

    _ffn_step(x_ref, g_ref, o_ref, xn_ref, cast(wg_ref, wgb_ref), cast(wu_ref, wub_ref),
              cast(wd_ref, wdb_ref))


def _ffn_first(x, g, wg, wu, wd, *, tm, tf):
    n, d = x.shape
    f = wg.shape[1]
    w_specs = [pl.BlockSpec((d, tf), lambda i, j: (0, j)), pl.BlockSpec((d, tf), lambda i, j: (0, j)),
               pl.BlockSpec((tf, d), lambda i, j: (j, 0))]
    return pl.pallas_call(
        _ffn_first_kernel,
        grid=(1, f // tf),
        in_specs=[pl.BlockSpec((tm, d), lambda i, j: (0, 0)),
                  pl.BlockSpec((1, d), lambda i, j: (0, 0))] + w_specs,
        out_specs=[pl.BlockSpec((tm, d), lambda i, j: (0, 0))] + w_specs,
        out_shape=[jax.ShapeDtypeStruct((tm, d), F32)]
        + [jax.ShapeDtypeStruct(w.shape, BF16) for w in (wg, wu, wd)],
        scratch_shapes=[pltpu.VMEM((tm, d), BF16)],
        compiler_params=_params("arbitrary", "arbitrary"),
        name="ffn_first",
    )(x, g, wg, wu, wd)


class _ProjOut(NamedTuple):
    first: int
    count: int
    scale: float
    dtype: type


def _dot_nt(a, b):
    return lax.dot_general(a, b, (((1,), (1,)), ((), ())), preferred_element_type=F32)


def _row_parts_specs(parts, tm, **kw):
    d = parts[0].shape[1]
    nf = parts[0].shape[0] // tm
    specs = [pl.BlockSpec((tm, d), lambda i, *_: (jnp.minimum(i, nf - 1), 0), **kw)]
    if len(parts) == 2:
        specs.append(pl.BlockSpec((tm, d), lambda i, *_: (jnp.maximum(i - nf, 0), 0)))
    return specs, nf


def _for_row_part(part_refs, nf, fn):
    if len(part_refs) == 1:
        return fn(part_refs[0])
    i = pl.program_id(0)
    pl.when(i < nf)(lambda: fn(part_refs[0]))
    pl.when(i >= nf)(lambda: fn(part_refs[1]))


def _proj_kernel(*refs, outs, tn, n_fl, rank, n_parts, nf):
    h_refs = refs[:n_parts]
    g_ref, w_ref, wfl_ref, wgd_ref = refs[n_parts:n_parts + 4]
    out_refs = refs[n_parts + 4:n_parts + 4 + len(outs)]
    small_ref = refs[n_parts + 4 + len(outs)]

    def body(h_ref):
        un = _rms(h_ref[...], g_ref[...]).astype(BF16)
        small_ref[...] = jnp.zeros(small_ref.shape, F32)
        small_ref[:, :n_fl] = _dot_nt(un, wfl_ref[...])
        small_ref[:, n_fl:n_fl + rank] = _dot_nt(un, wgd_ref[...])
        for o_ref, o in zip(out_refs, outs):
            for t in range(o.count):
                c0 = (o.first + t) * tn
                tile = _dot(un, w_ref[:, c0:c0 + tn])
                if o.scale != 1.0:
                    tile = tile * o.scale
                o_ref[t] = tile.astype(o_ref.dtype)

    _for_row_part(h_refs, nf, body)


def _proj(h_parts, g, w_main, w_in_t, layer, *, tm, tn, outs, fl_rows, gd_rows):
    n, d = sum(a.shape[0] for a in h_parts), h_parts[0].shape[1]
    assert fl_rows[1] + gd_rows[1] <= LANES
    once = dict(pipeline_mode=pl.Buffered(1))
    h_specs, nf = _row_parts_specs(h_parts, tm, **(once if len(h_parts) == 2 else {}))
    kern = functools.partial(_proj_kernel, outs=outs, tn=tn, n_fl=fl_rows[1], rank=gd_rows[1],
                             n_parts=len(h_parts), nf=nf)
    rows = lambda start, size: pl.BlockSpec(
        (None, pl.Element(size), pl.Element(d)), lambda i: (layer, start, 0), **once)
    return pl.pallas_call(
        kern,
        grid=(n // tm,),
        in_specs=h_specs + [
            pl.BlockSpec((1, d), lambda i: (0, 0)),
            pl.BlockSpec(w_main.shape, lambda i: (0, 0), **once),
            rows(*fl_rows),
            rows(*gd_rows),
        ],
        out_specs=[pl.BlockSpec((o.count, tm, tn), lambda i: (0, i, 0)) for o in outs]
        + [pl.BlockSpec((tm, LANES), lambda i: (i, 0))],
        out_shape=[jax.ShapeDtypeStruct((o.count, n, tn), o.dtype) for o in outs]
        + [jax.ShapeDtypeStruct((n, LANES), F32)],
        compiler_params=_params("parallel"),
        name="proj",
    )(*h_parts, g, w_main, w_in_t, w_in_t)


def _fcum_kernel(small_ref, bias_ref, f_ref):
    s = small_ref.shape[0]
    f_ref[0] = _log_sigmoid(small_ref[...] + bias_ref[...])
    row = lax.broadcasted_iota(jnp.int32, (LANES, LANES), 0)
    col = lax.broadcasted_iota(jnp.int32, (LANES, LANES), 1)
    tril = (row >= col).astype(F32)

    def body(r, carry):
        rows = pl.ds(pl.multiple_of(r * LANES, LANES), LANES)
        c = jnp.dot(tril, f_ref[0, rows, :], precision=lax.Precision.HIGHEST,
                    preferred_element_type=F32) + carry
        f_ref[0, rows, :] = c
        return c[LANES - 1:LANES, :]

    lax.fori_loop(0, s // LANES, body, jnp.zeros((1, LANES), F32))


def _fcum(small, bias_row, *, batch, seq):
    return pl.pallas_call(
        _fcum_kernel,
        grid=(batch,),
        in_specs=[
            pl.BlockSpec((seq, LANES), lambda b: (b, 0)),
            pl.BlockSpec((1, LANES), lambda b: (0, 0)),
        ],
        out_specs=pl.BlockSpec((1, seq, LANES), lambda b: (b, 0, 0)),
        out_shape=jax.ShapeDtypeStruct((batch, seq, LANES), F32),
        compiler_params=_params("parallel"),
        name="fcum",
    )(small, bias_row)


def _fox_kernel(*refs, lives, t, c2, hp, dh):
    nc = len(lives)
    q_ref, k_ref, v_ref, f_ref = refs[:4]
    o_ref = refs[4 + nc]
    (frep_ref, vt_ref, m_ref, l_ref, acc_ref, s0_ref, s1_ref, mc0_ref, mc1_ref) = refs[5 + 2 * nc:]
    _run_casts(lives, refs[4:4 + nc], refs[5 + nc:5 + 2 * nc], 3)
    g = pl.program_id(1)
    i = pl.program_id(2)
    s_refs = (s0_ref, s1_ref)
    mc_refs = (mc0_ref, mc1_ref)

    @pl.when(i == 0)
    def _():
        row = lax.broadcasted_iota(jnp.int32, (LANES, LANES), 0)
        for h in range(hp):
            onehot = (row == g * hp + h).astype(F32)
            frep_ref[h] = jnp.dot(f_ref[0], onehot, precision=lax.Precision.HIGHEST,
                                  preferred_element_type=F32)
            vt_ref[h] = v_ref[0, :, h * dh:(h + 1) * dh].astype(F32).T.astype(BF16)

    q0 = pl.multiple_of(i * t, t)
    m_ref[...] = jnp.full(m_ref.shape, -jnp.inf, F32)
    l_ref[...] = jnp.zeros(l_ref.shape, F32)
    acc_ref[...] = jnp.zeros(acc_ref.shape, F32)
    reps = t // LANES

    def scores(kk, slot, masked):
        k0 = pl.multiple_of(kk * t, t)
        for h in range(hp):
            hs = slice(h * dh, (h + 1) * dh)
            f_base = frep_ref[h, pl.ds(q0, 8), :][0:1, :]
            bias = (f_base - frep_ref[h, pl.ds(k0, t), :]) * LOG2E
            s = lax.dot_general(k_ref[0, pl.ds(k0, t), hs], q_ref[0, :, hs],
                                (((1,), (1,)), ((), ())), preferred_element_type=F32)
            s = s * c2 + jnp.tile(bias, (1, reps))
            if masked:
                key = lax.broadcasted_iota(jnp.int32, (t, t), 0)
                qry = lax.broadcasted_iota(jnp.int32, (t, t), 1)
                s = jnp.where(key <= qry, s, -jnp.inf)
            s_refs[slot][h] = s
            mc_refs[slot][h] = jnp.max(s, axis=0, keepdims=True)

    def absorb(kk, slot):
        k0 = pl.multiple_of(kk * t, t)
        for h in range(hp):
            m_old = m_ref[h]
            m_new = jnp.maximum(m_old, mc_refs[slot][h])
            alpha = jnp.exp2(m_old - m_new)
            p = jnp.exp2(s_refs[slot][h] - m_new)
            l_ref[h] = alpha * l_ref[h] + jnp.sum(p, axis=0, keepdims=True)
            acc_ref[h] = alpha * acc_ref[h] + _dot(vt_ref[h, :, pl.ds(k0, t)], p.astype(BF16))
            m_ref[h] = m_new

    scores(i, 0, True)

    def body(kk, carry):
        prev = jnp.where(kk == 0, i, kk - 1)
        for par in range(2):
            @pl.when(kk % 2 == par)
            def _():
                absorb(prev, par)
                scores(kk, 1 - par, False)
        return carry

    lax.fori_loop(0, i, body, 0)
    last = jnp.where(i == 0, i, i - 1)
    for par in range(2):
        @pl.when(i % 2 == par)
        def _():
            absorb(last, par)
    for h in range(hp):
        o_ref[0, :, h * dh:(h + 1) * dh] = (acc_ref[h] / l_ref[h]).T.astype(o_ref.dtype)


def _fox_grid(batch, seq, n_heads, t, hp):
    return (batch, n_heads // hp, seq // t)


def _fox(qkv, fcum, *, batch, seq, n_heads, head_dim, t, hp, casts=()):
    assert head_dim == LANES and n_heads % hp == 0
    c2 = (head_dim ** -0.5) * LOG2E
    ng = n_heads // hp
    w = hp * head_dim
    assert qkv.shape == (3 * ng, batch, seq, w)
    cast_in_specs, cast_out_specs, cast_shapes = _cast_specs(casts)
    return pl.pallas_call(
        functools.partial(_fox_kernel, lives=tuple((c.live, c.transpose) for c in casts), t=t, c2=c2, hp=hp,
                          dh=head_dim),
        grid=_fox_grid(batch, seq, n_heads, t, hp),
        in_specs=[
            pl.BlockSpec((None, 1, t, w), lambda b, g, i: (g, b, i, 0)),
            pl.BlockSpec((None, 1, seq, w), lambda b, g, i: (ng + g, b, 0, 0)),
            pl.BlockSpec((None, 1, seq, w), lambda b, g, i: (2 * ng + g, b, 0, 0),
                         pipeline_mode=pl.Buffered(1)),
            pl.BlockSpec((1, seq, LANES), lambda b, g, i: (b, 0, 0), pipeline_mode=pl.Buffered(1)),
        ] + cast_in_specs,
        out_specs=[pl.BlockSpec((1, t, w), lambda b, g, i: (b, i, g))] + cast_out_specs,
        out_shape=[jax.ShapeDtypeStruct((batch, seq, n_heads * head_dim), BF16)] + cast_shapes,
        scratch_shapes=[pltpu.VMEM((hp, seq, LANES), F32), pltpu.VMEM((hp, head_dim, seq), BF16),
                        pltpu.VMEM((hp, 1, t), F32), pltpu.VMEM((hp, 1, t), F32),
                        pltpu.VMEM((hp, head_dim, t), F32),
                        pltpu.VMEM((hp, t, t), F32), pltpu.VMEM((hp, t, t), F32),
                        pltpu.VMEM((hp, 1, t), F32), pltpu.VMEM((hp, 1, t), F32)],
        compiler_params=_params("parallel", "arbitrary", "arbitrary"),
        name="fox",
    )(qkv, qkv, qkv, fcum, *[c.src for c in casts])


def _gla_kernel(*refs, lives, n_heads, dk, dv):
    nc = len(lives)
    q_ref, k_ref, v_ref, r_ref, small_ref, wgu_ref, gb_ref, gain_ref = refs[:8]
    o_ref = refs[8 + nc]
    st_ref, la_ref, kd_ref, ea_ref = refs[9 + 2 * nc:]
    _run_casts(lives, refs[8:8 + nc], refs[9 + nc:9 + 2 * nc], 1)
    nb, tb, tn = q_ref.shape[1:]
    n_chunks = tb // CHUNK

    def cols(ref, b, rows, start, size):
        return ref[start // tn, b, rows, start % tn:start % tn + size]

    @pl.when(pl.program_id(0) == 0)
    def _():
        st_ref[...] = jnp.zeros(st_ref.shape, F32)

    for b in range(nb):
        pre = _dot(small_ref[b].astype(BF16), wgu_ref[...]) + gb_ref[...]
        la_ref[b] = _log_sigmoid(pre) * (1.0 / GLA_GATE_TAU)

    row = lax.broadcasted_iota(jnp.int32, (CHUNK, CHUNK), 0)
    col = lax.broadcasted_iota(jnp.int32, (CHUNK, CHUNK), 1)
    after = (col > row).astype(F32)

    def prep(c, carry):
        rows = pl.ds(pl.multiple_of(c * CHUNK, CHUNK), CHUNK)
        for b in range(nb):
            la = la_ref[b, rows, :]
            rev = jnp.dot(after, la, precision=lax.Precision.HIGHEST, preferred_element_type=F32)
            for tt in range(k_ref.shape[0]):
                ts = slice(tt * tn, (tt + 1) * tn)
                kd_ref[b, rows, ts] = (k_ref[tt, b, rows, :] * jnp.exp(rev[:, ts])).astype(BF16)
            ea_ref[c, b:b + 1, :] = jnp.exp(jnp.sum(la, axis=0, keepdims=True))
        return carry

    lax.fori_loop(0, n_chunks, prep, 0)

    def step(c, carry):
        rows = pl.ds(pl.multiple_of(c * CHUNK, CHUNK), CHUNK)
        ea = ea_ref[c]
        for b in range(nb):
            for h in range(n_heads):
                ks = slice(h * dk, (h + 1) * dk)
                vs = slice(h * dv, (h + 1) * dv)
                kv_t = lax.dot_general(cols(v_ref, b, rows, h * dv, dv), kd_ref[b, rows, ks],
                                       (((0,), (0,)), ((), ())), preferred_element_type=F32)
                st = st_ref[b * n_heads + h] * ea[b:b + 1, ks] + kv_t
                st_ref[b * n_heads + h] = st
                o = lax.dot_general(cols(q_ref, b, rows, h * dk, dk), st.astype(BF16),
                                    (((1,), (1,)), ((), ())), preferred_element_type=F32)
                on = _rms(o, gain_ref[...])
                r = cols(r_ref, b, rows, h * dv, dv)
                o_ref[b, rows, vs] = (on * (r * jax.nn.sigmoid(r))).astype(o_ref.dtype)
        return carry

    lax.fori_loop(0, n_chunks, step, 0)


def _gla(q, k, v, r, small, wgu, gate_bias, gain, *, batch, seq, n_heads, dk, dv, tb, casts=()):
    kw, vw = n_heads * dk, n_heads * dv
    tn = q.shape[-1]
    assert tn % dk == 0 and tn % dv == 0
    tiles = lambda a: pl.BlockSpec((a.shape[0], batch, tb, tn), lambda t: (0, 0, t, 0))
    cast_in_specs, cast_out_specs, cast_shapes = _cast_specs(casts)
    return pl.pallas_call(
        functools.partial(_gla_kernel, lives=tuple((c.live, c.transpose) for c in casts), n_heads=n_heads,
                          dk=dk, dv=dv),
        grid=(seq // tb,),
        in_specs=[
            tiles(q), tiles(k), tiles(v), tiles(r),
            pl.BlockSpec((batch, tb, LANES), lambda t: (0, t, 0)),
            pl.BlockSpec((LANES, kw), lambda t: (0, 0)),
            pl.BlockSpec((1, kw), lambda t: (0, 0)),
            pl.BlockSpec((1, dv), lambda t: (0, 0)),
        ] + cast_in_specs,
        out_specs=[pl.BlockSpec((batch, tb, vw), lambda t: (0, t, 0))] + cast_out_specs,
        out_shape=[jax.ShapeDtypeStruct((batch, seq, vw), BF16)] + cast_shapes,
        scratch_shapes=[
            pltpu.VMEM((batch * n_heads, dv, dk), F32),
            pltpu.VMEM((batch, tb, kw), F32),
            pltpu.VMEM((batch, tb, kw), BF16),
            pltpu.VMEM((tb // CHUNK, batch, kw), F32),
        ],
        compiler_params=_params("arbitrary"),
        name="gla",
    )(q, k, v, r, small, wgu, gate_bias, gain, *[c.src for c in casts])


def _merge_kernel(*refs, lives, n_parts, nf):
    nc = len(lives)
    h_refs = refs[:n_parts]
    (g_ref, yf_ref, yg_ref, wgf_ref, wgg_ref, bgf_ref, bgg_ref,
     wa_ref, wb_ref, wo_ref) = refs[n_parts:n_parts + 10]
    n_in = n_parts + 10
    o_ref, un_ref = refs[n_in + nc], refs[-1]
    _run_casts(lives, refs[n_in:n_in + nc], refs[n_in + 1 + nc:-1], 2)

    def prologue(h_ref):
        h = h_ref[...]
        un_ref[...] = _rms(h, g_ref[...]).astype(BF16)
        o_ref[...] = h

    @pl.when(pl.program_id(1) == 0)
    def _():
        _for_row_part(h_refs, nf, prologue)

    un = un_ref[...]
    g_fox = jax.nn.sigmoid(_dot(un, wgf_ref[...]) + bgf_ref[...])
    g_gla = jax.nn.sigmoid(_dot(un, wgg_ref[...]) + bgg_ref[...])
    merged = g_fox * _dot(yf_ref[...], wa_ref[...]) + g_gla * _dot(yg_ref[...], wb_ref[...])
    o_ref[...] += _dot(merged.astype(BF16), wo_ref[...])


def _merge(h_parts, g, y_fox, y_gla, w_gate, b_gate, w_a, w_b, w_o, *, tm, tn, casts=()):
    n, d = sum(a.shape[0] for a in h_parts), h_parts[0].shape[1]
    nj = d // tn
    cast_in_specs, cast_out_specs, cast_shapes = _cast_specs(casts)
    h_specs, nf = _row_parts_specs(h_parts, tm)
    return pl.pallas_call(
        functools.partial(_merge_kernel, lives=tuple((c.live, c.transpose) for c in casts),
                          n_parts=len(h_parts), nf=nf),
        grid=(n // tm, nj),
        in_specs=h_specs + [
            pl.BlockSpec((1, d), lambda i, j: (0, 0)),
            pl.BlockSpec((tm, y_fox.shape[1]), lambda i, j: (i, 0)),
            pl.BlockSpec((tm, y_gla.shape[1]), lambda i, j: (i, 0)),
            pl.BlockSpec((d, tn), lambda i, j: (0, j)),
            pl.BlockSpec((d, tn), lambda i, j: (0, nj + j)),
            pl.BlockSpec((1, tn), lambda i, j: (0, j)),
            pl.BlockSpec((1, tn), lambda i, j: (0, nj + j)),
            pl.BlockSpec((w_a.shape[0], tn), lambda i, j: (0, j)),
            pl.BlockSpec((w_b.shape[0], tn), lambda i, j: (0, j)),
            pl.BlockSpec((tn, d), lambda i, j: (j, 0)),
        ] + cast_in_specs,
        out_specs=[pl.BlockSpec((tm, d), lambda i, j: (i, 0))] + cast_out_specs,
        out_shape=[jax.ShapeDtypeStruct((n, d), F32)] + cast_shapes,
        scratch_shapes=[pltpu.VMEM((tm, d), BF16)],
        compiler_params=_params("parallel", "arbitrary"),
        name="merge",
    )(*h_parts, g, y_fox, y_gla, w_gate, w_gate, b_gate, b_gate, w_a, w_b, w_o,
      *[c.src for c in casts])


def _ple_kernel(h_ref, p_ref, gp_ref, gf_ref, wpg_ref, wpe_ref, o_ref, *, final):
    h = h_ref[...]
    hn = _rms(h, gp_ref[...]).astype(BF16)
    gate = jax.nn.sigmoid(_dot(hn, wpg_ref[...]))
    out = h + gate * _dot(p_ref[...].astype(BF16), wpe_ref[...])
    if final:
        out = _rms(out, gf_ref[...])
    o_ref[...] = out


def _ple(h, p, g_ple, g_final, w_gate, w_proj, *, tm, final):
    n, d = h.shape
    dp = p.shape[1]
    return pl.pallas_call(
        functools.partial(_ple_kernel, final=final),
        grid=(n // tm,),
        in_specs=[
            pl.BlockSpec((tm, d), lambda i: (i, 0)),
            pl.BlockSpec((tm, dp), lambda i: (i, 0)),
            pl.BlockSpec((1, d), lambda i: (0, 0)),
            pl.BlockSpec((1, d), lambda i: (0, 0)),
            pl.BlockSpec((d, d), lambda i: (0, 0)),
            pl.BlockSpec((dp, d), lambda i: (0, 0)),
        ],
        out_specs=pl.BlockSpec((tm, d), lambda i: (i, 0)),
        out_shape=jax.ShapeDtypeStruct((n, d), F32),
        compiler_params=_params("parallel"),
        name="ple",
    )(h, p, g_ple, g_final, w_gate, w_proj)


def _tile(n, want):
    t = min(n, want)
    assert n % t == 0, (n, want)
    return t


def kernel(x, p, ffn1_norm, ffn1_w_gate, ffn1_w_up, ffn1_w_down, mix_norm, w_in, fox_forget_bias, gla_gate_up, gla_gate_bias, gla_head_norm, w_branch_fox, w_branch_gla, w_merge_gate, b_merge_gate, w_out, ffn2_norm, ffn2_w_gate, ffn2_w_up, ffn2_w_down, ple_norm, w_ple_proj, w_ple_gate, final_norm):
    batch, seq, d = x.shape
    depth = p.shape[0]
    n = batch * seq
    fox_heads = fox_forget_bias.shape[-1]
    fox_dim = w_branch_fox.shape[1] // fox_heads
    fox_w = fox_heads * fox_dim
    rank, gla_kw = gla_gate_up.shape[1:]
    gla_dv = gla_head_norm.shape[-1]
    gla_vw = w_branch_gla.shape[1]
    gla_heads = gla_vw // gla_dv
    gla_dk = gla_kw // gla_heads
    assert fox_heads + rank <= LANES

    tn = 512
    assert w_in.shape[-1] == 3 * fox_w + fox_heads + 2 * gla_kw + 2 * gla_vw + rank
    assert (3 * fox_w) % tn == 0 and gla_kw % tn == 0 and gla_vw % tn == 0
    assert fox_heads % 8 == 0 and rank % 8 == 0
    lead = 3 * fox_w // tn
    kt, vt = gla_kw // tn, gla_vw // tn
    proj_outs = (_ProjOut(0, lead, 1.0, BF16),
                 _ProjOut(lead, kt, gla_dk ** -0.5, BF16),
                 _ProjOut(lead + kt, kt, 1.0, F32),
                 _ProjOut(lead + 2 * kt, vt, 1.0, BF16),
                 _ProjOut(lead + 2 * kt + vt, vt, 1.0, F32))
    proj_rows = (lead + 2 * kt + 2 * vt) * tn
    fl_rows = (3 * fox_w, fox_heads)
    gd_rows = (w_in.shape[-1] - rank, rank)

    row = lambda v: v.reshape(1, -1).astype(F32)
    h = x.reshape(n, d)
    tm_ffn = _tile(n, 1024)
    tf = _tile(ffn1_w_gate.shape[-1], 512)
    tm_proj = _tile(n, 512)
    tm_merge = _tile(n, 512)
    tm_ple = _tile(n, 512)
    t_fox = _tile(seq, 512)
    fox_hp = 4
    tb_gla = _tile(seq, 256)

    tf_first = _tile(ffn1_w_gate.shape[-1], 256)
    ffn_rest_grid = (n // tm_ffn - 1, ffn1_w_gate.shape[-1] // tf)
    fox_grid = _fox_grid(batch, seq, fox_heads, t_fox, fox_hp)
    gla_grid = (seq // tb_gla,)

    for i in range(depth):
        fbias = jnp.zeros((1, LANES), F32).at[0, :fox_heads].set(fox_forget_bias[i])
        wgu = jnp.zeros((LANES, gla_kw), F32).at[fox_heads:fox_heads + rank].set(gla_gate_up[i]).astype(BF16)

        w_in_t = jnp.swapaxes(w_in, 1, 2)

        h1, w1g, w1u, w1d = _ffn_first(h, row(ffn1_norm[i]), ffn1_w_gate[i], ffn1_w_up[i],
                                       ffn1_w_down[i], tm=tm_ffn, tf=tf_first)
        if ffn_rest_grid[0]:
            h_rest, w_proj = _ffn(h, row(ffn1_norm[i]), w1g, w1u, w1d, tm=tm_ffn, tf=tf,
                                  first_tile=1, casts=(_cast_compact_rows_t(
                                      w_in_t, i, ffn_rest_grid, 3 * fox_w, fox_heads, proj_rows),))
            h_parts = (h1, h_rest)
        else:
            h_parts, w_proj = (h1,), jnp.concatenate(
                [w_in[i][:, :3 * fox_w], w_in[i][:, 3 * fox_w + fox_heads:][:, :proj_rows - 3 * fox_w]],
                axis=1).astype(BF16)

        tiled = lambda a: a.reshape(a.shape[0], batch, seq, a.shape[-1])
        fox_qkv, gq, gk, gv, gr, small = _proj(
            h_parts, row(mix_norm[i]), w_proj, w_in_t, i, tm=tm_proj, tn=tn, outs=proj_outs,
            fl_rows=fl_rows, gd_rows=gd_rows)

        fcum = _fcum(small, fbias, batch=batch, seq=seq)
        y_fox, w2g, w2u, wpg, wpe = _fox(
            tiled(fox_qkv), fcum, batch=batch, seq=seq, n_heads=fox_heads,
            head_dim=fox_dim, t=t_fox, hp=fox_hp,
            casts=tuple(_cast_row_blocks(w, fox_grid) for w in (
                ffn2_w_gate[i], ffn2_w_up[i], w_ple_gate[i], w_ple_proj[i])))

        y_gla, wmg, wbf, wbg, wo, w2d = _gla(
            tiled(gq), tiled(gk), tiled(gv), tiled(gr), small.reshape(batch, seq, LANES), wgu,
            row(gla_gate_bias[i]),
            row(gla_head_norm[i]), batch=batch, seq=seq, n_heads=gla_heads, dk=gla_dk, dv=gla_dv,
            tb=tb_gla,
            casts=tuple(_cast_row_blocks(w, gla_grid) for w in (
                w_merge_gate[i], w_branch_fox[i], w_branch_gla[i], w_out[i], ffn2_w_down[i])))

        h, = _merge(h_parts, row(mix_norm[i]), y_fox.reshape(n, fox_w), y_gla.reshape(n, gla_vw),
                    wmg, row(b_merge_gate[i]), wbf, wbg, wo, tm=tm_merge, tn=tn)

        h, = _ffn(h, row(ffn2_norm[i]), w2g, w2u, w2d, tm=tm_ffn, tf=tf)

        h = _ple(h, p[i].reshape(n, -1), row(ple_norm[i]), row(final_norm), wpg, wpe,
                 tm=tm_ple, final=(i == depth - 1))

    return h.reshape(batch, seq, d)
```

```python
import functools
import math
from typing import Callable, NamedTuple

import jax
import jax.numpy as jnp
from jax import lax
from jax.experimental import pallas as pl
from jax.experimental.pallas import tpu as pltpu

EPS = 1e-6
CHUNK = 64
GLA_GATE_TAU = 16.0
LANES = 128
V7X_VMEM_LIMIT_BYTES = 62 * 1024 * 1024
LOG2E = math.log2(math.e)

F32 = jnp.float32
BF16 = jnp.bfloat16


def _rms(x, g):
    return x * lax.rsqrt(jnp.mean(x * x, axis=-1, keepdims=True) + EPS) * g


def _log_sigmoid(z):
    return jnp.minimum(z, 0.0) - jnp.log1p(jnp.exp(-jnp.abs(z)))


def _dot(a, b):
    return lax.dot_general(a, b, (((1,), (0,)), ((), ())), preferred_element_type=F32)


def _params(*sem):
    return pltpu.CompilerParams(dimension_semantics=sem,
                                vmem_limit_bytes=V7X_VMEM_LIMIT_BYTES)


class _Cast(NamedTuple):
    src: jax.Array
    src_spec: pl.BlockSpec
    dst_spec: pl.BlockSpec
    dst_shape: tuple
    live: Callable
    transpose: bool = False


def _flat_step(grid):
    def step(*ids):
        s = ids[0]
        for size, idx in zip(grid[1:], ids[1:]):
            s = s * size + idx
        return s
    return step


def _row_block_count(rows, n_steps, extra=lambda rb: True):
    ok = lambda k: rows % k == 0 and (rows // k) % 16 == 0 and extra(rows // k)
    return max(k for k in range(1, n_steps + 1) if ok(k))


def _cast_row_blocks(src, grid):
    r, c = src.shape
    step = _flat_step(grid)
    nb = _row_block_count(r, math.prod(grid))
    spec = pl.BlockSpec((r // nb, c), lambda *ids: (jnp.minimum(step(*ids), nb - 1), 0))
    return _Cast(src, spec, spec, src.shape, lambda *ids: step(*ids) < nb)


def _cast_compact_rows_t(src, layer, grid, lead_rows, skip, total_rows):
    d = src.shape[2]
    step = _flat_step(grid)
    nb = _row_block_count(total_rows, math.prod(grid),
                          lambda rb: lead_rows % rb == 0 and rb % LANES == 0)
    rb = total_rows // nb
    blk = lambda *ids: jnp.minimum(step(*ids), nb - 1)
    src_row = lambda *ids: pl.multiple_of(
        blk(*ids) * rb + jnp.where(blk(*ids) * rb < lead_rows, 0, skip), 8)
    return _Cast(src,
                 pl.BlockSpec((None, pl.Element(rb), pl.Element(d)),
                              lambda *ids: (layer, src_row(*ids), 0)),
                 pl.BlockSpec((d, rb), lambda *ids: (0, blk(*ids))),
                 (d, total_rows),
                 lambda *ids: step(*ids) < nb,
                 transpose=True)


def _cast_specs(casts):
    shapes = [jax.ShapeDtypeStruct(c.dst_shape, BF16) for c in casts]
    return [c.src_spec for c in casts], [c.dst_spec for c in casts], shapes


def _run_casts(lives, src_refs, dst_refs, n_axes):
    ids = [pl.program_id(a) for a in range(n_axes)]
    for (live, transpose), src, dst in zip(lives, src_refs, dst_refs):
        @pl.when(live(*ids))
        def _():
            blk = src[...]
            dst[...] = (blk.T if transpose else blk).astype(BF16)


def _ffn_step(x_ref, g_ref, o_ref, xn_ref, wg, wu, wd):
    @pl.when(pl.program_id(1) == 0)
    def _():
        x = x_ref[...]
        xn_ref[...] = _rms(x, g_ref[...]).astype(BF16)
        o_ref[...] = x

    xn = xn_ref[...]
    gate = _dot(xn, wg())
    up = _dot(xn, wu())
    hid = (0.5 * (gate * jax.nn.sigmoid(gate)) * up).astype(BF16)
    o_ref[...] += _dot(hid, wd())


def _ffn_kernel(*refs, lives):
    nc = len(lives)
    x_ref, g_ref, wg_ref, wu_ref, wd_ref = refs[:5]
    o_ref, xn_ref = refs[5 + nc], refs[-1]
    _ffn_step(x_ref, g_ref, o_ref, xn_ref,
              lambda: wg_ref[...], lambda: wu_ref[...], lambda: wd_ref[...])
    _run_casts(lives, refs[5:5 + nc], refs[6 + nc:6 + 2 * nc], 2)


def _ffn(x, g, wg, wu, wd, *, tm, tf, casts=(), first_tile=0):
    n, d = x.shape
    f = wg.shape[1]
    cast_in_specs, cast_out_specs, cast_shapes = _cast_specs(casts)
    return pl.pallas_call(
        functools.partial(_ffn_kernel, lives=tuple((c.live, c.transpose) for c in casts)),
        grid=(n // tm - first_tile, f // tf),
        in_specs=[
            pl.BlockSpec((tm, d), lambda i, j: (i + first_tile, 0)),
            pl.BlockSpec((1, d), lambda i, j: (0, 0)),
            pl.BlockSpec((d, tf), lambda i, j: (0, j)),
            pl.BlockSpec((d, tf), lambda i, j: (0, j)),
            pl.BlockSpec((tf, d), lambda i, j: (j, 0)),
        ] + cast_in_specs,
        out_specs=[pl.BlockSpec((tm, d), lambda i, j: (i, 0))] + cast_out_specs,
        out_shape=[jax.ShapeDtypeStruct((n - first_tile * tm, d), F32)] + cast_shapes,
        scratch_shapes=[pltpu.VMEM((tm, d), BF16)],
        compiler_params=_params("parallel", "arbitrary"),
        name="ffn",
    )(x, g, wg, wu, wd, *[c.src for c in casts])


def _ffn_first_kernel(x_ref, g_ref, wg_ref, wu_ref, wd_ref, o_ref, wgb_ref, wub_ref, wdb_ref,
                      xn_ref):
    def cast(src, dst):
        def thunk():
            w = src[...].astype(BF16)
            dst[...] = w
            return w
        return thunk

    _ffn_step(x_ref, g_ref, o_ref, xn_ref, cast(wg_ref, wgb_ref), cast(wu_ref, wub_ref),
              cast(wd_ref, wdb_ref))


def _ffn_first(x, g, wg, wu, wd, *, tm, tf):
    n, d = x.shape
    f = wg.shape[1]
    w_specs = [pl.BlockSpec((d, tf), lambda i, j: (0, j)), pl.BlockSpec((d, tf), lambda i, j: (0, j)),
               pl.BlockSpec((tf, d), lambda i, j: (j, 0))]
    return pl.pallas_call(
        _ffn_first_kernel,
        grid=(1, f // tf),
        in_specs=[pl.BlockSpec((tm, d), lambda i, j: (0, 0)),
                  pl.BlockSpec((1, d), lambda i, j: (0, 0))] + w_specs,
        out_specs=[pl.BlockSpec((tm, d), lambda i, j: (0, 0))] + w_specs,
        out_shape=[jax.ShapeDtypeStruct((tm, d), F32)]
        + [jax.ShapeDtypeStruct(w.shape, BF16) for w in (wg, wu, wd)],
        scratch_shapes=[pltpu.VMEM((tm, d), BF16)],
        compiler_params=_params("arbitrary", "arbitrary"),
        name="ffn_first",
    )(x, g, wg, wu, wd)


class _ProjOut(NamedTuple):
    first: int
    count: int
    scale: float
    dtype: type


def _dot_nt(a, b):
    return lax.dot_general(a, b, (((1,), (1,)), ((), ())), preferred_element_type=F32)


def _row_parts_specs(parts, tm, **kw):
    d = parts[0].shape[1]
    nf = parts[0].shape[0] // tm
    specs = [pl.BlockSpec((tm, d), lambda i, *_: (jnp.minimum(i, nf - 1), 0), **kw)]
    if len(parts) == 2:
        specs.append(pl.BlockSpec((tm, d), lambda i, *_: (jnp.maximum(i - nf, 0), 0)))
    return specs, nf


def _for_row_part(part_refs, nf, fn):
    if len(part_refs) == 1:
        return fn(part_refs[0])
    i = pl.program_id(0)
    pl.when(i < nf)(lambda: fn(part_refs[0]))
    pl.when(i >= nf)(lambda: fn(part_refs[1]))


def _proj_kernel(*refs, outs, tn, n_fl, rank, n_parts, nf):
    h_refs = refs[:n_parts]
    g_ref, w_ref, wfl_ref, wgd_ref = refs[n_parts:n_parts + 4]
    out_refs = refs[n_parts + 4:n_parts + 4 + len(outs)]
    small_ref = refs[n_parts + 4 + len(outs)]

    def body(h_ref):
        un = _rms(h_ref[...], g_ref[...]).astype(BF16)
        small_ref[...] = jnp.zeros(small_ref.shape, F32)
        small_ref[:, :n_fl] = _dot_nt(un, wfl_ref[...])
        small_ref[:, n_fl:n_fl + rank] = _dot_nt(un, wgd_ref[...])
        for o_ref, o in zip(out_refs, outs):
            for t in range(o.count):
                c0 = (o.first + t) * tn
                tile = _dot(un, w_ref[:, c0:c0 + tn])
                if o.scale != 1.0:
                    tile = tile * o.scale
                o_ref[t] = tile.astype(o_ref.dtype)

    _for_row_part(h_refs, nf, body)


def _proj(h_parts, g, w_main, w_in_t, layer, *, tm, tn, outs, fl_rows, gd_rows):
    n, d = sum(a.shape[0] for a in h_parts), h_parts[0].shape[1]
    assert fl_rows[1] + gd_rows[1] <= LANES
    once = dict(pipeline_mode=pl.Buffered(1))
    h_specs, nf = _row_parts_specs(h_parts, tm, **(once if len(h_parts) == 2 else {}))
    kern = functools.partial(_proj_kernel, outs=outs, tn=tn, n_fl=fl_rows[1], rank=gd_rows[1],
                             n_parts=len(h_parts), nf=nf)
    rows = lambda start, size: pl.BlockSpec(
        (None, pl.Element(size), pl.Element(d)), lambda i: (layer, start, 0), **once)
    return pl.pallas_call(
        kern,
        grid=(n // tm,),
        in_specs=h_specs + [
            pl.BlockSpec((1, d), lambda i: (0, 0)),
            pl.BlockSpec(w_main.shape, lambda i: (0, 0), **once),
            rows(*fl_rows),
            rows(*gd_rows),
        ],
        out_specs=[pl.BlockSpec((o.count, tm, tn), lambda i: (0, i, 0)) for o in outs]
        + [pl.BlockSpec((tm, LANES), lambda i: (i, 0))],
        out_shape=[jax.ShapeDtypeStruct((o.count, n, tn), o.dtype) for o in outs]
        + [jax.ShapeDtypeStruct((n, LANES), F32)],
        compiler_params=_params("parallel"),
        name="proj",
    )(*h_parts, g, w_main, w_in_t, w_in_t)


def _fcum_kernel(small_ref, bias_ref, f_ref):
    s = small_ref.shape[0]
    f_ref[0] = _log_sigmoid(small_ref[...] + bias_ref[...])
    row = lax.broadcasted_iota(jnp.int32, (LANES, LANES), 0)
    col = lax.broadcasted_iota(jnp.int32, (LANES, LANES), 1)
    tril = (row >= col).astype(F32)

    def body(r, carry):
        rows = pl.ds(pl.multiple_of(r * LANES, LANES), LANES)
        c = jnp.dot(tril, f_ref[0, rows, :], precision=lax.Precision.HIGHEST,
                    preferred_element_type=F32) + carry
        f_ref[0, rows, :] = c
        return c[LANES - 1:LANES, :]

    lax.fori_loop(0, s // LANES, body, jnp.zeros((1, LANES), F32))


def _fcum(small, bias_row, *, batch, seq):
    return pl.pallas_call(
        _fcum_kernel,
        grid=(batch,),
        in_specs=[
            pl.BlockSpec((seq, LANES), lambda b: (b, 0)),
            pl.BlockSpec((1, LANES), lambda b: (0, 0)),
        ],
        out_specs=pl.BlockSpec((1, seq, LANES), lambda b: (b, 0, 0)),
        out_shape=jax.ShapeDtypeStruct((batch, seq, LANES), F32),
        compiler_params=_params("parallel"),
        name="fcum",
    )(small, bias_row)


def _fox_kernel(*refs, lives, t, c2, hp, dh):
    nc = len(lives)
    q_ref, k_ref, v_ref, f_ref = refs[:4]
    o_ref = refs[4 + nc]
    (frep_ref, vt_ref, m_ref, l_ref, acc_ref, s0_ref, s1_ref, mc0_ref, mc1_ref) = refs[5 + 2 * nc:]
    _run_casts(lives, refs[4:4 + nc], refs[5 + nc:5 + 2 * nc], 3)
    g = pl.program_id(1)
    i = pl.program_id(2)
    s_refs = (s0_ref, s1_ref)
    mc_refs = (mc0_ref, mc1_ref)

    @pl.when(i == 0)
    def _():
        row = lax.broadcasted_iota(jnp.int32, (LANES, LANES), 0)
        for h in range(hp):
            onehot = (row == g * hp + h).astype(F32)
            frep_ref[h] = jnp.dot(f_ref[0], onehot, precision=lax.Precision.HIGHEST,
                                  preferred_element_type=F32)
            vt_ref[h] = v_ref[0, :, h * dh:(h + 1) * dh].astype(F32).T.astype(BF16)

    q0 = pl.multiple_of(i * t, t)
    m_ref[...] = jnp.full(m_ref.shape, -jnp.inf, F32)
    l_ref[...] = jnp.zeros(l_ref.shape, F32)
    acc_ref[...] = jnp.zeros(acc_ref.shape, F32)
    reps = t // LANES

    def scores(kk, slot, masked):
        k0 = pl.multiple_of(kk * t, t)
        for h in range(hp):
            hs = slice(h * dh, (h + 1) * dh)
            f_base = frep_ref[h, pl.ds(q0, 8), :][0:1, :]
            bias = (f_base - frep_ref[h, pl.ds(k0, t), :]) * LOG2E
            s = lax.dot_general(k_ref[0, pl.ds(k0, t), hs], q_ref[0, :, hs],
                                (((1,), (1,)), ((), ())), preferred_element_type=F32)
            s = s * c2 + jnp.tile(bias, (1, reps))
            if masked:
                key = lax.broadcasted_iota(jnp.int32, (t, t), 0)
                qry = lax.broadcasted_iota(jnp.int32, (t, t), 1)
                s = jnp.where(key <= qry, s, -jnp.inf)
            s_refs[slot][h] = s
            mc_refs[slot][h] = jnp.max(s, axis=0, keepdims=True)

    def absorb(kk, slot):
        k0 = pl.multiple_of(kk * t, t)
        for h in range(hp):
            m_old = m_ref[h]
            m_new = jnp.maximum(m_old, mc_refs[slot][h])
            alpha = jnp.exp2(m_old - m_new)
            p = jnp.exp2(s_refs[slot][h] - m_new)
            l_ref[h] = alpha * l_ref[h] + jnp.sum(p, axis=0, keepdims=True)
            acc_ref[h] = alpha * acc_ref[h] + _dot(vt_ref[h, :, pl.ds(k0, t)], p.astype(BF16))
            m_ref[h] = m_new

    scores(i, 0, True)

    def body(kk, carry):
        prev = jnp.where(kk == 0, i, kk - 1)
        for par in range(2):
            @pl.when(kk % 2 == par)
            def _():
                absorb(prev, par)
                scores(kk, 1 - par, False)
        return carry

    lax.fori_loop(0, i, body, 0)
    last = jnp.where(i == 0, i, i - 1)
    for par in range(2):
        @pl.when(i % 2 == par)
        def _():
            absorb(last, par)
    for h in range(hp):
        o_ref[0, :, h * dh:(h + 1) * dh] = (acc_ref[h] / l_ref[h]).T.astype(o_ref.dtype)


def _fox_grid(batch, seq, n_heads, t, hp):
    return (batch, n_heads // hp, seq // t)


def _fox(qkv, fcum, *, batch, seq, n_heads, head_dim, t, hp, casts=()):
    assert head_dim == LANES and n_heads % hp == 0
    c2 = (head_dim ** -0.5) * LOG2E
    ng = n_heads // hp
    w = hp * head_dim
    assert qkv.shape == (3 * ng, batch, seq, w)
    cast_in_specs, cast_out_specs, cast_shapes = _cast_specs(casts)
    return pl.pallas_call(
        functools.partial(_fox_kernel, lives=tuple((c.live, c.transpose) for c in casts), t=t, c2=c2, hp=hp,
                          dh=head_dim),
        grid=_fox_grid(batch, seq, n_heads, t, hp),
        in_specs=[
            pl.BlockSpec((None, 1, t, w), lambda b, g, i: (g, b, i, 0)),
            pl.BlockSpec((None, 1, seq, w), lambda b, g, i: (ng + g, b, 0, 0)),
            pl.BlockSpec((None, 1, seq, w), lambda b, g, i: (2 * ng + g, b, 0, 0),
                         pipeline_mode=pl.Buffered(1)),
            pl.BlockSpec((1, seq, LANES), lambda b, g, i: (b, 0, 0), pipeline_mode=pl.Buffered(1)),
        ] + cast_in_specs,
        out_specs=[pl.BlockSpec((1, t, w), lambda b, g, i: (b, i, g))] + cast_out_specs,
        out_shape=[jax.ShapeDtypeStruct((batch, seq, n_heads * head_dim), BF16)] + cast_shapes,
        scratch_shapes=[pltpu.VMEM((hp, seq, LANES), F32), pltpu.VMEM((hp, head_dim, seq), BF16),
                        pltpu.VMEM((hp, 1, t), F32), pltpu.VMEM((hp, 1, t), F32),
                        pltpu.VMEM((hp, head_dim, t), F32),
                        pltpu.VMEM((hp, t, t), F32), pltpu.VMEM((hp, t, t), F32),
                        pltpu.VMEM((hp, 1, t), F32), pltpu.VMEM((hp, 1, t), F32)],
        compiler_params=_params("parallel", "arbitrary", "arbitrary"),
        name="fox",
    )(qkv, qkv, qkv, fcum, *[c.src for c in casts])


def _gla_kernel(*refs, lives, n_heads, dk, dv):
    nc = len(lives)
    q_ref, k_ref, v_ref, r_ref, small_ref, wgu_ref, gb_ref, gain_ref = refs[:8]
    o_ref = refs[8 + nc]
    st_ref, la_ref, kd_ref, ea_ref = refs[9 + 2 * nc:]
    _run_casts(lives, refs[8:8 + nc], refs[9 + nc:9 + 2 * nc], 1)
    nb, tb, tn = q_ref.shape[1:]
    n_chunks = tb // CHUNK

    def cols(ref, b, rows, start, size):
        return ref[start // tn, b, rows, start % tn:start % tn + size]

    @pl.when(pl.program_id(0) == 0)
    def _():
        st_ref[...] = jnp.zeros(st_ref.shape, F32)

    for b in range(nb):
        pre = _dot(small_ref[b].astype(BF16), wgu_ref[...]) + gb_ref[...]
        la_ref[b] = _log_sigmoid(pre) * (1.0 / GLA_GATE_TAU)

    row = lax.broadcasted_iota(jnp.int32, (CHUNK, CHUNK), 0)
    col = lax.broadcasted_iota(jnp.int32, (CHUNK, CHUNK), 1)
    after = (col > row).astype(F32)

    def prep(c, carry):
        rows = pl.ds(pl.multiple_of(c * CHUNK, CHUNK), CHUNK)
        for b in range(nb):
            la = la_ref[b, rows, :]
            rev = jnp.dot(after, la, precision=lax.Precision.HIGHEST, preferred_element_type=F32)
            for tt in range(k_ref.shape[0]):
                ts = slice(tt * tn, (tt + 1) * tn)
                kd_ref[b, rows, ts] = (k_ref[tt, b, rows, :] * jnp.exp(rev[:, ts])).astype(BF16)
            ea_ref[c, b:b + 1, :] = jnp.exp(jnp.sum(la, axis=0, keepdims=True))
        return carry

    lax.fori_loop(0, n_chunks, prep, 0)

    def step(c, carry):
        rows = pl.ds(pl.multiple_of(c * CHUNK, CHUNK), CHUNK)
        ea = ea_ref[c]
        for b in range(nb):
            for h in range(n_heads):
                ks = slice(h * dk, (h + 1) * dk)
                vs = slice(h * dv, (h + 1) * dv)
                kv_t = lax.dot_general(cols(v_ref, b, rows, h * dv, dv), kd_ref[b, rows, ks],
                                       (((0,), (0,)), ((), ())), preferred_element_type=F32)
                st = st_ref[b * n_heads + h] * ea[b:b + 1, ks] + kv_t
                st_ref[b * n_heads + h] = st
                o = lax.dot_general(cols(q_ref, b, rows, h * dk, dk), st.astype(BF16),
                                    (((1,), (1,)), ((), ())), preferred_element_type=F32)
                on = _rms(o, gain_ref[...])
                r = cols(r_ref, b, rows, h * dv, dv)
                o_ref[b, rows, vs] = (on * (r * jax.nn.sigmoid(r))).astype(o_ref.dtype)
        return carry

    lax.fori_loop(0, n_chunks, step, 0)


def _gla(q, k, v, r, small, wgu, gate_bias, gain, *, batch, seq, n_heads, dk, dv, tb, casts=()):
    kw, vw = n_heads * dk, n_heads * dv
    tn = q.shape[-1]
    assert tn % dk == 0 and tn % dv == 0
    tiles = lambda a: pl.BlockSpec((a.shape[0], batch, tb, tn), lambda t: (0, 0, t, 0))
    cast_in_specs, cast_out_specs, cast_shapes = _cast_specs(casts)
    return pl.pallas_call(
        functools.partial(_gla_kernel, lives=tuple((c.live, c.transpose) for c in casts), n_heads=n_heads,
                          dk=dk, dv=dv),
        grid=(seq // tb,),
        in_specs=[
            tiles(q), tiles(k), tiles(v), tiles(r),
            pl.BlockSpec((batch, tb, LANES), lambda t: (0, t, 0)),
            pl.BlockSpec((LANES, kw), lambda t: (0, 0)),
            pl.BlockSpec((1, kw), lambda t: (0, 0)),
            pl.BlockSpec((1, dv), lambda t: (0, 0)),
        ] + cast_in_specs,
        out_specs=[pl.BlockSpec((batch, tb, vw), lambda t: (0, t, 0))] + cast_out_specs,
        out_shape=[jax.ShapeDtypeStruct((batch, seq, vw), BF16)] + cast_shapes,
        scratch_shapes=[
            pltpu.VMEM((batch * n_heads, dv, dk), F32),
            pltpu.VMEM((batch, tb, kw), F32),
            pltpu.VMEM((batch, tb, kw), BF16),
            pltpu.VMEM((tb // CHUNK, batch, kw), F32),
        ],
        compiler_params=_params("arbitrary"),
        name="gla",
    )(q, k, v, r, small, wgu, gate_bias, gain, *[c.src for c in casts])


def _merge_kernel(*refs, n_parts, nf, tn):
    h_refs = refs[:n_parts]
    g_ref, yf_ref, yg_ref, wg_ref, bg_ref, wa_ref, wb_ref, wo_ref, o_ref = refs[n_parts:]
    d = o_ref.shape[1]

    def body(h_ref):
        h = h_ref[...]
        un = _rms(h, g_ref[...]).astype(BF16)
        yf, yg = yf_ref[...], yg_ref[...]
        acc = h
        for c0 in range(0, d, tn):
            cs, gs = slice(c0, c0 + tn), slice(d + c0, d + c0 + tn)
            g_fox = jax.nn.sigmoid(_dot(un, wg_ref[:, cs]) + bg_ref[:, cs])
            g_gla = jax.nn.sigmoid(_dot(un, wg_ref[:, gs]) + bg_ref[:, gs])
            merged = g_fox * _dot(yf, wa_ref[:, cs]) + g_gla * _dot(yg, wb_ref[:, cs])
            acc = acc + _dot(merged.astype(BF16), wo_ref[cs, :])
        o_ref[...] = acc

    _for_row_part(h_refs, nf, body)


def _merge(h_parts, g, y_fox, y_gla, w_gate, b_gate, w_a, w_b, w_o, *, tm, tn):
    n, d = sum(a.shape[0] for a in h_parts), h_parts[0].shape[1]
    once = dict(pipeline_mode=pl.Buffered(1))
    h_specs, nf = _row_parts_specs(h_parts, tm, **(once if len(h_parts) == 2 else {}))
    whole = lambda a: pl.BlockSpec(a.shape, lambda i: (0, 0), **once)
    return pl.pallas_call(
        functools.partial(_merge_kernel, n_parts=len(h_parts), nf=nf, tn=tn),
        grid=(n // tm,),
        in_specs=h_specs + [
            pl.BlockSpec((1, d), lambda i: (0, 0)),
            pl.BlockSpec((tm, y_fox.shape[1]), lambda i: (i, 0)),
            pl.BlockSpec((tm, y_gla.shape[1]), lambda i: (i, 0)),
            whole(w_gate), pl.BlockSpec((1, 2 * d), lambda i: (0, 0)),
            whole(w_a), whole(w_b), whole(w_o),
        ],
        out_specs=[pl.BlockSpec((tm, d), lambda i: (i, 0))],
        out_shape=[jax.ShapeDtypeStruct((n, d), F32)],
        compiler_params=_params("parallel"),
        name="merge",
    )(*h_parts, g, y_fox, y_gla, w_gate, b_gate, w_a, w_b, w_o)


def _ple_kernel(h_ref, p_ref, gp_ref, gf_ref, wpg_ref, wpe_ref, o_ref, *, final):
    h = h_ref[...]
    hn = _rms(h, gp_ref[...]).astype(BF16)
    gate = jax.nn.sigmoid(_dot(hn, wpg_ref[...]))
    out = h + gate * _dot(p_ref[...].astype(BF16), wpe_ref[...])
    if final:
        out = _rms(out, gf_ref[...])
    o_ref[...] = out


def _ple(h, p, g_ple, g_final, w_gate, w_proj, *, tm, final):
    n, d = h.shape
    dp = p.shape[1]
    return pl.pallas_call(
        functools.partial(_ple_kernel, final=final),
        grid=(n // tm,),
        in_specs=[
            pl.BlockSpec((tm, d), lambda i: (i, 0)),
            pl.BlockSpec((tm, dp), lambda i: (i, 0)),
            pl.BlockSpec((1, d), lambda i: (0, 0)),
            pl.BlockSpec((1, d), lambda i: (0, 0)),
            pl.BlockSpec((d, d), lambda i: (0, 0)),
            pl.BlockSpec((dp, d), lambda i: (0, 0)),
        ],
        out_specs=pl.BlockSpec((tm, d), lambda i: (i, 0)),
        out_shape=jax.ShapeDtypeStruct((n, d), F32),
        compiler_params=_params("parallel"),
        name="ple",
    )(h, p, g_ple, g_final, w_gate, w_proj)


def _tile(n, want):
    t = min(n, want)
    assert n % t == 0, (n, want)
    return t


def kernel(x, p, ffn1_norm, ffn1_w_gate, ffn1_w_up, ffn1_w_down, mix_norm, w_in, fox_forget_bias, gla_gate_up, gla_gate_bias, gla_head_norm, w_branch_fox, w_branch_gla, w_merge_gate, b_merge_gate, w_out, ffn2_norm, ffn2_w_gate, ffn2_w_up, ffn2_w_down, ple_norm, w_ple_proj, w_ple_gate, final_norm):
    batch, seq, d = x.shape
    depth = p.shape[0]
    n = batch * seq
    fox_heads = fox_forget_bias.shape[-1]
    fox_dim = w_branch_fox.shape[1] // fox_heads
    fox_w = fox_heads * fox_dim
    rank, gla_kw = gla_gate_up.shape[1:]
    gla_dv = gla_head_norm.shape[-1]
    gla_vw = w_branch_gla.shape[1]
    gla_heads = gla_vw // gla_dv
    gla_dk = gla_kw // gla_heads
    assert fox_heads + rank <= LANES

    tn = 512
    assert w_in.shape[-1] == 3 * fox_w + fox_heads + 2 * gla_kw + 2 * gla_vw + rank
    assert (3 * fox_w) % tn == 0 and gla_kw % tn == 0 and gla_vw % tn == 0
    assert fox_heads % 8 == 0 and rank % 8 == 0
    lead = 3 * fox_w // tn
    kt, vt = gla_kw // tn, gla_vw // tn
    proj_outs = (_ProjOut(0, lead, 1.0, BF16),
                 _ProjOut(lead, kt, gla_dk ** -0.5, BF16),
                 _ProjOut(lead + kt, kt, 1.0, F32),
                 _ProjOut(lead + 2 * kt, vt, 1.0, BF16),
                 _ProjOut(lead + 2 * kt + vt, vt, 1.0, F32))
    proj_rows = (lead + 2 * kt + 2 * vt) * tn
    fl_rows = (3 * fox_w, fox_heads)
    gd_rows = (w_in.shape[-1] - rank, rank)

    row = lambda v: v.reshape(1, -1).astype(F32)
    h = x.reshape(n, d)
    tm_ffn = _tile(n, 1024)
    tf = _tile(ffn1_w_gate.shape[-1], 512)
    tm_proj = _tile(n, 512)
    tm_merge = _tile(n, 256)
    tm_ple = _tile(n, 512)
    t_fox = _tile(seq, 512)
    fox_hp = 4
    tb_gla = _tile(seq, 256)

    tf_first = _tile(ffn1_w_gate.shape[-1], 256)
    ffn_rest_grid = (n // tm_ffn - 1, ffn1_w_gate.shape[-1] // tf)
    fox_grid = _fox_grid(batch, seq, fox_heads, t_fox, fox_hp)
    gla_grid = (seq // tb_gla,)

    for i in range(depth):
        fbias = jnp.zeros((1, LANES), F32).at[0, :fox_heads].set(fox_forget_bias[i])
        wgu = jnp.zeros((LANES, gla_kw), F32).at[fox_heads:fox_heads + rank].set(gla_gate_up[i]).astype(BF16)

        w_in_t = jnp.swapaxes(w_in, 1, 2)

        h1, w1g, w1u, w1d = _ffn_first(h, row(ffn1_norm[i]), ffn1_w_gate[i], ffn1_w_up[i],
                                       ffn1_w_down[i], tm=tm_ffn, tf=tf_first)
        if ffn_rest_grid[0]:
            h_rest, w_proj = _ffn(h, row(ffn1_norm[i]), w1g, w1u, w1d, tm=tm_ffn, tf=tf,
                                  first_tile=1, casts=(_cast_compact_rows_t(
                                      w_in_t, i, ffn_rest_grid, 3 * fox_w, fox_heads, proj_rows),))
            h_parts = (h1, h_rest)
        else:
            h_parts, w_proj = (h1,), jnp.concatenate(
                [w_in[i][:, :3 * fox_w], w_in[i][:, 3 * fox_w + fox_heads:][:, :proj_rows - 3 * fox_w]],
                axis=1).astype(BF16)

        tiled = lambda a: a.reshape(a.shape[0], batch, seq, a.shape[-1])
        fox_qkv, gq, gk, gv, gr, small = _proj(
            h_parts, row(mix_norm[i]), w_proj, w_in_t, i, tm=tm_proj, tn=tn, outs=proj_outs,
            fl_rows=fl_rows, gd_rows=gd_rows)

        fcum = _fcum(small, fbias, batch=batch, seq=seq)
        y_fox, w2g, w2u, wpg, wpe = _fox(
            tiled(fox_qkv), fcum, batch=batch, seq=seq, n_heads=fox_heads,
            head_dim=fox_dim, t=t_fox, hp=fox_hp,
            casts=tuple(_cast_row_blocks(w, fox_grid) for w in (
                ffn2_w_gate[i], ffn2_w_up[i], w_ple_gate[i], w_ple_proj[i])))

        y_gla, wmg, wbf, wbg, wo, w2d = _gla(
            tiled(gq), tiled(gk), tiled(gv), tiled(gr), small.reshape(batch, seq, LANES), wgu,
            row(gla_gate_bias[i]),
            row(gla_head_norm[i]), batch=batch, seq=seq, n_heads=gla_heads, dk=gla_dk, dv=gla_dv,
            tb=tb_gla,
            casts=tuple(_cast_row_blocks(w, gla_grid) for w in (
                w_merge_gate[i], w_branch_fox[i], w_branch_gla[i], w_out[i], ffn2_w_down[i])))

        h, = _merge(h_parts, row(mix_norm[i]), y_fox.reshape(n, fox_w), y_gla.reshape(n, gla_vw),
                    wmg, row(b_merge_gate[i]), wbf, wbg, wo, tm=tm_merge, tn=tn)

        h, = _ffn(h, row(ffn2_norm[i]), w2g, w2u, w2d, tm=tm_ffn, tf=tf)

        h = _ple(h, p[i].reshape(n, -1), row(ple_norm[i]), row(final_norm), wpg, wpe,
                 tm=tm_ple, final=(i == depth - 1))

    return h.reshape(batch, seq, d)
```

```python
import functools
import math
from typing import Callable, NamedTuple

import jax
import jax.numpy as jnp
from jax import lax
from jax.experimental import pallas as pl
from jax.experimental.pallas import tpu as pltpu

EPS = 1e-6
CHUNK = 64
GLA_GATE_TAU = 16.0
LANES = 128
V7X_VMEM_LIMIT_BYTES = 62 * 1024 * 1024
LOG2E = math.log2(math.e)

F32 = jnp.float32
BF16 = jnp.bfloat16


def _rms(x, g):
    return x * lax.rsqrt(jnp.mean(x * x, axis=-1, keepdims=True) + EPS) * g


def _log_sigmoid(z):
    return jnp.minimum(z, 0.0) - jnp.log1p(jnp.exp(-jnp.abs(z)))


def _dot(a, b):
    return lax.dot_general(a, b, (((1,), (0,)), ((), ())), preferred_element_type=F32)


def _params(*sem):
    return pltpu.CompilerParams(dimension_semantics=sem,
                                vmem_limit_bytes=V7X_VMEM_LIMIT_BYTES)


class _Cast(NamedTuple):
    src: jax.Array
    src_spec: pl.BlockSpec
    dst_spec: pl.BlockSpec
    dst_shape: tuple
    live: Callable
    transpose: bool = False


def _flat_step(grid):
    def step(*ids):
        s = ids[0]
        for size, idx in zip(grid[1:], ids[1:]):
            s = s * size + idx
        return s
    return step


def _row_block_count(rows, n_steps, extra=lambda rb: True):
    ok = lambda k: rows % k == 0 and (rows // k) % 16 == 0 and extra(rows // k)
    return max(k for k in range(1, n_steps + 1) if ok(k))


def _cast_row_blocks(src, grid):
    r, c = src.shape
    step = _flat_step(grid)
    nb = _row_block_count(r, math.prod(grid))
    spec = pl.BlockSpec((r // nb, c), lambda *ids: (jnp.minimum(step(*ids), nb - 1), 0))
    return _Cast(src, spec, spec, src.shape, lambda *ids: step(*ids) < nb)


def _cast_compact_rows_t(src, layer, grid, lead_rows, skip, total_rows):
    d = src.shape[2]
    step = _flat_step(grid)
    nb = _row_block_count(total_rows, math.prod(grid),
                          lambda rb: lead_rows % rb == 0 and rb % LANES == 0)
    rb = total_rows // nb
    blk = lambda *ids: jnp.minimum(step(*ids), nb - 1)
    src_row = lambda *ids: pl.multiple_of(
        blk(*ids) * rb + jnp.where(blk(*ids) * rb < lead_rows, 0, skip), 8)
    return _Cast(src,
                 pl.BlockSpec((None, pl.Element(rb), pl.Element(d)),
                              lambda *ids: (layer, src_row(*ids), 0)),
                 pl.BlockSpec((d, rb), lambda *ids: (0, blk(*ids))),
                 (d, total_rows),
                 lambda *ids: step(*ids) < nb,
                 transpose=True)


def _cast_specs(casts):
    shapes = [jax.ShapeDtypeStruct(c.dst_shape, BF16) for c in casts]
    return [c.src_spec for c in casts], [c.dst_spec for c in casts], shapes


def _run_casts(lives, src_refs, dst_refs, n_axes):
    ids = [pl.program_id(a) for a in range(n_axes)]
    for (live, transpose), src, dst in zip(lives, src_refs, dst_refs):
        @pl.when(live(*ids))
        def _():
            blk = src[...]
            dst[...] = (blk.T if transpose else blk).astype(BF16)


def _ffn_step(x_ref, g_ref, o_ref, xn_ref, wg, wu, wd):
    @pl.when(pl.program_id(1) == 0)
    def _():
        x = x_ref[...]
        xn_ref[...] = _rms(x, g_ref[...]).astype(BF16)
        o_ref[...] = x

    xn = xn_ref[...]
    gate = _dot(xn, wg())
    up = _dot(xn, wu())
    hid = (0.5 * (gate * jax.nn.sigmoid(gate)) * up).astype(BF16)
    o_ref[...] += _dot(hid, wd())


def _ffn_kernel(*refs, lives):
    nc = len(lives)
    x_ref, g_ref, wg_ref, wu_ref, wd_ref = refs[:5]
    o_ref, xn_ref = refs[5 + nc], refs[-1]
    _ffn_step(x_ref, g_ref, o_ref, xn_ref,
              lambda: wg_ref[...], lambda: wu_ref[...], lambda: wd_ref[...])
    _run_casts(lives, refs[5:5 + nc], refs[6 + nc:6 + 2 * nc], 2)


def _ffn(x, g, wg, wu, wd, *, tm, tf, casts=(), first_tile=0):
    n, d = x.shape
    f = wg.shape[1]
    cast_in_specs, cast_out_specs, cast_shapes = _cast_specs(casts)
    return pl.pallas_call(
        functools.partial(_ffn_kernel, lives=tuple((c.live, c.transpose) for c in casts)),
        grid=(n // tm - first_tile, f // tf),
        in_specs=[
            pl.BlockSpec((tm, d), lambda i, j: (i + first_tile, 0)),
            pl.BlockSpec((1, d), lambda i, j: (0, 0)),
            pl.BlockSpec((d, tf), lambda i, j: (0, j)),
            pl.BlockSpec((d, tf), lambda i, j: (0, j)),
            pl.BlockSpec((tf, d), lambda i, j: (j, 0)),
        ] + cast_in_specs,
        out_specs=[pl.BlockSpec((tm, d), lambda i, j: (i, 0))] + cast_out_specs,
        out_shape=[jax.ShapeDtypeStruct((n - first_tile * tm, d), F32)] + cast_shapes,
        scratch_shapes=[pltpu.VMEM((tm, d), BF16)],
        compiler_params=_params("parallel", "arbitrary"),
        name="ffn",
    )(x, g, wg, wu, wd, *[c.src for c in casts])


def _ffn_first_kernel(x_ref, g_ref, wg_ref, wu_ref, wd_ref, o_ref, wgb_ref, wub_ref, wdb_ref,
                      xn_ref):
    def cast(src, dst):
        def thunk():
            w = src[...].astype(BF16)
            dst[...] = w
            return w
        return thunk

    _ffn_step(x_ref, g_ref, o_ref, xn_ref, cast(wg_ref, wgb_ref), cast(wu_ref, wub_ref),
              cast(wd_ref, wdb_ref))


def _ffn_first(x, g, wg, wu, wd, *, tm, tf):
    n, d = x.shape
    f = wg.shape[1]
    w_specs = [pl.BlockSpec((d, tf), lambda i, j: (0, j)), pl.BlockSpec((d, tf), lambda i, j: (0, j)),
               pl.BlockSpec((tf, d), lambda i, j: (j, 0))]
    return pl.pallas_call(
        _ffn_first_kernel,
        grid=(1, f // tf),
        in_specs=[pl.BlockSpec((tm, d), lambda i, j: (0, 0)),
                  pl.BlockSpec((1, d), lambda i, j: (0, 0))] + w_specs,
        out_specs=[pl.BlockSpec((tm, d), lambda i, j: (0, 0))] + w_specs,
        out_shape=[jax.ShapeDtypeStruct((tm, d), F32)]
        + [jax.ShapeDtypeStruct(w.shape, BF16) for w in (wg, wu, wd)],
        scratch_shapes=[pltpu.VMEM((tm, d), BF16)],
        compiler_params=_params("arbitrary", "arbitrary"),
        name="ffn_first",
    )(x, g, wg, wu, wd)


class _ProjOut(NamedTuple):
    first: int
    count: int
    scale: float
    dtype: type


def _dot_nt(a, b):
    return lax.dot_general(a, b, (((1,), (1,)), ((), ())), preferred_element_type=F32)


def _row_parts_specs(parts, tm, **kw):
    d = parts[0].shape[1]
    nf = parts[0].shape[0] // tm
    specs = [pl.BlockSpec((tm, d), lambda i, *_: (jnp.minimum(i, nf - 1), 0), **kw)]
    if len(parts) == 2:
        specs.append(pl.BlockSpec((tm, d), lambda i, *_: (jnp.maximum(i - nf, 0), 0)))
    return specs, nf


def _for_row_part(part_refs, nf, fn):
    if len(part_refs) == 1:
        return fn(part_refs[0])
    i = pl.program_id(0)
    pl.when(i < nf)(lambda: fn(part_refs[0]))
    pl.when(i >= nf)(lambda: fn(part_refs[1]))


def _proj_kernel(*refs, outs, tn, n_fl, rank, n_parts, nf):
    h_refs = refs[:n_parts]
    g_ref, w_ref, wfl_ref, wgd_ref = refs[n_parts:n_parts + 4]
    out_refs = refs[n_parts + 4:n_parts + 4 + len(outs)]
    small_ref = refs[n_parts + 4 + len(outs)]

    def body(h_ref):
        un = _rms(h_ref[...], g_ref[...]).astype(BF16)
        small_ref[...] = jnp.zeros(small_ref.shape, F32)
        small_ref[:, :n_fl] = _dot_nt(un, wfl_ref[...])
        small_ref[:, n_fl:n_fl + rank] = _dot_nt(un, wgd_ref[...])
        for o_ref, o in zip(out_refs, outs):
            for t in range(o.count):
                c0 = (o.first + t) * tn
                tile = _dot(un, w_ref[:, c0:c0 + tn])
                if o.scale != 1.0:
                    tile = tile * o.scale
                o_ref[t] = tile.astype(o_ref.dtype)

    _for_row_part(h_refs, nf, body)


def _proj(h_parts, g, w_main, w_in_t, layer, *, tm, tn, outs, fl_rows, gd_rows):
    n, d = sum(a.shape[0] for a in h_parts), h_parts[0].shape[1]
    assert fl_rows[1] + gd_rows[1] <= LANES
    once = dict(pipeline_mode=pl.Buffered(1))
    h_specs, nf = _row_parts_specs(h_parts, tm, **(once if len(h_parts) == 2 else {}))
    kern = functools.partial(_proj_kernel, outs=outs, tn=tn, n_fl=fl_rows[1], rank=gd_rows[1],
                             n_parts=len(h_parts), nf=nf)
    rows = lambda start, size: pl.BlockSpec(
        (None, pl.Element(size), pl.Element(d)), lambda i: (layer, start, 0), **once)
    return pl.pallas_call(
        kern,
        grid=(n // tm,),
        in_specs=h_specs + [
            pl.BlockSpec((1, d), lambda i: (0, 0)),
            pl.BlockSpec(w_main.shape, lambda i: (0, 0), **once),
            rows(*fl_rows),
            rows(*gd_rows),
        ],
        out_specs=[pl.BlockSpec((o.count, tm, tn), lambda i: (0, i, 0)) for o in outs]
        + [pl.BlockSpec((tm, LANES), lambda i: (i, 0))],
        out_shape=[jax.ShapeDtypeStruct((o.count, n, tn), o.dtype) for o in outs]
        + [jax.ShapeDtypeStruct((n, LANES), F32)],
        compiler_params=_params("parallel"),
        name="proj",
    )(*h_parts, g, w_main, w_in_t, w_in_t)


def _fcum_kernel(small_ref, bias_ref, f_ref):
    s = small_ref.shape[0]
    f_ref[0] = _log_sigmoid(small_ref[...] + bias_ref[...])
    row = lax.broadcasted_iota(jnp.int32, (LANES, LANES), 0)
    col = lax.broadcasted_iota(jnp.int32, (LANES, LANES), 1)
    tril = (row >= col).astype(F32)

    carry = jnp.zeros((1, LANES), F32)
    for r in range(s // LANES):
        rows = slice(r * LANES, (r + 1) * LANES)
        c = jnp.dot(tril, f_ref[0, rows, :], precision=lax.Precision.HIGHEST,
                    preferred_element_type=F32)
        f_ref[0, rows, :] = c + carry
        carry = carry + c[LANES - 1:LANES, :]


def _fcum(small, bias_row, *, batch, seq):
    return pl.pallas_call(
        _fcum_kernel,
        grid=(batch,),
        in_specs=[
            pl.BlockSpec((seq, LANES), lambda b: (b, 0)),
            pl.BlockSpec((1, LANES), lambda b: (0, 0)),
        ],
        out_specs=pl.BlockSpec((1, seq, LANES), lambda b: (b, 0, 0)),
        out_shape=jax.ShapeDtypeStruct((batch, seq, LANES), F32),
        compiler_params=_params("parallel"),
        name="fcum",
    )(small, bias_row)


def _fox_kernel(*refs, lives, t, c2, hp, dh):
    nc = len(lives)
    q_ref, k_ref, v_ref, f_ref = refs[:4]
    o_ref = refs[4 + nc]
    (frep_ref, vt_ref, m_ref, l_ref, acc_ref, s0_ref, s1_ref, mc0_ref, mc1_ref) = refs[5 + 2 * nc:]
    _run_casts(lives, refs[4:4 + nc], refs[5 + nc:5 + 2 * nc], 3)
    g = pl.program_id(1)
    i = pl.program_id(2)
    s_refs = (s0_ref, s1_ref)
    mc_refs = (mc0_ref, mc1_ref)

    @pl.when(i == 0)
    def _():
        row = lax.broadcasted_iota(jnp.int32, (LANES, LANES), 0)
        for h in range(hp):
            onehot = (row == g * hp + h).astype(F32)
            frep_ref[h] = jnp.dot(f_ref[0], onehot, precision=lax.Precision.HIGHEST,
                                  preferred_element_type=F32)
            vt_ref[h] = v_ref[0, :, h * dh:(h + 1) * dh].astype(F32).T.astype(BF16)

    q0 = pl.multiple_of(i * t, t)
    m_ref[...] = jnp.full(m_ref.shape, -jnp.inf, F32)
    l_ref[...] = jnp.zeros(l_ref.shape, F32)
    acc_ref[...] = jnp.zeros(acc_ref.shape, F32)
    reps = t // LANES

    def scores(kk, slot, masked):
        k0 = pl.multiple_of(kk * t, t)
        for h in range(hp):
            hs = slice(h * dh, (h + 1) * dh)
            f_base = frep_ref[h, pl.ds(q0, 8), :][0:1, :]
            bias = (f_base - frep_ref[h, pl.ds(k0, t), :]) * LOG2E
            s = lax.dot_general(k_ref[0, pl.ds(k0, t), hs], q_ref[0, :, hs],
                                (((1,), (1,)), ((), ())), preferred_element_type=F32)
            s = s * c2 + jnp.tile(bias, (1, reps))
            if masked:
                key = lax.broadcasted_iota(jnp.int32, (t, t), 0)
                qry = lax.broadcasted_iota(jnp.int32, (t, t), 1)
                s = jnp.where(key <= qry, s, -jnp.inf)
            s_refs[slot][h] = s
            mc_refs[slot][h] = jnp.max(s, axis=0, keepdims=True)

    def absorb(kk, slot):
        k0 = pl.multiple_of(kk * t, t)
        for h in range(hp):
            m_old = m_ref[h]
            m_new = jnp.maximum(m_old, mc_refs[slot][h])
            alpha = jnp.exp2(m_old - m_new)
            p = jnp.exp2(s_refs[slot][h] - m_new)
            l_ref[h] = alpha * l_ref[h] + jnp.sum(p, axis=0, keepdims=True)
            acc_ref[h] = alpha * acc_ref[h] + _dot(vt_ref[h, :, pl.ds(k0, t)], p.astype(BF16))
            m_ref[h] = m_new

    scores(i, 0, True)

    def body(kk, carry):
        prev = jnp.where(kk == 0, i, kk - 1)
        for par in range(2):
            @pl.when(kk % 2 == par)
            def _():
                absorb(prev, par)
                scores(kk, 1 - par, False)
        return carry

    lax.fori_loop(0, i, body, 0)
    last = jnp.where(i == 0, i, i - 1)
    for par in range(2):
        @pl.when(i % 2 == par)
        def _():
            absorb(last, par)
    for h in range(hp):
        o_ref[0, :, h * dh:(h + 1) * dh] = (acc_ref[h] / l_ref[h]).T.astype(o_ref.dtype)


def _fox_grid(batch, seq, n_heads, t, hp):
    return (batch, n_heads // hp, seq // t)


def _fox(qkv, fcum, *, batch, seq, n_heads, head_dim, t, hp, casts=()):
    assert head_dim == LANES and n_heads % hp == 0
    c2 = (head_dim ** -0.5) * LOG2E
    ng = n_heads // hp
    w = hp * head_dim
    assert qkv.shape == (3 * ng, batch, seq, w)
    cast_in_specs, cast_out_specs, cast_shapes = _cast_specs(casts)
    return pl.pallas_call(
        functools.partial(_fox_kernel, lives=tuple((c.live, c.transpose) for c in casts), t=t, c2=c2, hp=hp,
                          dh=head_dim),
        grid=_fox_grid(batch, seq, n_heads, t, hp),
        in_specs=[
            pl.BlockSpec((None, 1, t, w), lambda b, g, i: (g, b, i, 0)),
            pl.BlockSpec((None, 1, seq, w), lambda b, g, i: (ng + g, b, 0, 0)),
            pl.BlockSpec((None, 1, seq, w), lambda b, g, i: (2 * ng + g, b, 0, 0)),
            pl.BlockSpec((1, seq, LANES), lambda b, g, i: (b, 0, 0)),
        ] + cast_in_specs,
        out_specs=[pl.BlockSpec((1, t, w), lambda b, g, i: (b, i, g))] + cast_out_specs,
        out_shape=[jax.ShapeDtypeStruct((batch, seq, n_heads * head_dim), BF16)] + cast_shapes,
        scratch_shapes=[pltpu.VMEM((hp, seq, LANES), F32), pltpu.VMEM((hp, head_dim, seq), BF16),
                        pltpu.VMEM((hp, 1, t), F32), pltpu.VMEM((hp, 1, t), F32),
                        pltpu.VMEM((hp, head_dim, t), F32),
                        pltpu.VMEM((hp, t, t), F32), pltpu.VMEM((hp, t, t), F32),
                        pltpu.VMEM((hp, 1, t), F32), pltpu.VMEM((hp, 1, t), F32)],
        compiler_params=_params("parallel", "arbitrary", "arbitrary"),
        name="fox",
    )(qkv, qkv, qkv, fcum, *[c.src for c in casts])


def _gla_kernel(*refs, lives, n_heads, dk, dv):
    nc = len(lives)
    q_ref, k_ref, v_ref, r_ref, small_ref, wgu_ref, gb_ref, gain_ref = refs[:8]
    o_ref = refs[8 + nc]
    st_ref, la_ref, kd_ref, ea_ref = refs[9 + 2 * nc:]
    _run_casts(lives, refs[8:8 + nc], refs[9 + nc:9 + 2 * nc], 1)
    nb, tb, tn = q_ref.shape[1:]
    n_chunks = tb // CHUNK

    def cols(ref, b, rows, start, size):
        return ref[start // tn, b, rows, start % tn:start % tn + size]

    @pl.when(pl.program_id(0) == 0)
    def _():
        st_ref[...] = jnp.zeros(st_ref.shape, F32)

    for b in range(nb):
        pre = _dot(small_ref[b].astype(BF16), wgu_ref[...]) + gb_ref[...]
        la_ref[b] = _log_sigmoid(pre) * (1.0 / GLA_GATE_TAU)

    row = lax.broadcasted_iota(jnp.int32, (CHUNK, CHUNK), 0)
    col = lax.broadcasted_iota(jnp.int32, (CHUNK, CHUNK), 1)
    after = (col > row).astype(F32)

    def prep(c, carry):
        rows = pl.ds(pl.multiple_of(c * CHUNK, CHUNK), CHUNK)
        for b in range(nb):
            la = la_ref[b, rows, :]
            rev = jnp.dot(after, la, precision=lax.Precision.HIGHEST, preferred_element_type=F32)
            for tt in range(k_ref.shape[0]):
                ts = slice(tt * tn, (tt + 1) * tn)
                kd_ref[b, rows, ts] = (k_ref[tt, b, rows, :] * jnp.exp(rev[:, ts])).astype(BF16)
            ea_ref[c, b:b + 1, :] = jnp.exp(jnp.sum(la, axis=0, keepdims=True))
        return carry

    lax.fori_loop(0, n_chunks, prep, 0)

    def step(c, carry):
        rows = pl.ds(pl.multiple_of(c * CHUNK, CHUNK), CHUNK)
        ea = ea_ref[c]
        for b in range(nb):
            for h in range(n_heads):
                ks = slice(h * dk, (h + 1) * dk)
                vs = slice(h * dv, (h + 1) * dv)
                kv_t = lax.dot_general(cols(v_ref, b, rows, h * dv, dv), kd_ref[b, rows, ks],
                                       (((0,), (0,)), ((), ())), preferred_element_type=F32)
                st = st_ref[b * n_heads + h] * ea[b:b + 1, ks] + kv_t
                st_ref[b * n_heads + h] = st
                o = lax.dot_general(cols(q_ref, b, rows, h * dk, dk), st.astype(BF16),
                                    (((1,), (1,)), ((), ())), preferred_element_type=F32)
                on = _rms(o, gain_ref[...])
                r = cols(r_ref, b, rows, h * dv, dv)
                o_ref[b, rows, vs] = (on * (r * jax.nn.sigmoid(r))).astype(o_ref.dtype)
        return carry

    lax.fori_loop(0, n_chunks, step, 0)


def _gla(q, k, v, r, small, wgu, gate_bias, gain, *, batch, seq, n_heads, dk, dv, tb, casts=()):
    kw, vw = n_heads * dk, n_heads * dv
    tn = q.shape[-1]
    assert tn % dk == 0 and tn % dv == 0
    tiles = lambda a: pl.BlockSpec((a.shape[0], batch, tb, tn), lambda t: (0, 0, t, 0))
    cast_in_specs, cast_out_specs, cast_shapes = _cast_specs(casts)
    return pl.pallas_call(
        functools.partial(_gla_kernel, lives=tuple((c.live, c.transpose) for c in casts), n_heads=n_heads,
                          dk=dk, dv=dv),
        grid=(seq // tb,),
        in_specs=[
            tiles(q), tiles(k), tiles(v), tiles(r),
            pl.BlockSpec((batch, tb, LANES), lambda t: (0, t, 0)),
            pl.BlockSpec((LANES, kw), lambda t: (0, 0)),
            pl.BlockSpec((1, kw), lambda t: (0, 0)),
            pl.BlockSpec((1, dv), lambda t: (0, 0)),
        ] + cast_in_specs,
        out_specs=[pl.BlockSpec((batch, tb, vw), lambda t: (0, t, 0))] + cast_out_specs,
        out_shape=[jax.ShapeDtypeStruct((batch, seq, vw), BF16)] + cast_shapes,
        scratch_shapes=[
            pltpu.VMEM((batch * n_heads, dv, dk), F32),
            pltpu.VMEM((batch, tb, kw), F32),
            pltpu.VMEM((batch, tb, kw), BF16),
            pltpu.VMEM((tb // CHUNK, batch, kw), F32),
        ],
        compiler_params=_params("arbitrary"),
        name="gla",
    )(q, k, v, r, small, wgu, gate_bias, gain, *[c.src for c in casts])


def _merge_kernel(*refs, lives, n_parts, nf, tn):
    nc = len(lives)
    h_refs = refs[:n_parts]
    g_ref, yf_ref, yg_ref, wg_ref, bg_ref, wa_ref, wb_ref, wo_ref = refs[n_parts:n_parts + 8]
    n_in = n_parts + 8
    o_ref = refs[n_in + nc]
    _run_casts(lives, refs[n_in:n_in + nc], refs[n_in + nc + 1:], 1)
    d = o_ref.shape[1]

    def body(h_ref):
        h = h_ref[...]
        un = _rms(h, g_ref[...]).astype(BF16)
        yf, yg = yf_ref[...], yg_ref[...]
        acc = h
        for c0 in range(0, d, tn):
            cs, gs = slice(c0, c0 + tn), slice(d + c0, d + c0 + tn)
            g_fox = jax.nn.sigmoid(_dot(un, wg_ref[:, cs]) + bg_ref[:, cs])
            g_gla = jax.nn.sigmoid(_dot(un, wg_ref[:, gs]) + bg_ref[:, gs])
            merged = g_fox * _dot(yf, wa_ref[:, cs]) + g_gla * _dot(yg, wb_ref[:, cs])
            acc = acc + _dot(merged.astype(BF16), wo_ref[cs, :])
        o_ref[...] = acc

    _for_row_part(h_refs, nf, body)


def _merge(h_parts, g, y_fox, y_gla, w_gate, b_gate, w_a, w_b, w_o, *, tm, tn, casts=()):
    n, d = sum(a.shape[0] for a in h_parts), h_parts[0].shape[1]
    once = dict(pipeline_mode=pl.Buffered(1))
    h_specs, nf = _row_parts_specs(h_parts, tm, **(once if len(h_parts) == 2 else {}))
    whole = lambda a: pl.BlockSpec(a.shape, lambda i: (0, 0), **once)
    cast_in_specs, cast_out_specs, cast_shapes = _cast_specs(casts)
    return pl.pallas_call(
        functools.partial(_merge_kernel, lives=tuple((c.live, c.transpose) for c in casts),
                          n_parts=len(h_parts), nf=nf, tn=tn),
        grid=(n // tm,),
        in_specs=h_specs + [
            pl.BlockSpec((1, d), lambda i: (0, 0)),
            pl.BlockSpec((tm, y_fox.shape[1]), lambda i: (i, 0)),
            pl.BlockSpec((tm, y_gla.shape[1]), lambda i: (i, 0)),
            whole(w_gate), pl.BlockSpec((1, 2 * d), lambda i: (0, 0)),
            whole(w_a), whole(w_b), whole(w_o),
        ] + cast_in_specs,
        out_specs=[pl.BlockSpec((tm, d), lambda i: (i, 0))] + cast_out_specs,
        out_shape=[jax.ShapeDtypeStruct((n, d), F32)] + cast_shapes,
        compiler_params=_params("parallel"),
        name="merge",
    )(*h_parts, g, y_fox, y_gla, w_gate, b_gate, w_a, w_b, w_o, *[c.src for c in casts])


def _ple_kernel(h_ref, p_ref, gp_ref, gf_ref, wpg_ref, wpe_ref, o_ref, *, final):
    h = h_ref[...]
    hn = _rms(h, gp_ref[...]).astype(BF16)
    gate = jax.nn.sigmoid(_dot(hn, wpg_ref[...]))
    out = h + gate * _dot(p_ref[...].astype(BF16), wpe_ref[...])
    if final:
        out = _rms(out, gf_ref[...])
    o_ref[...] = out


def _ple(h, p, g_ple, g_final, w_gate, w_proj, *, tm, final):
    n, d = h.shape
    dp = p.shape[1]
    return pl.pallas_call(
        functools.partial(_ple_kernel, final=final),
        grid=(n // tm,),
        in_specs=[
            pl.BlockSpec((tm, d), lambda i: (i, 0)),
            pl.BlockSpec((tm, dp), lambda i: (i, 0)),
            pl.BlockSpec((1, d), lambda i: (0, 0)),
            pl.BlockSpec((1, d), lambda i: (0, 0)),
            pl.BlockSpec((d, d), lambda i: (0, 0)),
            pl.BlockSpec((dp, d), lambda i: (0, 0)),
        ],
        out_specs=pl.BlockSpec((tm, d), lambda i: (i, 0)),
        out_shape=jax.ShapeDtypeStruct((n, d), F32),
        compiler_params=_params("parallel"),
        name="ple",
    )(h, p, g_ple, g_final, w_gate, w_proj)


def _tile(n, want):
    t = min(n, want)
    assert n % t == 0, (n, want)
    return t


def kernel(x, p, ffn1_norm, ffn1_w_gate, ffn1_w_up, ffn1_w_down, mix_norm, w_in, fox_forget_bias, gla_gate_up, gla_gate_bias, gla_head_norm, w_branch_fox, w_branch_gla, w_merge_gate, b_merge_gate, w_out, ffn2_norm, ffn2_w_gate, ffn2_w_up, ffn2_w_down, ple_norm, w_ple_proj, w_ple_gate, final_norm):
    batch, seq, d = x.shape
    depth = p.shape[0]
    n = batch * seq
    fox_heads = fox_forget_bias.shape[-1]
    fox_dim = w_branch_fox.shape[1] // fox_heads
    fox_w = fox_heads * fox_dim
    rank, gla_kw = gla_gate_up.shape[1:]
    gla_dv = gla_head_norm.shape[-1]
    gla_vw = w_branch_gla.shape[1]
    gla_heads = gla_vw // gla_dv
    gla_dk = gla_kw // gla_heads
    assert fox_heads + rank <= LANES

    tn = 512
    assert w_in.shape[-1] == 3 * fox_w + fox_heads + 2 * gla_kw + 2 * gla_vw + rank
    assert (3 * fox_w) % tn == 0 and gla_kw % tn == 0 and gla_vw % tn == 0
    assert fox_heads % 8 == 0 and rank % 8 == 0
    lead = 3 * fox_w // tn
    kt, vt = gla_kw // tn, gla_vw // tn
    proj_outs = (_ProjOut(0, lead, 1.0, BF16),
                 _ProjOut(lead, kt, gla_dk ** -0.5, BF16),
                 _ProjOut(lead + kt, kt, 1.0, F32),
                 _ProjOut(lead + 2 * kt, vt, 1.0, BF16),
                 _ProjOut(lead + 2 * kt + vt, vt, 1.0, F32))
    proj_rows = (lead + 2 * kt + 2 * vt) * tn
    fl_rows = (3 * fox_w, fox_heads)
    gd_rows = (w_in.shape[-1] - rank, rank)

    row = lambda v: v.reshape(1, -1).astype(F32)
    h = x.reshape(n, d)
    tm_ffn = _tile(n, 1024)
    tf = _tile(ffn1_w_gate.shape[-1], 512)
    tm_proj = _tile(n, 512)
    tm_merge = _tile(n, 256)
    tm_ple = _tile(n, 512)
    t_fox = _tile(seq, 512)
    fox_hp = 4
    tb_gla = _tile(seq, 256)

    tf_first = _tile(ffn1_w_gate.shape[-1], 256)
    ffn_rest_grid = (n // tm_ffn - 1, ffn1_w_gate.shape[-1] // tf)
    gla_grid = (seq // tb_gla,)

    for i in range(depth):
        fbias = jnp.zeros((1, LANES), F32).at[0, :fox_heads].set(fox_forget_bias[i])
        wgu = jnp.zeros((LANES, gla_kw), F32).at[fox_heads:fox_heads + rank].set(gla_gate_up[i]).astype(BF16)

        w_in_t = jnp.swapaxes(w_in, 1, 2)

        h1, w1g, w1u, w1d = _ffn_first(h, row(ffn1_norm[i]), ffn1_w_gate[i], ffn1_w_up[i],
                                       ffn1_w_down[i], tm=tm_ffn, tf=tf_first)
        if ffn_rest_grid[0]:
            h_rest, w_proj = _ffn(h, row(ffn1_norm[i]), w1g, w1u, w1d, tm=tm_ffn, tf=tf,
                                  first_tile=1, casts=(_cast_compact_rows_t(
                                      w_in_t, i, ffn_rest_grid, 3 * fox_w, fox_heads, proj_rows),))
            h_parts = (h1, h_rest)
        else:
            h_parts, w_proj = (h1,), jnp.concatenate(
                [w_in[i][:, :3 * fox_w], w_in[i][:, 3 * fox_w + fox_heads:][:, :proj_rows - 3 * fox_w]],
                axis=1).astype(BF16)

        tiled = lambda a: a.reshape(a.shape[0], batch, seq, a.shape[-1])
        fox_qkv, gq, gk, gv, gr, small = _proj(
            h_parts, row(mix_norm[i]), w_proj, w_in_t, i, tm=tm_proj, tn=tn, outs=proj_outs,
            fl_rows=fl_rows, gd_rows=gd_rows)

        fcum = _fcum(small, fbias, batch=batch, seq=seq)
        y_fox, = _fox(tiled(fox_qkv), fcum, batch=batch, seq=seq, n_heads=fox_heads,
                      head_dim=fox_dim, t=t_fox, hp=fox_hp)

        y_gla, wmg, wbf, wbg, wo, w2d = _gla(
            tiled(gq), tiled(gk), tiled(gv), tiled(gr), small.reshape(batch, seq, LANES), wgu,
            row(gla_gate_bias[i]),
            row(gla_head_norm[i]), batch=batch, seq=seq, n_heads=gla_heads, dk=gla_dk, dv=gla_dv,
            tb=tb_gla,
            casts=tuple(_cast_row_blocks(w, gla_grid) for w in (
                w_merge_gate[i], w_branch_fox[i], w_branch_gla[i], w_out[i], ffn2_w_down[i])))

        h, w2g, w2u, wpg, wpe = _merge(
            h_parts, row(mix_norm[i]), y_fox.reshape(n, fox_w), y_gla.reshape(n, gla_vw),
            wmg, row(b_merge_gate[i]), wbf, wbg, wo, tm=tm_merge, tn=tn,
            casts=tuple(_cast_row_blocks(w, (n // tm_merge,)) for w in (
                ffn2_w_gate[i], ffn2_w_up[i], w_ple_gate[i], w_ple_proj[i])))

        h, = _ffn(h, row(ffn2_norm[i]), w2g, w2u, w2d, tm=tm_ffn, tf=tf)

        h = _ple(h, p[i].reshape(n, -1), row(ple_norm[i]), row(final_norm), wpg, wpe,
                 tm=tm_ple, final=(i == depth - 1))

    return h.reshape(batch, seq, d)
```

```python
import functools
import math
from typing import Callable, NamedTuple

import jax
import jax.numpy as jnp
from jax import lax
from jax.experimental import pallas as pl
from jax.experimental.pallas import tpu as pltpu

EPS = 1e-6
CHUNK = 64
GLA_GATE_TAU = 16.0
LANES = 128
V7X_VMEM_LIMIT_BYTES = 62 * 1024 * 1024
LOG2E = math.log2(math.e)

F32 = jnp.float32
BF16 = jnp.bfloat16


def _rms(x, g):
    return x * lax.rsqrt(jnp.mean(x * x, axis=-1, keepdims=True) + EPS) * g


def _log_sigmoid(z):
    return jnp.minimum(z, 0.0) - jnp.log1p(jnp.exp(-jnp.abs(z)))


def _dot(a, b):
    return lax.dot_general(a, b, (((1,), (0,)), ((), ())), preferred_element_type=F32)


def _params(*sem):
    return pltpu.CompilerParams(dimension_semantics=sem,
                                vmem_limit_bytes=V7X_VMEM_LIMIT_BYTES)


class _Cast(NamedTuple):
    src: jax.Array
    src_spec: pl.BlockSpec
    dst_spec: pl.BlockSpec
    dst_shape: tuple
    live: Callable
    transpose: bool = False


def _flat_step(grid):
    def step(*ids):
        s = ids[0]
        for size, idx in zip(grid[1:], ids[1:]):
            s = s * size + idx
        return s
    return step


def _row_block_count(rows, n_steps, extra=lambda rb: True):
    ok = lambda k: rows % k == 0 and (rows // k) % 16 == 0 and extra(rows // k)
    return max(k for k in range(1, n_steps + 1) if ok(k))


def _cast_row_blocks(src, grid):
    r, c = src.shape
    step = _flat_step(grid)
    nb = _row_block_count(r, math.prod(grid))
    spec = pl.BlockSpec((r // nb, c), lambda *ids: (jnp.minimum(step(*ids), nb - 1), 0))
    return _Cast(src, spec, spec, src.shape, lambda *ids: step(*ids) < nb)


def _cast_compact_rows_t(src, layer, grid, lead_rows, skip, total_rows):
    d = src.shape[2]
    step = _flat_step(grid)
    nb = _row_block_count(total_rows, math.prod(grid),
                          lambda rb: lead_rows % rb == 0 and rb % LANES == 0)
    rb = total_rows // nb
    blk = lambda *ids: jnp.minimum(step(*ids), nb - 1)
    src_row = lambda *ids: pl.multiple_of(
        blk(*ids) * rb + jnp.where(blk(*ids) * rb < lead_rows, 0, skip), 8)
    return _Cast(src,
                 pl.BlockSpec((None, pl.Element(rb), pl.Element(d)),
                              lambda *ids: (layer, src_row(*ids), 0)),
                 pl.BlockSpec((d, rb), lambda *ids: (0, blk(*ids))),
                 (d, total_rows),
                 lambda *ids: step(*ids) < nb,
                 transpose=True)


def _cast_specs(casts):
    shapes = [jax.ShapeDtypeStruct(c.dst_shape, BF16) for c in casts]
    return [c.src_spec for c in casts], [c.dst_spec for c in casts], shapes


def _run_casts(lives, src_refs, dst_refs, n_axes):
    ids = [pl.program_id(a) for a in range(n_axes)]
    for (live, transpose), src, dst in zip(lives, src_refs, dst_refs):
        @pl.when(live(*ids))
        def _():
            blk = src[...]
            dst[...] = (blk.T if transpose else blk).astype(BF16)


def _ffn_step(x_ref, g_ref, o_ref, xn_ref, wg, wu, wd):
    @pl.when(pl.program_id(1) == 0)
    def _():
        x = x_ref[...]
        xn_ref[...] = _rms(x, g_ref[...]).astype(BF16)
        o_ref[...] = x

    xn = xn_ref[...]
    gate = _dot(xn, wg())
    up = _dot(xn, wu())
    hid = (0.5 * (gate * jax.nn.sigmoid(gate)) * up).astype(BF16)
    o_ref[...] += _dot(hid, wd())


def _ffn_kernel(*refs, lives):
    nc = len(lives)
    x_ref, g_ref, wg_ref, wu_ref, wd_ref = refs[:5]
    o_ref, xn_ref = refs[5 + nc], refs[-1]
    _ffn_step(x_ref, g_ref, o_ref, xn_ref,
              lambda: wg_ref[...], lambda: wu_ref[...], lambda: wd_ref[...])
    _run_casts(lives, refs[5:5 + nc], refs[6 + nc:6 + 2 * nc], 2)


def _ffn(x, g, wg, wu, wd, *, tm, tf, casts=(), first_tile=0):
    n, d = x.shape
    f = wg.shape[1]
    cast_in_specs, cast_out_specs, cast_shapes = _cast_specs(casts)
    return pl.pallas_call(
        functools.partial(_ffn_kernel, lives=tuple((c.live, c.transpose) for c in casts)),
        grid=(n // tm - first_tile, f // tf),
        in_specs=[
            pl.BlockSpec((tm, d), lambda i, j: (i + first_tile, 0)),
            pl.BlockSpec((1, d), lambda i, j: (0, 0)),
            pl.BlockSpec((d, tf), lambda i, j: (0, j)),
            pl.BlockSpec((d, tf), lambda i, j: (0, j)),
            pl.BlockSpec((tf, d), lambda i, j: (j, 0)),
        ] + cast_in_specs,
        out_specs=[pl.BlockSpec((tm, d), lambda i, j: (i, 0))] + cast_out_specs,
        out_shape=[jax.ShapeDtypeStruct((n - first_tile * tm, d), F32)] + cast_shapes,
        scratch_shapes=[pltpu.VMEM((tm, d), BF16)],
        compiler_params=_params("parallel", "arbitrary"),
        name="ffn",
    )(x, g, wg, wu, wd, *[c.src for c in casts])


def _ffn_first_kernel(x_ref, g_ref, wg_ref, wu_ref, wd_ref, o_ref, wgb_ref, wub_ref, wdb_ref,
                      xn_ref):
    def cast(src, dst):
        def thunk():
            w = src[...].astype(BF16)
            dst[...] = w
            return w
        return thunk

    _ffn_step(x_ref, g_ref, o_ref, xn_ref, cast(wg_ref, wgb_ref), cast(wu_ref, wub_ref),
              cast(wd_ref, wdb_ref))


def _ffn_first(x, g, wg, wu, wd, *, tm, tf):
    n, d = x.shape
    f = wg.shape[1]
    w_specs = [pl.BlockSpec((d, tf), lambda i, j: (0, j)), pl.BlockSpec((d, tf), lambda i, j: (0, j)),
               pl.BlockSpec((tf, d), lambda i, j: (j, 0))]
    return pl.pallas_call(
        _ffn_first_kernel,
        grid=(1, f // tf),
        in_specs=[pl.BlockSpec((tm, d), lambda i, j: (0, 0)),
                  pl.BlockSpec((1, d), lambda i, j: (0, 0))] + w_specs,
        out_specs=[pl.BlockSpec((tm, d), lambda i, j: (0, 0))] + w_specs,
        out_shape=[jax.ShapeDtypeStruct((tm, d), F32)]
        + [jax.ShapeDtypeStruct(w.shape, BF16) for w in (wg, wu, wd)],
        scratch_shapes=[pltpu.VMEM((tm, d), BF16)],
        compiler_params=_params("arbitrary", "arbitrary"),
        name="ffn_first",
    )(x, g, wg, wu, wd)


class _ProjOut(NamedTuple):
    first: int
    count: int
    scale: float
    dtype: type


def _dot_nt(a, b):
    return lax.dot_general(a, b, (((1,), (1,)), ((), ())), preferred_element_type=F32)


def _row_parts_specs(parts, tm, **kw):
    d = parts[0].shape[1]
    nf = parts[0].shape[0] // tm
    specs = [pl.BlockSpec((tm, d), lambda i, *_: (jnp.minimum(i, nf - 1), 0), **kw)]
    if len(parts) == 2:
        specs.append(pl.BlockSpec((tm, d), lambda i, *_: (jnp.maximum(i - nf, 0), 0)))
    return specs, nf


def _for_row_part(part_refs, nf, fn):
    if len(part_refs) == 1:
        return fn(part_refs[0])
    i = pl.program_id(0)
    pl.when(i < nf)(lambda: fn(part_refs[0]))
    pl.when(i >= nf)(lambda: fn(part_refs[1]))


def _proj_kernel(*refs, outs, tn, n_fl, rank, n_parts, nf):
    h_refs = refs[:n_parts]
    g_ref, w_ref, wfl_ref, wgd_ref = refs[n_parts:n_parts + 4]
    out_refs = refs[n_parts + 4:n_parts + 4 + len(outs)]
    small_ref = refs[n_parts + 4 + len(outs)]

    def body(h_ref):
        un = _rms(h_ref[...], g_ref[...]).astype(BF16)
        small_ref[...] = jnp.zeros(small_ref.shape, F32)
        small_ref[:, :n_fl] = _dot_nt(un, wfl_ref[...])
        small_ref[:, n_fl:n_fl + rank] = _dot_nt(un, wgd_ref[...])
        for o_ref, o in zip(out_refs, outs):
            for t in range(o.count):
                c0 = (o.first + t) * tn
                tile = _dot(un, w_ref[:, c0:c0 + tn])
                if o.scale != 1.0:
                    tile = tile * o.scale
                o_ref[t] = tile.astype(o_ref.dtype)

    _for_row_part(h_refs, nf, body)


def _proj(h_parts, g, w_main, w_in_t, layer, *, tm, tn, outs, fl_rows, gd_rows):
    n, d = sum(a.shape[0] for a in h_parts), h_parts[0].shape[1]
    assert fl_rows[1] + gd_rows[1] <= LANES
    once = dict(pipeline_mode=pl.Buffered(1))
    h_specs, nf = _row_parts_specs(h_parts, tm, **(once if len(h_parts) == 2 else {}))
    kern = functools.partial(_proj_kernel, outs=outs, tn=tn, n_fl=fl_rows[1], rank=gd_rows[1],
                             n_parts=len(h_parts), nf=nf)
    rows = lambda start, size: pl.BlockSpec(
        (None, pl.Element(size), pl.Element(d)), lambda i: (layer, start, 0), **once)
    return pl.pallas_call(
        kern,
        grid=(n // tm,),
        in_specs=h_specs + [
            pl.BlockSpec((1, d), lambda i: (0, 0)),
            pl.BlockSpec(w_main.shape, lambda i: (0, 0), **once),
            rows(*fl_rows),
            rows(*gd_rows),
        ],
        out_specs=[pl.BlockSpec((o.count, tm, tn), lambda i: (0, i, 0)) for o in outs]
        + [pl.BlockSpec((tm, LANES), lambda i: (i, 0))],
        out_shape=[jax.ShapeDtypeStruct((o.count, n, tn), o.dtype) for o in outs]
        + [jax.ShapeDtypeStruct((n, LANES), F32)],
        compiler_params=_params("parallel"),
        name="proj",
    )(*h_parts, g, w_main, w_in_t, w_in_t)


FOX_AUG_LANES = 8


def _split3(x):
    p0 = x.astype(BF16)
    r = x - p0.astype(F32)
    p1 = r.astype(BF16)
    return p0, p1, (r - p1.astype(F32)).astype(BF16)


def _fcum_kernel(small_ref, bias_ref, ka_ref, qa_ref, f_ref, *, n_heads, scale):
    s = small_ref.shape[0]
    f_ref[...] = _log_sigmoid(small_ref[...] + bias_ref[...])
    row = lax.broadcasted_iota(jnp.int32, (LANES, LANES), 0)
    col = lax.broadcasted_iota(jnp.int32, (LANES, LANES), 1)
    tril = (row >= col).astype(F32)

    carry = jnp.zeros((1, LANES), F32)
    for r in range(s // LANES):
        rows = slice(r * LANES, (r + 1) * LANES)
        c = jnp.dot(tril, f_ref[rows, :], precision=lax.Precision.HIGHEST,
                    preferred_element_type=F32)
        f_ref[rows, :] = c + carry
        carry = carry + c[LANES - 1:LANES, :]

    route = ((col // FOX_AUG_LANES == row) & (col % FOX_AUG_LANES < 6)
             & (row < n_heads)).astype(BF16)
    routed = sum(_dot(piece, route) for piece in _split3(f_ref[...] * scale))
    p0, p1, p2 = (p.astype(F32) for p in _split3(routed))
    lane = lax.broadcasted_iota(jnp.int32, (s, LANES), 1)
    j = lane % FOX_AUG_LANES
    live = lane < n_heads * FOX_AUG_LANES
    piece = jnp.where(j % 3 == 0, p0, jnp.where(j % 3 == 1, p1, p2))
    ka_ref[0] = jnp.where(live & (j < 3), -piece,
                          jnp.where(live & (j < 6), 1.0, 0.0)).astype(BF16)
    qa_ref[0] = jnp.where(live & (j < 3), 1.0,
                          jnp.where(live & (j < 6), piece, 0.0)).astype(BF16)


def _fcum(small, bias_row, *, batch, seq, n_heads, head_dim):
    assert n_heads * FOX_AUG_LANES <= LANES
    blk = pl.BlockSpec((1, seq, LANES), lambda b: (b, 0, 0))
    return pl.pallas_call(
        functools.partial(_fcum_kernel, n_heads=n_heads, scale=head_dim ** 0.5),
        grid=(batch,),
        in_specs=[
            pl.BlockSpec((seq, LANES), lambda b: (b, 0)),
            pl.BlockSpec((1, LANES), lambda b: (0, 0)),
        ],
        out_specs=[blk, blk],
        out_shape=[jax.ShapeDtypeStruct((batch, seq, LANES), BF16)] * 2,
        scratch_shapes=[pltpu.VMEM((seq, LANES), F32)],
        compiler_params=_params("parallel"),
        name="fcum",
    )(small, bias_row)


def _fox_kernel(*refs, lives, t, c2, hp, dh):
    nc = len(lives)
    q_ref, k_ref, v_ref, ka_ref, qa_ref = refs[:5]
    o_ref = refs[5 + nc]
    (vt_ref, qaug_ref, m_ref, l_ref, acc_ref, s0_ref, s1_ref, mc0_ref, mc1_ref) = refs[6 + 2 * nc:]
    _run_casts(lives, refs[5:5 + nc], refs[6 + nc:6 + 2 * nc], 3)
    g = pl.program_id(1)
    i = pl.program_id(2)
    s_refs = (s0_ref, s1_ref)
    mc_refs = (mc0_ref, mc1_ref)

    @pl.when(i == 0)
    def _():
        for h in range(hp):
            vt_ref[h] = v_ref[0, :, h * dh:(h + 1) * dh].astype(F32).T.astype(BF16)

    q0 = pl.multiple_of(i * t, t)
    m_ref[...] = jnp.full(m_ref.shape, -jnp.inf, F32)
    l_ref[...] = jnp.zeros(l_ref.shape, F32)
    acc_ref[...] = jnp.zeros(acc_ref.shape, F32)

    qa_row = qa_ref[0, pl.ds(q0, 16), :].astype(F32)[0:1, :]
    lane = lax.broadcasted_iota(jnp.int32, (1, LANES), 1)
    for h in range(hp):
        mine = lane // FOX_AUG_LANES == g * hp + h
        qaug_ref[h, :, :dh] = q_ref[0, :, h * dh:(h + 1) * dh]
        qaug_ref[h, :, dh:] = jnp.broadcast_to(
            jnp.where(mine, qa_row, 0.0), (t, LANES)).astype(BF16)

    def scores(kk, slot, masked, h):
        k0 = pl.multiple_of(kk * t, t)
        hs = slice(h * dh, (h + 1) * dh)
        k_aug = jnp.concatenate([k_ref[0, pl.ds(k0, t), hs], ka_ref[0, pl.ds(k0, t), :]], axis=1)
        s = lax.dot_general(k_aug, qaug_ref[h], (((1,), (1,)), ((), ())),
                            preferred_element_type=F32) * c2
        if masked:
            key = lax.broadcasted_iota(jnp.int32, (t, t), 0)
            qry = lax.broadcasted_iota(jnp.int32, (t, t), 1)
            s = jnp.where(key <= qry, s, -jnp.inf)
        s_refs[slot][h] = s
        mc_refs[slot][h] = jnp.max(s, axis=0, keepdims=True)

    def absorb(kk, slot, h):
        k0 = pl.multiple_of(kk * t, t)
        m_old = m_ref[h]
        m_new = jnp.maximum(m_old, mc_refs[slot][h])
        alpha = jnp.exp2(m_old - m_new)
        p = jnp.exp2(s_refs[slot][h] - m_new)
        l_ref[h] = alpha * l_ref[h] + jnp.sum(p, axis=0, keepdims=True)
        acc_ref[h] = alpha * acc_ref[h] + _dot(vt_ref[h, :, pl.ds(k0, t)], p.astype(BF16))
        m_ref[h] = m_new

    for h in range(hp):
        scores(i, 0, True, h)

    def body(kk, carry):
        prev = jnp.where(kk == 0, i, kk - 1)
        for par in range(2):
            @pl.when(kk % 2 == par)
            def _():
                for h in range(hp):
                    scores(kk, 1 - par, False, h)
                    absorb(prev, par, h)
        return carry

    lax.fori_loop(0, i, body, 0)
    last = jnp.where(i == 0, i, i - 1)
    for par in range(2):
        @pl.when(i % 2 == par)
        def _():
            for h in range(hp):
                absorb(last, par, h)
    for h in range(hp):
        o_ref[0, :, h * dh:(h + 1) * dh] = (acc_ref[h] / l_ref[h]).T.astype(o_ref.dtype)


def _fox_grid(batch, seq, n_heads, t, hp):
    return (batch, n_heads // hp, seq // t)


def _fox(qkv, ka, qa, *, batch, seq, n_heads, head_dim, t, hp, casts=()):
    assert head_dim == LANES and n_heads % hp == 0
    c2 = (head_dim ** -0.5) * LOG2E
    ng = n_heads // hp
    w = hp * head_dim
    assert qkv.shape == (3 * ng, batch, seq, w)
    cast_in_specs, cast_out_specs, cast_shapes = _cast_specs(casts)
    return pl.pallas_call(
        functools.partial(_fox_kernel, lives=tuple((c.live, c.transpose) for c in casts), t=t, c2=c2, hp=hp,
                          dh=head_dim),
        grid=_fox_grid(batch, seq, n_heads, t, hp),
        in_specs=[
            pl.BlockSpec((None, 1, t, w), lambda b, g, i: (g, b, i, 0)),
            pl.BlockSpec((None, 1, seq, w), lambda b, g, i: (ng + g, b, 0, 0)),
            pl.BlockSpec((None, 1, seq, w), lambda b, g, i: (2 * ng + g, b, 0, 0)),
            pl.BlockSpec((1, seq, LANES), lambda b, g, i: (b, 0, 0)),
            pl.BlockSpec((1, seq, LANES), lambda b, g, i: (b, 0, 0)),
        ] + cast_in_specs,
        out_specs=[pl.BlockSpec((1, t, w), lambda b, g, i: (b, i, g))] + cast_out_specs,
        out_shape=[jax.ShapeDtypeStruct((batch, seq, n_heads * head_dim), BF16)] + cast_shapes,
        scratch_shapes=[pltpu.VMEM((hp, head_dim, seq), BF16),
                        pltpu.VMEM((hp, t, head_dim + LANES), BF16),
                        pltpu.VMEM((hp, 1, t), F32), pltpu.VMEM((hp, 1, t), F32),
                        pltpu.VMEM((hp, head_dim, t), F32),
                        pltpu.VMEM((hp, t, t), F32), pltpu.VMEM((hp, t, t), F32),
                        pltpu.VMEM((hp, 1, t), F32), pltpu.VMEM((hp, 1, t), F32)],
        compiler_params=_params("parallel", "arbitrary", "arbitrary"),
        name="fox",
    )(qkv, qkv, qkv, ka, qa, *[c.src for c in casts])


def _gla_kernel(*refs, lives, n_heads, dk, dv):
    nc = len(lives)
    q_ref, k_ref, v_ref, r_ref, small_ref, wgu_ref, gb_ref, gain_ref = refs[:8]
    o_ref = refs[8 + nc]
    st_ref, la_ref, kd_ref, ea_ref = refs[9 + 2 * nc:]
    _run_casts(lives, refs[8:8 + nc], refs[9 + nc:9 + 2 * nc], 1)
    nb, tb, tn = q_ref.shape[1:]
    n_chunks = tb // CHUNK

    def cols(ref, b, rows, start, size):
        return ref[start // tn, b, rows, start % tn:start % tn + size]

    @pl.when(pl.program_id(0) == 0)
    def _():
        st_ref[...] = jnp.zeros(st_ref.shape, F32)

    for b in range(nb):
        pre = _dot(small_ref[b].astype(BF16), wgu_ref[...]) + gb_ref[...]
        la_ref[b] = _log_sigmoid(pre) * (1.0 / GLA_GATE_TAU)

    row = lax.broadcasted_iota(jnp.int32, (CHUNK, CHUNK), 0)
    col = lax.broadcasted_iota(jnp.int32, (CHUNK, CHUNK), 1)
    after = (col > row).astype(F32)

    def prep(c, carry):
        rows = pl.ds(pl.multiple_of(c * CHUNK, CHUNK), CHUNK)
        for b in range(nb):
            la = la_ref[b, rows, :]
            rev = jnp.dot(after, la, precision=lax.Precision.HIGHEST, preferred_element_type=F32)
            for tt in range(k_ref.shape[0]):
                ts = slice(tt * tn, (tt + 1) * tn)
                kd_ref[b, rows, ts] = (k_ref[tt, b, rows, :] * jnp.exp(rev[:, ts])).astype(BF16)
            ea_ref[c, b:b + 1, :] = jnp.exp(jnp.sum(la, axis=0, keepdims=True))
        return carry

    lax.fori_loop(0, n_chunks, prep, 0)

    def step(c, carry):
        rows = pl.ds(pl.multiple_of(c * CHUNK, CHUNK), CHUNK)
        ea = ea_ref[c]
        for b in range(nb):
            for h in range(n_heads):
                ks = slice(h * dk, (h + 1) * dk)
                vs = slice(h * dv, (h + 1) * dv)
                kv_t = lax.dot_general(cols(v_ref, b, rows, h * dv, dv), kd_ref[b, rows, ks],
                                       (((0,), (0,)), ((), ())), preferred_element_type=F32)
                st = st_ref[b * n_heads + h] * ea[b:b + 1, ks] + kv_t
                st_ref[b * n_heads + h] = st
                o = lax.dot_general(cols(q_ref, b, rows, h * dk, dk), st.astype(BF16),
                                    (((1,), (1,)), ((), ())), preferred_element_type=F32)
                on = _rms(o, gain_ref[...])
                r = cols(r_ref, b, rows, h * dv, dv)
                o_ref[b, rows, vs] = (on * (r * jax.nn.sigmoid(r))).astype(o_ref.dtype)
        return carry

    lax.fori_loop(0, n_chunks, step, 0)


def _gla(q, k, v, r, small, wgu, gate_bias, gain, *, batch, seq, n_heads, dk, dv, tb, casts=()):
    kw, vw = n_heads * dk, n_heads * dv
    tn = q.shape[-1]
    assert tn % dk == 0 and tn % dv == 0
    tiles = lambda a: pl.BlockSpec((a.shape[0], batch, tb, tn), lambda t: (0, 0, t, 0))
    cast_in_specs, cast_out_specs, cast_shapes = _cast_specs(casts)
    return pl.pallas_call(
        functools.partial(_gla_kernel, lives=tuple((c.live, c.transpose) for c in casts), n_heads=n_heads,
                          dk=dk, dv=dv),
        grid=(seq // tb,),
        in_specs=[
            tiles(q), tiles(k), tiles(v), tiles(r),
            pl.BlockSpec((batch, tb, LANES), lambda t: (0, t, 0)),
            pl.BlockSpec((LANES, kw), lambda t: (0, 0)),
            pl.BlockSpec((1, kw), lambda t: (0, 0)),
            pl.BlockSpec((1, dv), lambda t: (0, 0)),
        ] + cast_in_specs,
        out_specs=[pl.BlockSpec((batch, tb, vw), lambda t: (0, t, 0))] + cast_out_specs,
        out_shape=[jax.ShapeDtypeStruct((batch, seq, vw), BF16)] + cast_shapes,
        scratch_shapes=[
            pltpu.VMEM((batch * n_heads, dv, dk), F32),
            pltpu.VMEM((batch, tb, kw), F32),
            pltpu.VMEM((batch, tb, kw), BF16),
            pltpu.VMEM((tb // CHUNK, batch, kw), F32),
        ],
        compiler_params=_params("arbitrary"),
        name="gla",
    )(q, k, v, r, small, wgu, gate_bias, gain, *[c.src for c in casts])


def _merge_kernel(*refs, lives, n_parts, nf, tn):
    nc = len(lives)
    h_refs = refs[:n_parts]
    g_ref, yf_ref, yg_ref, wg_ref, bg_ref, wa_ref, wb_ref, wo_ref = refs[n_parts:n_parts + 8]
    n_in = n_parts + 8
    o_ref = refs[n_in + nc]
    _run_casts(lives, refs[n_in:n_in + nc], refs[n_in + nc + 1:], 1)
    d = o_ref.shape[1]

    def body(h_ref):
        h = h_ref[...]
        un = _rms(h, g_ref[...]).astype(BF16)
        yf, yg = yf_ref[...], yg_ref[...]
        acc = h
        for c0 in range(0, d, tn):
            cs, gs = slice(c0, c0 + tn), slice(d + c0, d + c0 + tn)
            g_fox = jax.nn.sigmoid(_dot(un, wg_ref[:, cs]) + bg_ref[:, cs])
            g_gla = jax.nn.sigmoid(_dot(un, wg_ref[:, gs]) + bg_ref[:, gs])
            merged = g_fox * _dot(yf, wa_ref[:, cs]) + g_gla * _dot(yg, wb_ref[:, cs])
            acc = acc + _dot(merged.astype(BF16), wo_ref[cs, :])
        o_ref[...] = acc

    _for_row_part(h_refs, nf, body)


def _merge(h_parts, g, y_fox, y_gla, w_gate, b_gate, w_a, w_b, w_o, *, tm, tn, casts=()):
    n, d = sum(a.shape[0] for a in h_parts), h_parts[0].shape[1]
    once = dict(pipeline_mode=pl.Buffered(1))
    h_specs, nf = _row_parts_specs(h_parts, tm, **(once if len(h_parts) == 2 else {}))
    whole = lambda a: pl.BlockSpec(a.shape, lambda i: (0, 0), **once)
    cast_in_specs, cast_out_specs, cast_shapes = _cast_specs(casts)
    return pl.pallas_call(
        functools.partial(_merge_kernel, lives=tuple((c.live, c.transpose) for c in casts),
                          n_parts=len(h_parts), nf=nf, tn=tn),
        grid=(n // tm,),
        in_specs=h_specs + [
            pl.BlockSpec((1, d), lambda i: (0, 0)),
            pl.BlockSpec((tm, y_fox.shape[1]), lambda i: (i, 0)),
            pl.BlockSpec((tm, y_gla.shape[1]), lambda i: (i, 0)),
            whole(w_gate), pl.BlockSpec((1, 2 * d), lambda i: (0, 0)),
            whole(w_a), whole(w_b), whole(w_o),
        ] + cast_in_specs,
        out_specs=[pl.BlockSpec((tm, d), lambda i: (i, 0))] + cast_out_specs,
        out_shape=[jax.ShapeDtypeStruct((n, d), F32)] + cast_shapes,
        compiler_params=_params("parallel"),
        name="merge",
    )(*h_parts, g, y_fox, y_gla, w_gate, b_gate, w_a, w_b, w_o, *[c.src for c in casts])


def _ple_kernel(h_ref, p_ref, gp_ref, gf_ref, wpg_ref, wpe_ref, o_ref, *, final):
    h = h_ref[...]
    hn = _rms(h, gp_ref[...]).astype(BF16)
    gate = jax.nn.sigmoid(_dot(hn, wpg_ref[...]))
    out = h + gate * _dot(p_ref[...].astype(BF16), wpe_ref[...])
    if final:
        out = _rms(out, gf_ref[...])
    o_ref[...] = out


def _ple(h, p, g_ple, g_final, w_gate, w_proj, *, tm, final):
    n, d = h.shape
    dp = p.shape[1]
    return pl.pallas_call(
        functools.partial(_ple_kernel, final=final),
        grid=(n // tm,),
        in_specs=[
            pl.BlockSpec((tm, d), lambda i: (i, 0)),
            pl.BlockSpec((tm, dp), lambda i: (i, 0)),
            pl.BlockSpec((1, d), lambda i: (0, 0)),
            pl.BlockSpec((1, d), lambda i: (0, 0)),
            pl.BlockSpec((d, d), lambda i: (0, 0)),
            pl.BlockSpec((dp, d), lambda i: (0, 0)),
        ],
        out_specs=pl.BlockSpec((tm, d), lambda i: (i, 0)),
        out_shape=jax.ShapeDtypeStruct((n, d), F32),
        compiler_params=_params("parallel"),
        name="ple",
    )(h, p, g_ple, g_final, w_gate, w_proj)


def _tile(n, want):
    t = min(n, want)
    assert n % t == 0, (n, want)
    return t


def kernel(x, p, ffn1_norm, ffn1_w_gate, ffn1_w_up, ffn1_w_down, mix_norm, w_in, fox_forget_bias, gla_gate_up, gla_gate_bias, gla_head_norm, w_branch_fox, w_branch_gla, w_merge_gate, b_merge_gate, w_out, ffn2_norm, ffn2_w_gate, ffn2_w_up, ffn2_w_down, ple_norm, w_ple_proj, w_ple_gate, final_norm):
    batch, seq, d = x.shape
    depth = p.shape[0]
    n = batch * seq
    fox_heads = fox_forget_bias.shape[-1]
    fox_dim = w_branch_fox.shape[1] // fox_heads
    fox_w = fox_heads * fox_dim
    rank, gla_kw = gla_gate_up.shape[1:]
    gla_dv = gla_head_norm.shape[-1]
    gla_vw = w_branch_gla.shape[1]
    gla_heads = gla_vw // gla_dv
    gla_dk = gla_kw // gla_heads
    assert fox_heads + rank <= LANES

    tn = 512
    assert w_in.shape[-1] == 3 * fox_w + fox_heads + 2 * gla_kw + 2 * gla_vw + rank
    assert (3 * fox_w) % tn == 0 and gla_kw % tn == 0 and gla_vw % tn == 0
    assert fox_heads % 8 == 0 and rank % 8 == 0
    lead = 3 * fox_w // tn
    kt, vt = gla_kw // tn, gla_vw // tn
    proj_outs = (_ProjOut(0, lead, 1.0, BF16),
                 _ProjOut(lead, kt, gla_dk ** -0.5, BF16),
                 _ProjOut(lead + kt, kt, 1.0, F32),
                 _ProjOut(lead + 2 * kt, vt, 1.0, BF16),
                 _ProjOut(lead + 2 * kt + vt, vt, 1.0, F32))
    proj_rows = (lead + 2 * kt + 2 * vt) * tn
    fl_rows = (3 * fox_w, fox_heads)
    gd_rows = (w_in.shape[-1] - rank, rank)

    row = lambda v: v.reshape(1, -1).astype(F32)
    h = x.reshape(n, d)
    tm_ffn = _tile(n, 1024)
    tf = _tile(ffn1_w_gate.shape[-1], 512)
    tm_proj = _tile(n, 512)
    tm_merge = _tile(n, 256)
    tm_ple = _tile(n, 512)
    t_fox = _tile(seq, 512)
    fox_hp = 4
    tb_gla = _tile(seq, 256)

    tf_first = _tile(ffn1_w_gate.shape[-1], 256)
    ffn_rest_grid = (n // tm_ffn - 1, ffn1_w_gate.shape[-1] // tf)
    gla_grid = (seq // tb_gla,)

    for i in range(depth):
        fbias = jnp.zeros((1, LANES), F32).at[0, :fox_heads].set(fox_forget_bias[i])
        wgu = jnp.zeros((LANES, gla_kw), F32).at[fox_heads:fox_heads + rank].set(gla_gate_up[i]).astype(BF16)

        w_in_t = jnp.swapaxes(w_in, 1, 2)

        h1, w1g, w1u, w1d = _ffn_first(h, row(ffn1_norm[i]), ffn1_w_gate[i], ffn1_w_up[i],
                                       ffn1_w_down[i], tm=tm_ffn, tf=tf_first)
        if ffn_rest_grid[0]:
            h_rest, w_proj = _ffn(h, row(ffn1_norm[i]), w1g, w1u, w1d, tm=tm_ffn, tf=tf,
                                  first_tile=1, casts=(_cast_compact_rows_t(
                                      w_in_t, i, ffn_rest_grid, 3 * fox_w, fox_heads, proj_rows),))
            h_parts = (h1, h_rest)
        else:
            h_parts, w_proj = (h1,), jnp.concatenate(
                [w_in[i][:, :3 * fox_w], w_in[i][:, 3 * fox_w + fox_heads:][:, :proj_rows - 3 * fox_w]],
                axis=1).astype(BF16)

        tiled = lambda a: a.reshape(a.shape[0], batch, seq, a.shape[-1])
        fox_qkv, gq, gk, gv, gr, small = _proj(
            h_parts, row(mix_norm[i]), w_proj, w_in_t, i, tm=tm_proj, tn=tn, outs=proj_outs,
            fl_rows=fl_rows, gd_rows=gd_rows)

        fox_ka, fox_qa = _fcum(small, fbias, batch=batch, seq=seq, n_heads=fox_heads,
                               head_dim=fox_dim)
        y_fox, = _fox(tiled(fox_qkv), fox_ka, fox_qa, batch=batch, seq=seq, n_heads=fox_heads,
                      head_dim=fox_dim, t=t_fox, hp=fox_hp)

        y_gla, wmg, wbf, wbg, wo, w2d = _gla(
            tiled(gq), tiled(gk), tiled(gv), tiled(gr), small.reshape(batch, seq, LANES), wgu,
            row(gla_gate_bias[i]),
            row(gla_head_norm[i]), batch=batch, seq=seq, n_heads=gla_heads, dk=gla_dk, dv=gla_dv,
            tb=tb_gla,
            casts=tuple(_cast_row_blocks(w, gla_grid) for w in (
                w_merge_gate[i], w_branch_fox[i], w_branch_gla[i], w_out[i], ffn2_w_down[i])))

        h, w2g, w2u, wpg, wpe = _merge(
            h_parts, row(mix_norm[i]), y_fox.reshape(n, fox_w), y_gla.reshape(n, gla_vw),
            wmg, row(b_merge_gate[i]), wbf, wbg, wo, tm=tm_merge, tn=tn,
            casts=tuple(_cast_row_blocks(w, (n // tm_merge,)) for w in (
                ffn2_w_gate[i], ffn2_w_up[i], w_ple_gate[i], w_ple_proj[i])))

        h, = _ffn(h, row(ffn2_norm[i]), w2g, w2u, w2d, tm=tm_ffn, tf=tf)

        h = _ple(h, p[i].reshape(n, -1), row(ple_norm[i]), row(final_norm), wpg, wpe,
                 tm=tm_ple, final=(i == depth - 1))

    return h.reshape(batch, seq, d)
```

```python
import functools
import math
from typing import Callable, NamedTuple

import jax
import jax.numpy as jnp
from jax import lax
from jax.experimental import pallas as pl
from jax.experimental.pallas import tpu as pltpu

EPS = 1e-6
CHUNK = 64
GLA_GATE_TAU = 16.0
LANES = 128
V7X_VMEM_LIMIT_BYTES = 62 * 1024 * 1024
LOG2E = math.log2(math.e)

F32 = jnp.float32
BF16 = jnp.bfloat16


def _rms(x, g):
    return x * lax.rsqrt(jnp.mean(x * x, axis=-1, keepdims=True) + EPS) * g


def _log_sigmoid(z):
    return jnp.minimum(z, 0.0) - jnp.log1p(jnp.exp(-jnp.abs(z)))


def _dot(a, b):
    return lax.dot_general(a, b, (((1,), (0,)), ((), ())), preferred_element_type=F32)


def _params(*sem):
    return pltpu.CompilerParams(dimension_semantics=sem,
                                vmem_limit_bytes=V7X_VMEM_LIMIT_BYTES)


class _Cast(NamedTuple):
    src: jax.Array
    src_spec: pl.BlockSpec
    dst_spec: pl.BlockSpec
    dst_shape: tuple
    live: Callable
    transpose: bool = False


def _flat_step(grid):
    def step(*ids):
        s = ids[0]
        for size, idx in zip(grid[1:], ids[1:]):
            s = s * size + idx
        return s
    return step


def _row_block_count(rows, n_steps, extra=lambda rb: True):
    ok = lambda k: rows % k == 0 and (rows // k) % 16 == 0 and extra(rows // k)
    return max(k for k in range(1, n_steps + 1) if ok(k))


def _cast_row_blocks(src, grid):
    r, c = src.shape
    step = _flat_step(grid)
    nb = _row_block_count(r, math.prod(grid))
    spec = pl.BlockSpec((r // nb, c), lambda *ids: (jnp.minimum(step(*ids), nb - 1), 0))
    return _Cast(src, spec, spec, src.shape, lambda *ids: step(*ids) < nb)


def _cast_compact_rows_t(src, layer, grid, lead_rows, skip, total_rows):
    d = src.shape[2]
    step = _flat_step(grid)
    nb = _row_block_count(total_rows, math.prod(grid),
                          lambda rb: lead_rows % rb == 0 and rb % LANES == 0)
    rb = total_rows // nb
    blk = lambda *ids: jnp.minimum(step(*ids), nb - 1)
    src_row = lambda *ids: pl.multiple_of(
        blk(*ids) * rb + jnp.where(blk(*ids) * rb < lead_rows, 0, skip), 8)
    return _Cast(src,
                 pl.BlockSpec((None, pl.Element(rb), pl.Element(d)),
                              lambda *ids: (layer, src_row(*ids), 0)),
                 pl.BlockSpec((d, rb), lambda *ids: (0, blk(*ids))),
                 (d, total_rows),
                 lambda *ids: step(*ids) < nb,
                 transpose=True)


def _cast_specs(casts):
    shapes = [jax.ShapeDtypeStruct(c.dst_shape, BF16) for c in casts]
    return [c.src_spec for c in casts], [c.dst_spec for c in casts], shapes


def _run_casts(lives, src_refs, dst_refs, n_axes):
    ids = [pl.program_id(a) for a in range(n_axes)]
    for (live, transpose), src, dst in zip(lives, src_refs, dst_refs):
        @pl.when(live(*ids))
        def _():
            blk = src[...]
            dst[...] = (blk.T if transpose else blk).astype(BF16)


def _ffn_step(x_ref, g_ref, o_ref, xn_ref, wg, wu, wd):
    @pl.when(pl.program_id(1) == 0)
    def _():
        x = x_ref[...]
        xn_ref[...] = _rms(x, g_ref[...]).astype(BF16)
        o_ref[...] = x

    xn = xn_ref[...]
    gate = _dot(xn, wg())
    up = _dot(xn, wu())
    hid = (0.5 * (gate * jax.nn.sigmoid(gate)) * up).astype(BF16)
    o_ref[...] += _dot(hid, wd())


def _ffn_kernel(*refs, lives):
    nc = len(lives)
    x_ref, g_ref, wg_ref, wu_ref, wd_ref = refs[:5]
    o_ref, xn_ref = refs[5 + nc], refs[-1]
    _ffn_step(x_ref, g_ref, o_ref, xn_ref,
              lambda: wg_ref[...], lambda: wu_ref[...], lambda: wd_ref[...])
    _run_casts(lives, refs[5:5 + nc], refs[6 + nc:6 + 2 * nc], 2)


def _ffn(x, g, wg, wu, wd, *, tm, tf, casts=(), first_tile=0):
    n, d = x.shape
    f = wg.shape[1]
    cast_in_specs, cast_out_specs, cast_shapes = _cast_specs(casts)
    return pl.pallas_call(
        functools.partial(_ffn_kernel, lives=tuple((c.live, c.transpose) for c in casts)),
        grid=(n // tm - first_tile, f // tf),
        in_specs=[
            pl.BlockSpec((tm, d), lambda i, j: (i + first_tile, 0)),
            pl.BlockSpec((1, d), lambda i, j: (0, 0)),
            pl.BlockSpec((d, tf), lambda i, j: (0, j)),
            pl.BlockSpec((d, tf), lambda i, j: (0, j)),
            pl.BlockSpec((tf, d), lambda i, j: (j, 0)),
        ] + cast_in_specs,
        out_specs=[pl.BlockSpec((tm, d), lambda i, j: (i, 0))] + cast_out_specs,
        out_shape=[jax.ShapeDtypeStruct((n - first_tile * tm, d), F32)] + cast_shapes,
        scratch_shapes=[pltpu.VMEM((tm, d), BF16)],
        compiler_params=_params("parallel", "arbitrary"),
        name="ffn",
    )(x, g, wg, wu, wd, *[c.src for c in casts])


def _ffn_first_kernel(x_ref, g_ref, wg_ref, wu_ref, wd_ref, o_ref, wgb_ref, wub_ref, wdb_ref,
                      xn_ref):
    def cast(src, dst):
        def thunk():
            w = src[...].astype(BF16)
            dst[...] = w
            return w
        return thunk

    _ffn_step(x_ref, g_ref, o_ref, xn_ref, cast(wg_ref, wgb_ref), cast(wu_ref, wub_ref),
              cast(wd_ref, wdb_ref))


def _ffn_first(x, g, wg, wu, wd, *, tm, tf):
    n, d = x.shape
    f = wg.shape[1]
    w_specs = [pl.BlockSpec((d, tf), lambda i, j: (0, j)), pl.BlockSpec((d, tf), lambda i, j: (0, j)),
               pl.BlockSpec((tf, d), lambda i, j: (j, 0))]
    return pl.pallas_call(
        _ffn_first_kernel,
        grid=(1, f // tf),
        in_specs=[pl.BlockSpec((tm, d), lambda i, j: (0, 0)),
                  pl.BlockSpec((1, d), lambda i, j: (0, 0))] + w_specs,
        out_specs=[pl.BlockSpec((tm, d), lambda i, j: (0, 0))] + w_specs,
        out_shape=[jax.ShapeDtypeStruct((tm, d), F32)]
        + [jax.ShapeDtypeStruct(w.shape, BF16) for w in (wg, wu, wd)],
        scratch_shapes=[pltpu.VMEM((tm, d), BF16)],
        compiler_params=_params("arbitrary", "arbitrary"),
        name="ffn_first",
    )(x, g, wg, wu, wd)


class _ProjOut(NamedTuple):
    first: int
    count: int
    scale: float
    dtype: type


def _dot_nt(a, b):
    return lax.dot_general(a, b, (((1,), (1,)), ((), ())), preferred_element_type=F32)


def _row_parts_specs(parts, tm, **kw):
    d = parts[0].shape[1]
    nf = parts[0].shape[0] // tm
    specs = [pl.BlockSpec((tm, d), lambda i, *_: (jnp.minimum(i, nf - 1), 0), **kw)]
    if len(parts) == 2:
        specs.append(pl.BlockSpec((tm, d), lambda i, *_: (jnp.maximum(i - nf, 0), 0)))
    return specs, nf


def _for_row_part(part_refs, nf, fn):
    if len(part_refs) == 1:
        return fn(part_refs[0])
    i = pl.program_id(0)
    pl.when(i < nf)(lambda: fn(part_refs[0]))
    pl.when(i >= nf)(lambda: fn(part_refs[1]))


def _proj_kernel(*refs, outs, tn, n_fl, rank, n_parts, nf):
    h_refs = refs[:n_parts]
    g_ref, w_ref, wfl_ref, wgd_ref = refs[n_parts:n_parts + 4]
    out_refs = refs[n_parts + 4:n_parts + 4 + len(outs)]
    small_ref = refs[n_parts + 4 + len(outs)]

    def body(h_ref):
        un = _rms(h_ref[...], g_ref[...]).astype(BF16)
        small_ref[...] = jnp.zeros(small_ref.shape, F32)
        small_ref[:, :n_fl] = _dot_nt(un, wfl_ref[...])
        small_ref[:, n_fl:n_fl + rank] = _dot_nt(un, wgd_ref[...])
        for o_ref, o in zip(out_refs, outs):
            for t in range(o.count):
                c0 = (o.first + t) * tn
                tile = _dot(un, w_ref[:, c0:c0 + tn])
                if o.scale != 1.0:
                    tile = tile * o.scale
                o_ref[t] = tile.astype(o_ref.dtype)

    _for_row_part(h_refs, nf, body)


def _proj(h_parts, g, w_main, w_in_t, layer, *, tm, tn, outs, fl_rows, gd_rows):
    n, d = sum(a.shape[0] for a in h_parts), h_parts[0].shape[1]
    assert fl_rows[1] + gd_rows[1] <= LANES
    once = dict(pipeline_mode=pl.Buffered(1))
    h_specs, nf = _row_parts_specs(h_parts, tm, **(once if len(h_parts) == 2 else {}))
    kern = functools.partial(_proj_kernel, outs=outs, tn=tn, n_fl=fl_rows[1], rank=gd_rows[1],
                             n_parts=len(h_parts), nf=nf)
    rows = lambda start, size: pl.BlockSpec(
        (None, pl.Element(size), pl.Element(d)), lambda i: (layer, start, 0), **once)
    return pl.pallas_call(
        kern,
        grid=(n // tm,),
        in_specs=h_specs + [
            pl.BlockSpec((1, d), lambda i: (0, 0)),
            pl.BlockSpec(w_main.shape, lambda i: (0, 0), **once),
            rows(*fl_rows),
            rows(*gd_rows),
        ],
        out_specs=[pl.BlockSpec((o.count, tm, tn), lambda i: (0, i, 0)) for o in outs]
        + [pl.BlockSpec((tm, LANES), lambda i: (i, 0))],
        out_shape=[jax.ShapeDtypeStruct((o.count, n, tn), o.dtype) for o in outs]
        + [jax.ShapeDtypeStruct((n, LANES), F32)],
        compiler_params=_params("parallel"),
        name="proj",
    )(*h_parts, g, w_main, w_in_t, w_in_t)


FOX_AUG_LANES = 8


def _split3(x):
    p0 = x.astype(BF16)
    r = x - p0.astype(F32)
    p1 = r.astype(BF16)
    return p0, p1, (r - p1.astype(F32)).astype(BF16)


def _fcum_kernel(small_ref, bias_ref, ka_ref, qa_ref, f_ref, *, n_heads, scale):
    s = small_ref.shape[0]
    f_ref[...] = _log_sigmoid(small_ref[...] + bias_ref[...])
    row = lax.broadcasted_iota(jnp.int32, (LANES, LANES), 0)
    col = lax.broadcasted_iota(jnp.int32, (LANES, LANES), 1)
    tril = (row >= col).astype(BF16)

    carry = jnp.zeros((1, LANES), F32)
    for r in range(s // LANES):
        rows = slice(r * LANES, (r + 1) * LANES)
        c = sum(_dot(tril, piece) for piece in _split3(f_ref[rows, :]))
        f_ref[rows, :] = c + carry
        carry = carry + c[LANES - 1:LANES, :]

    route = ((col // FOX_AUG_LANES == row) & (col % FOX_AUG_LANES < 6)
             & (row < n_heads)).astype(BF16)
    routed = sum(_dot(piece, route) for piece in _split3(f_ref[...] * scale))
    p0, p1, p2 = (p.astype(F32) for p in _split3(routed))
    lane = lax.broadcasted_iota(jnp.int32, (s, LANES), 1)
    j = lane % FOX_AUG_LANES
    live = lane < n_heads * FOX_AUG_LANES
    piece = jnp.where(j % 3 == 0, p0, jnp.where(j % 3 == 1, p1, p2))
    ka_ref[0] = jnp.where(live & (j < 3), -piece,
                          jnp.where(live & (j < 6), 1.0, 0.0)).astype(BF16)
    qa_ref[0] = jnp.where(live & (j < 3), 1.0,
                          jnp.where(live & (j < 6), piece, 0.0)).astype(BF16)


def _fcum(small, bias_row, *, batch, seq, n_heads, head_dim):
    assert n_heads * FOX_AUG_LANES <= LANES
    blk = pl.BlockSpec((1, seq, LANES), lambda b: (b, 0, 0))
    return pl.pallas_call(
        functools.partial(_fcum_kernel, n_heads=n_heads, scale=head_dim ** 0.5),
        grid=(batch,),
        in_specs=[
            pl.BlockSpec((seq, LANES), lambda b: (b, 0)),
            pl.BlockSpec((1, LANES), lambda b: (0, 0)),
        ],
        out_specs=[blk, blk],
        out_shape=[jax.ShapeDtypeStruct((batch, seq, LANES), BF16)] * 2,
        scratch_shapes=[pltpu.VMEM((seq, LANES), F32)],
        compiler_params=_params("parallel"),
        name="fcum",
    )(small, bias_row)


def _fox_kernel(*refs, lives, t, c2, hp, dh):
    nc = len(lives)
    q_ref, k_ref, v_ref, ka_ref, qa_ref = refs[:5]
    o_ref = refs[5 + nc]
    (vt_ref, qaug_ref, m_ref, l_ref, acc_ref, s0_ref, s1_ref, mc0_ref, mc1_ref) = refs[6 + 2 * nc:]
    _run_casts(lives, refs[5:5 + nc], refs[6 + nc:6 + 2 * nc], 3)
    g = pl.program_id(1)
    i = pl.program_id(2)
    s_refs = (s0_ref, s1_ref)
    mc_refs = (mc0_ref, mc1_ref)

    @pl.when(i == 0)
    def _():
        for h in range(hp):
            vt_ref[h] = v_ref[0, :, h * dh:(h + 1) * dh].astype(F32).T.astype(BF16)

    q0 = pl.multiple_of(i * t, t)
    m_ref[...] = jnp.full(m_ref.shape, -jnp.inf, F32)
    l_ref[...] = jnp.zeros(l_ref.shape, F32)
    acc_ref[...] = jnp.zeros(acc_ref.shape, F32)

    qa_row = qa_ref[0, pl.ds(q0, 16), :].astype(F32)[0:1, :]
    lane = lax.broadcasted_iota(jnp.int32, (1, LANES), 1)
    for h in range(hp):
        mine = lane // FOX_AUG_LANES == g * hp + h
        qaug_ref[h, :, :dh] = q_ref[0, :, h * dh:(h + 1) * dh]
        qaug_ref[h, :, dh:] = jnp.broadcast_to(
            jnp.where(mine, qa_row, 0.0), (t, LANES)).astype(BF16)

    def scores(kk, slot, masked, h):
        k0 = pl.multiple_of(kk * t, t)
        hs = slice(h * dh, (h + 1) * dh)
        k_aug = jnp.concatenate([k_ref[0, pl.ds(k0, t), hs], ka_ref[0, pl.ds(k0, t), :]], axis=1)
        s = lax.dot_general(k_aug, qaug_ref[h], (((1,), (1,)), ((), ())),
                            preferred_element_type=F32) * c2
        if masked:
            key = lax.broadcasted_iota(jnp.int32, (t, t), 0)
            qry = lax.broadcasted_iota(jnp.int32, (t, t), 1)
            s = jnp.where(key <= qry, s, -jnp.inf)
        s_refs[slot][h] = s
        mc_refs[slot][h] = jnp.max(s, axis=0, keepdims=True)

    def absorb(kk, slot, h):
        k0 = pl.multiple_of(kk * t, t)
        m_old = m_ref[h]
        m_new = jnp.maximum(m_old, mc_refs[slot][h])
        alpha = jnp.exp2(m_old - m_new)
        p = jnp.exp2(s_refs[slot][h] - m_new)
        l_ref[h] = alpha * l_ref[h] + jnp.sum(p, axis=0, keepdims=True)
        acc_ref[h] = alpha * acc_ref[h] + _dot(vt_ref[h, :, pl.ds(k0, t)], p.astype(BF16))
        m_ref[h] = m_new

    for h in range(hp):
        scores(i, 0, True, h)

    def body(kk, carry):
        prev = jnp.where(kk == 0, i, kk - 1)
        for par in range(2):
            @pl.when(kk % 2 == par)
            def _():
                for h in range(hp):
                    scores(kk, 1 - par, False, h)
                    absorb(prev, par, h)
        return carry

    lax.fori_loop(0, i, body, 0)
    last = jnp.where(i == 0, i, i - 1)
    for par in range(2):
        @pl.when(i % 2 == par)
        def _():
            for h in range(hp):
                absorb(last, par, h)
    for h in range(hp):
        o_ref[0, :, h * dh:(h + 1) * dh] = (acc_ref[h] / l_ref[h]).T.astype(o_ref.dtype)


def _fox_grid(batch, seq, n_heads, t, hp):
    return (batch, n_heads // hp, seq // t)


def _fox(qkv, ka, qa, *, batch, seq, n_heads, head_dim, t, hp, casts=()):
    assert head_dim == LANES and n_heads % hp == 0
    c2 = (head_dim ** -0.5) * LOG2E
    ng = n_heads // hp
    w = hp * head_dim
    assert qkv.shape == (3 * ng, batch, seq, w)
    cast_in_specs, cast_out_specs, cast_shapes = _cast_specs(casts)
    return pl.pallas_call(
        functools.partial(_fox_kernel, lives=tuple((c.live, c.transpose) for c in casts), t=t, c2=c2, hp=hp,
                          dh=head_dim),
        grid=_fox_grid(batch, seq, n_heads, t, hp),
        in_specs=[
            pl.BlockSpec((None, 1, t, w), lambda b, g, i: (g, b, i, 0)),
            pl.BlockSpec((None, 1, seq, w), lambda b, g, i: (ng + g, b, 0, 0)),
            pl.BlockSpec((None, 1, seq, w), lambda b, g, i: (2 * ng + g, b, 0, 0)),
            pl.BlockSpec((1, seq, LANES), lambda b, g, i: (b, 0, 0)),
            pl.BlockSpec((1, seq, LANES), lambda b, g, i: (b, 0, 0)),
        ] + cast_in_specs,
        out_specs=[pl.BlockSpec((1, t, w), lambda b, g, i: (b, i, g))] + cast_out_specs,
        out_shape=[jax.ShapeDtypeStruct((batch, seq, n_heads * head_dim), BF16)] + cast_shapes,
        scratch_shapes=[pltpu.VMEM((hp, head_dim, seq), BF16),
                        pltpu.VMEM((hp, t, head_dim + LANES), BF16),
                        pltpu.VMEM((hp, 1, t), F32), pltpu.VMEM((hp, 1, t), F32),
                        pltpu.VMEM((hp, head_dim, t), F32),
                        pltpu.VMEM((hp, t, t), F32), pltpu.VMEM((hp, t, t), F32),
                        pltpu.VMEM((hp, 1, t), F32), pltpu.VMEM((hp, 1, t), F32)],
        compiler_params=_params("parallel", "arbitrary", "arbitrary"),
        name="fox",
    )(qkv, qkv, qkv, ka, qa, *[c.src for c in casts])


def _gla_kernel(*refs, lives, n_heads, dk, dv):
    nc = len(lives)
    q_ref, k_ref, v_ref, r_ref, small_ref, wgu_ref, gb_ref, gain_ref = refs[:8]
    o_ref = refs[8 + nc]
    st_ref, la_ref, kd_ref, ea_ref = refs[9 + 2 * nc:]
    _run_casts(lives, refs[8:8 + nc], refs[9 + nc:9 + 2 * nc], 1)
    nb, tb, tn = q_ref.shape[1:]
    n_chunks = tb // CHUNK

    def cols(ref, b, rows, start, size):
        return ref[start // tn, b, rows, start % tn:start % tn + size]

    @pl.when(pl.program_id(0) == 0)
    def _():
        st_ref[...] = jnp.zeros(st_ref.shape, F32)

    for b in range(nb):
        pre = _dot(small_ref[b].astype(BF16), wgu_ref[...]) + gb_ref[...]
        la_ref[b] = _log_sigmoid(pre) * (1.0 / GLA_GATE_TAU)

    row = lax.broadcasted_iota(jnp.int32, (CHUNK, CHUNK), 0)
    col = lax.broadcasted_iota(jnp.int32, (CHUNK, CHUNK), 1)
    after = (col > row).astype(BF16)

    def prep(c, carry):
        rows = pl.ds(pl.multiple_of(c * CHUNK, CHUNK), CHUNK)
        for b in range(nb):
            la = la_ref[b, rows, :]
            rev = sum(_dot(after, piece) for piece in _split3(la))
            for tt in range(k_ref.shape[0]):
                ts = slice(tt * tn, (tt + 1) * tn)
                kd_ref[b, rows, ts] = (k_ref[tt, b, rows, :] * jnp.exp(rev[:, ts])).astype(BF16)
            ea_ref[c, b:b + 1, :] = jnp.exp(jnp.sum(la, axis=0, keepdims=True))
        return carry

    lax.fori_loop(0, n_chunks, prep, 0)

    def step(c, carry):
        rows = pl.ds(pl.multiple_of(c * CHUNK, CHUNK), CHUNK)
        ea = ea_ref[c]
        for b in range(nb):
            for h in range(n_heads):
                ks = slice(h * dk, (h + 1) * dk)
                vs = slice(h * dv, (h + 1) * dv)
                kv_t = lax.dot_general(cols(v_ref, b, rows, h * dv, dv), kd_ref[b, rows, ks],
                                       (((0,), (0,)), ((), ())), preferred_element_type=F32)
                st = st_ref[b * n_heads + h] * ea[b:b + 1, ks] + kv_t
                st_ref[b * n_heads + h] = st
                o = lax.dot_general(cols(q_ref, b, rows, h * dk, dk), st.astype(BF16),
                                    (((1,), (1,)), ((), ())), preferred_element_type=F32)
                on = _rms(o, gain_ref[...])
                r = cols(r_ref, b, rows, h * dv, dv)
                o_ref[b, rows, vs] = (on * (r * jax.nn.sigmoid(r))).astype(o_ref.dtype)
        return carry

    lax.fori_loop(0, n_chunks, step, 0)


def _gla(q, k, v, r, small, wgu, gate_bias, gain, *, batch, seq, n_heads, dk, dv, tb, casts=()):
    kw, vw = n_heads * dk, n_heads * dv
    tn = q.shape[-1]
    assert tn % dk == 0 and tn % dv == 0
    tiles = lambda a: pl.BlockSpec((a.shape[0], batch, tb, tn), lambda t: (0, 0, t, 0))
    cast_in_specs, cast_out_specs, cast_shapes = _cast_specs(casts)
    return pl.pallas_call(
        functools.partial(_gla_kernel, lives=tuple((c.live, c.transpose) for c in casts), n_heads=n_heads,
                          dk=dk, dv=dv),
        grid=(seq // tb,),
        in_specs=[
            tiles(q), tiles(k), tiles(v), tiles(r),
            pl.BlockSpec((batch, tb, LANES), lambda t: (0, t, 0)),
            pl.BlockSpec((LANES, kw), lambda t: (0, 0)),
            pl.BlockSpec((1, kw), lambda t: (0, 0)),
            pl.BlockSpec((1, dv), lambda t: (0, 0)),
        ] + cast_in_specs,
        out_specs=[pl.BlockSpec((batch, tb, vw), lambda t: (0, t, 0))] + cast_out_specs,
        out_shape=[jax.ShapeDtypeStruct((batch, seq, vw), BF16)] + cast_shapes,
        scratch_shapes=[
            pltpu.VMEM((batch * n_heads, dv, dk), F32),
            pltpu.VMEM((batch, tb, kw), F32),
            pltpu.VMEM((batch, tb, kw), BF16),
            pltpu.VMEM((tb // CHUNK, batch, kw), F32),
        ],
        compiler_params=_params("arbitrary"),
        name="gla",
    )(q, k, v, r, small, wgu, gate_bias, gain, *[c.src for c in casts])


def _merge_kernel(*refs, lives, n_parts, nf, tn):
    nc = len(lives)
    h_refs = refs[:n_parts]
    g_ref, yf_ref, yg_ref, wg_ref, bg_ref, wa_ref, wb_ref, wo_ref = refs[n_parts:n_parts + 8]
    n_in = n_parts + 8
    o_ref = refs[n_in + nc]
    _run_casts(lives, refs[n_in:n_in + nc], refs[n_in + nc + 1:], 1)
    d = o_ref.shape[1]

    def body(h_ref):
        h = h_ref[...]
        un = _rms(h, g_ref[...]).astype(BF16)
        yf, yg = yf_ref[...], yg_ref[...]
        acc = h
        for c0 in range(0, d, tn):
            cs, gs = slice(c0, c0 + tn), slice(d + c0, d + c0 + tn)
            g_fox = jax.nn.sigmoid(_dot(un, wg_ref[:, cs]) + bg_ref[:, cs])
            g_gla = jax.nn.sigmoid(_dot(un, wg_ref[:, gs]) + bg_ref[:, gs])
            merged = g_fox * _dot(yf, wa_ref[:, cs]) + g_gla * _dot(yg, wb_ref[:, cs])
            acc = acc + _dot(merged.astype(BF16), wo_ref[cs, :])
        o_ref[...] = acc

    _for_row_part(h_refs, nf, body)


def _merge(h_parts, g, y_fox, y_gla, w_gate, b_gate, w_a, w_b, w_o, *, tm, tn, casts=()):
    n, d = sum(a.shape[0] for a in h_parts), h_parts[0].shape[1]
    once = dict(pipeline_mode=pl.Buffered(1))
    h_specs, nf = _row_parts_specs(h_parts, tm, **(once if len(h_parts) == 2 else {}))
    whole = lambda a: pl.BlockSpec(a.shape, lambda i: (0, 0), **once)
    cast_in_specs, cast_out_specs, cast_shapes = _cast_specs(casts)
    return pl.pallas_call(
        functools.partial(_merge_kernel, lives=tuple((c.live, c.transpose) for c in casts),
                          n_parts=len(h_parts), nf=nf, tn=tn),
        grid=(n // tm,),
        in_specs=h_specs + [
            pl.BlockSpec((1, d), lambda i: (0, 0)),
            pl.BlockSpec((tm, y_fox.shape[1]), lambda i: (i, 0)),
            pl.BlockSpec((tm, y_gla.shape[1]), lambda i: (i, 0)),
            whole(w_gate), pl.BlockSpec((1, 2 * d), lambda i: (0, 0)),
            whole(w_a), whole(w_b), whole(w_o),
        ] + cast_in_specs,
        out_specs=[pl.BlockSpec((tm, d), lambda i: (i, 0))] + cast_out_specs,
        out_shape=[jax.ShapeDtypeStruct((n, d), F32)] + cast_shapes,
        compiler_params=_params("parallel"),
        name="merge",
    )(*h_parts, g, y_fox, y_gla, w_gate, b_gate, w_a, w_b, w_o, *[c.src for c in casts])


def _ple_kernel(h_ref, p_ref, gp_ref, gf_ref, wpg_ref, wpe_ref, o_ref, *, final):
    h = h_ref[...]
    hn = _rms(h, gp_ref[...]).astype(BF16)
    gate = jax.nn.sigmoid(_dot(hn, wpg_ref[...]))
    out = h + gate * _dot(p_ref[...].astype(BF16), wpe_ref[...])
    if final:
        out = _rms(out, gf_ref[...])
    o_ref[...] = out


def _ple(h, p, g_ple, g_final, w_gate, w_proj, *, tm, final):
    n, d = h.shape
    dp = p.shape[1]
    return pl.pallas_call(
        functools.partial(_ple_kernel, final=final),
        grid=(n // tm,),
        in_specs=[
            pl.BlockSpec((tm, d), lambda i: (i, 0)),
            pl.BlockSpec((tm, dp), lambda i: (i, 0)),
            pl.BlockSpec((1, d), lambda i: (0, 0)),
            pl.BlockSpec((1, d), lambda i: (0, 0)),
            pl.BlockSpec((d, d), lambda i: (0, 0)),
            pl.BlockSpec((dp, d), lambda i: (0, 0)),
        ],
        out_specs=pl.BlockSpec((tm, d), lambda i: (i, 0)),
        out_shape=jax.ShapeDtypeStruct((n, d), F32),
        compiler_params=_params("parallel"),
        name="ple",
    )(h, p, g_ple, g_final, w_gate, w_proj)


def _tile(n, want):
    t = min(n, want)
    assert n % t == 0, (n, want)
    return t


def kernel(x, p, ffn1_norm, ffn1_w_gate, ffn1_w_up, ffn1_w_down, mix_norm, w_in, fox_forget_bias, gla_gate_up, gla_gate_bias, gla_head_norm, w_branch_fox, w_branch_gla, w_merge_gate, b_merge_gate, w_out, ffn2_norm, ffn2_w_gate, ffn2_w_up, ffn2_w_down, ple_norm, w_ple_proj, w_ple_gate, final_norm):
    batch, seq, d = x.shape
    depth = p.shape[0]
    n = batch * seq
    fox_heads = fox_forget_bias.shape[-1]
    fox_dim = w_branch_fox.shape[1] // fox_heads
    fox_w = fox_heads * fox_dim
    rank, gla_kw = gla_gate_up.shape[1:]
    gla_dv = gla_head_norm.shape[-1]
    gla_vw = w_branch_gla.shape[1]
    gla_heads = gla_vw // gla_dv
    gla_dk = gla_kw // gla_heads
    assert fox_heads + rank <= LANES

    tn = 512
    assert w_in.shape[-1] == 3 * fox_w + fox_heads + 2 * gla_kw + 2 * gla_vw + rank
    assert (3 * fox_w) % tn == 0 and gla_kw % tn == 0 and gla_vw % tn == 0
    assert fox_heads % 8 == 0 and rank % 8 == 0
    lead = 3 * fox_w // tn
    kt, vt = gla_kw // tn, gla_vw // tn
    proj_outs = (_ProjOut(0, lead, 1.0, BF16),
                 _ProjOut(lead, kt, gla_dk ** -0.5, BF16),
                 _ProjOut(lead + kt, kt, 1.0, F32),
                 _ProjOut(lead + 2 * kt, vt, 1.0, BF16),
                 _ProjOut(lead + 2 * kt + vt, vt, 1.0, F32))
    proj_rows = (lead + 2 * kt + 2 * vt) * tn
    fl_rows = (3 * fox_w, fox_heads)
    gd_rows = (w_in.shape[-1] - rank, rank)

    row = lambda v: v.reshape(1, -1).astype(F32)
    h = x.reshape(n, d)
    tm_ffn = _tile(n, 1024)
    tf = _tile(ffn1_w_gate.shape[-1], 512)
    tm_proj = _tile(n, 512)
    tm_merge = _tile(n, 256)
    tm_ple = _tile(n, 512)
    t_fox = _tile(seq, 512)
    fox_hp = 4
    tb_gla = _tile(seq, 256)

    tf_first = _tile(ffn1_w_gate.shape[-1], 256)
    ffn_rest_grid = (n // tm_ffn - 1, ffn1_w_gate.shape[-1] // tf)
    gla_grid = (seq // tb_gla,)

    for i in range(depth):
        fbias = jnp.zeros((1, LANES), F32).at[0, :fox_heads].set(fox_forget_bias[i])
        wgu = jnp.zeros((LANES, gla_kw), F32).at[fox_heads:fox_heads + rank].set(gla_gate_up[i]).astype(BF16)

        w_in_t = jnp.swapaxes(w_in, 1, 2)

        h1, w1g, w1u, w1d = _ffn_first(h, row(ffn1_norm[i]), ffn1_w_gate[i], ffn1_w_up[i],
                                       ffn1_w_down[i], tm=tm_ffn, tf=tf_first)
        if ffn_rest_grid[0]:
            h_rest, w_proj = _ffn(h, row(ffn1_norm[i]), w1g, w1u, w1d, tm=tm_ffn, tf=tf,
                                  first_tile=1, casts=(_cast_compact_rows_t(
                                      w_in_t, i, ffn_rest_grid, 3 * fox_w, fox_heads, proj_rows),))
            h_parts = (h1, h_rest)
        else:
            h_parts, w_proj = (h1,), jnp.concatenate(
                [w_in[i][:, :3 * fox_w], w_in[i][:, 3 * fox_w + fox_heads:][:, :proj_rows - 3 * fox_w]],
                axis=1).astype(BF16)

        tiled = lambda a: a.reshape(a.shape[0], batch, seq, a.shape[-1])
        fox_qkv, gq, gk, gv, gr, small = _proj(
            h_parts, row(mix_norm[i]), w_proj, w_in_t, i, tm=tm_proj, tn=tn, outs=proj_outs,
            fl_rows=fl_rows, gd_rows=gd_rows)

        fox_ka, fox_qa = _fcum(small, fbias, batch=batch, seq=seq, n_heads=fox_heads,
                               head_dim=fox_dim)
        y_fox, = _fox(tiled(fox_qkv), fox_ka, fox_qa, batch=batch, seq=seq, n_heads=fox_heads,
                      head_dim=fox_dim, t=t_fox, hp=fox_hp)

        y_gla, wmg, wbf, wbg, wo, w2d = _gla(
            tiled(gq), tiled(gk), tiled(gv), tiled(gr), small.reshape(batch, seq, LANES), wgu,
            row(gla_gate_bias[i]),
            row(gla_head_norm[i]), batch=batch, seq=seq, n_heads=gla_heads, dk=gla_dk, dv=gla_dv,
            tb=tb_gla,
            casts=tuple(_cast_row_blocks(w, gla_grid) for w in (
                w_merge_gate[i], w_branch_fox[i], w_branch_gla[i], w_out[i], ffn2_w_down[i])))

        h, w2g, w2u, wpg, wpe = _merge(
            h_parts, row(mix_norm[i]), y_fox.reshape(n, fox_w), y_gla.reshape(n, gla_vw),
            wmg, row(b_merge_gate[i]), wbf, wbg, wo, tm=tm_merge, tn=tn,
            casts=tuple(_cast_row_blocks(w, (n // tm_merge,)) for w in (
                ffn2_w_gate[i], ffn2_w_up[i], w_ple_gate[i], w_ple_proj[i])))

        h, = _ffn(h, row(ffn2_norm[i]), w2g, w2u, w2d, tm=tm_ffn, tf=tf)

        h = _ple(h, p[i].reshape(n, -1), row(ple_norm[i]), row(final_norm), wpg, wpe,
                 tm=tm_ple, final=(i == depth - 1))

    return h.reshape(batch, seq, d)
```

```python
import functools
import math
from typing import Callable, NamedTuple

import jax
import jax.numpy as jnp
from jax import lax
from jax.experimental import pallas as pl
from jax.experimental.pallas import tpu as pltpu

EPS = 1e-6
CHUNK = 64
GLA_GATE_TAU = 16.0
LANES = 128
V7X_VMEM_LIMIT_BYTES = 62 * 1024 * 1024
LOG2E = math.log2(math.e)

F32 = jnp.float32
BF16 = jnp.bfloat16


def _rms(x, g):
    return x * lax.rsqrt(jnp.mean(x * x, axis=-1, keepdims=True) + EPS) * g


def _log_sigmoid(z):
    return jnp.minimum(z, 0.0) - jnp.log1p(jnp.exp(-jnp.abs(z)))


def _dot(a, b):
    return lax.dot_general(a, b, (((1,), (0,)), ((), ())), preferred_element_type=F32)


def _params(*sem):
    return pltpu.CompilerParams(dimension_semantics=sem,
                                vmem_limit_bytes=V7X_VMEM_LIMIT_BYTES)


class _Cast(NamedTuple):
    src: jax.Array
    src_spec: pl.BlockSpec
    dst_spec: pl.BlockSpec
    dst_shape: tuple
    live: Callable
    transpose: bool = False


def _flat_step(grid):
    def step(*ids):
        s = ids[0]
        for size, idx in zip(grid[1:], ids[1:]):
            s = s * size + idx
        return s
    return step


def _row_block_count(rows, n_steps, extra=lambda rb: True):
    ok = lambda k: rows % k == 0 and (rows // k) % 16 == 0 and extra(rows // k)
    return max(k for k in range(1, n_steps + 1) if ok(k))


def _cast_row_blocks(src, grid):
    r, c = src.shape
    step = _flat_step(grid)
    nb = _row_block_count(r, math.prod(grid))
    spec = pl.BlockSpec((r // nb, c), lambda *ids: (jnp.minimum(step(*ids), nb - 1), 0))
    return _Cast(src, spec, spec, src.shape, lambda *ids: step(*ids) < nb)


def _cast_compact_rows_t(src, layer, grid, lead_rows, skip, total_rows):
    d = src.shape[2]
    step = _flat_step(grid)
    nb = _row_block_count(total_rows, math.prod(grid),
                          lambda rb: lead_rows % rb == 0 and rb % LANES == 0)
    rb = total_rows // nb
    blk = lambda *ids: jnp.minimum(step(*ids), nb - 1)
    src_row = lambda *ids: pl.multiple_of(
        blk(*ids) * rb + jnp.where(blk(*ids) * rb < lead_rows, 0, skip), 8)
    return _Cast(src,
                 pl.BlockSpec((None, pl.Element(rb), pl.Element(d)),
                              lambda *ids: (layer, src_row(*ids), 0)),
                 pl.BlockSpec((d, rb), lambda *ids: (0, blk(*ids))),
                 (d, total_rows),
                 lambda *ids: step(*ids) < nb,
                 transpose=True)


def _cast_specs(casts):
    shapes = [jax.ShapeDtypeStruct(c.dst_shape, BF16) for c in casts]
    return [c.src_spec for c in casts], [c.dst_spec for c in casts], shapes


def _run_casts(lives, src_refs, dst_refs, n_axes):
    ids = [pl.program_id(a) for a in range(n_axes)]
    for (live, transpose), src, dst in zip(lives, src_refs, dst_refs):
        @pl.when(live(*ids))
        def _():
            blk = src[...]
            dst[...] = (blk.T if transpose else blk).astype(BF16)


def _ffn_step(x_ref, g_ref, o_ref, xn_ref, wg, wu, wd):
    @pl.when(pl.program_id(1) == 0)
    def _():
        x = x_ref[...]
        xn_ref[...] = _rms(x, g_ref[...]).astype(BF16)
        o_ref[...] = x

    xn = xn_ref[...]
    gate = _dot(xn, wg())
    up = _dot(xn, wu())
    hid = (0.5 * (gate * jax.nn.sigmoid(gate)) * up).astype(BF16)
    o_ref[...] += _dot(hid, wd())


def _ffn_kernel(*refs, lives):
    nc = len(lives)
    x_ref, g_ref, wg_ref, wu_ref, wd_ref = refs[:5]
    o_ref, xn_ref = refs[5 + nc], refs[-1]
    _ffn_step(x_ref, g_ref, o_ref, xn_ref,
              lambda: wg_ref[...], lambda: wu_ref[...], lambda: wd_ref[...])
    _run_casts(lives, refs[5:5 + nc], refs[6 + nc:6 + 2 * nc], 2)


def _ffn(x, g, wg, wu, wd, *, tm, tf, casts=(), first_tile=0):
    n, d = x.shape
    f = wg.shape[1]
    cast_in_specs, cast_out_specs, cast_shapes = _cast_specs(casts)
    return pl.pallas_call(
        functools.partial(_ffn_kernel, lives=tuple((c.live, c.transpose) for c in casts)),
        grid=(n // tm - first_tile, f // tf),
        in_specs=[
            pl.BlockSpec((tm, d), lambda i, j: (i + first_tile, 0)),
            pl.BlockSpec((1, d), lambda i, j: (0, 0)),
            pl.BlockSpec((d, tf), lambda i, j: (0, j)),
            pl.BlockSpec((d, tf), lambda i, j: (0, j)),
            pl.BlockSpec((tf, d), lambda i, j: (j, 0)),
        ] + cast_in_specs,
        out_specs=[pl.BlockSpec((tm, d), lambda i, j: (i, 0))] + cast_out_specs,
        out_shape=[jax.ShapeDtypeStruct((n - first_tile * tm, d), F32)] + cast_shapes,
        scratch_shapes=[pltpu.VMEM((tm, d), BF16)],
        compiler_params=_params("parallel", "arbitrary"),
        name="ffn",
    )(x, g, wg, wu, wd, *[c.src for c in casts])


def _ffn_first_kernel(x_ref, g_ref, wg_ref, wu_ref, wd_ref, o_ref, wgb_ref, wub_ref, wdb_ref,
                      xn_ref):
    def cast(src, dst):
        def thunk():
            w = src[...].astype(BF16)
            dst[...] = w
            return w
        return thunk

    _ffn_step(x_ref, g_ref, o_ref, xn_ref, cast(wg_ref, wgb_ref), cast(wu_ref, wub_ref),
              cast(wd_ref, wdb_ref))


def _ffn_first(x, g, wg, wu, wd, *, tm, tf):
    n, d = x.shape
    f = wg.shape[1]
    w_specs = [pl.BlockSpec((d, tf), lambda i, j: (0, j)), pl.BlockSpec((d, tf), lambda i, j: (0, j)),
               pl.BlockSpec((tf, d), lambda i, j: (j, 0))]
    return pl.pallas_call(
        _ffn_first_kernel,
        grid=(1, f // tf),
        in_specs=[pl.BlockSpec((tm, d), lambda i, j: (0, 0)),
                  pl.BlockSpec((1, d), lambda i, j: (0, 0))] + w_specs,
        out_specs=[pl.BlockSpec((tm, d), lambda i, j: (0, 0))] + w_specs,
        out_shape=[jax.ShapeDtypeStruct((tm, d), F32)]
        + [jax.ShapeDtypeStruct(w.shape, BF16) for w in (wg, wu, wd)],
        scratch_shapes=[pltpu.VMEM((tm, d), BF16)],
        compiler_params=_params("arbitrary", "arbitrary"),
        name="ffn_first",
    )(x, g, wg, wu, wd)


class _ProjOut(NamedTuple):
    first: int
    count: int
    scale: float
    dtype: type


def _dot_nt(a, b):
    return lax.dot_general(a, b, (((1,), (1,)), ((), ())), preferred_element_type=F32)


def _row_parts_specs(parts, tm, **kw):
    d = parts[0].shape[1]
    nf = parts[0].shape[0] // tm
    specs = [pl.BlockSpec((tm, d), lambda i, *_: (jnp.minimum(i, nf - 1), 0), **kw)]
    if len(parts) == 2:
        specs.append(pl.BlockSpec((tm, d), lambda i, *_: (jnp.maximum(i - nf, 0), 0)))
    return specs, nf


def _for_row_part(part_refs, nf, fn):
    if len(part_refs) == 1:
        return fn(part_refs[0])
    i = pl.program_id(0)
    pl.when(i < nf)(lambda: fn(part_refs[0]))
    pl.when(i >= nf)(lambda: fn(part_refs[1]))


def _proj_kernel(*refs, outs, tn, n_fl, rank, n_parts, nf):
    h_refs = refs[:n_parts]
    g_ref, w_ref, wfl_ref, wgd_ref = refs[n_parts:n_parts + 4]
    out_refs = refs[n_parts + 4:n_parts + 4 + len(outs)]
    small_ref = refs[n_parts + 4 + len(outs)]

    def body(h_ref):
        un = _rms(h_ref[...], g_ref[...]).astype(BF16)
        small_ref[...] = jnp.zeros(small_ref.shape, F32)
        small_ref[:, :n_fl] = _dot_nt(un, wfl_ref[...])
        small_ref[:, n_fl:n_fl + rank] = _dot_nt(un, wgd_ref[...])
        for o_ref, o in zip(out_refs, outs):
            for t in range(o.count):
                c0 = (o.first + t) * tn
                tile = _dot(un, w_ref[:, c0:c0 + tn])
                if o.scale != 1.0:
                    tile = tile * o.scale
                o_ref[t] = tile.astype(o_ref.dtype)

    _for_row_part(h_refs, nf, body)


def _proj(h_parts, g, w_main, w_in_t, layer, *, tm, tn, outs, fl_rows, gd_rows):
    n, d = sum(a.shape[0] for a in h_parts), h_parts[0].shape[1]
    assert fl_rows[1] + gd_rows[1] <= LANES
    once = dict(pipeline_mode=pl.Buffered(1))
    h_specs, nf = _row_parts_specs(h_parts, tm, **(once if len(h_parts) == 2 else {}))
    kern = functools.partial(_proj_kernel, outs=outs, tn=tn, n_fl=fl_rows[1], rank=gd_rows[1],
                             n_parts=len(h_parts), nf=nf)
    rows = lambda start, size: pl.BlockSpec(
        (None, pl.Element(size), pl.Element(d)), lambda i: (layer, start, 0), **once)
    return pl.pallas_call(
        kern,
        grid=(n // tm,),
        in_specs=h_specs + [
            pl.BlockSpec((1, d), lambda i: (0, 0)),
            pl.BlockSpec(w_main.shape, lambda i: (0, 0), **once),
            rows(*fl_rows),
            rows(*gd_rows),
        ],
        out_specs=[pl.BlockSpec((o.count, tm, tn), lambda i: (0, i, 0)) for o in outs]
        + [pl.BlockSpec((tm, LANES), lambda i: (i, 0))],
        out_shape=[jax.ShapeDtypeStruct((o.count, n, tn), o.dtype) for o in outs]
        + [jax.ShapeDtypeStruct((n, LANES), F32)],
        compiler_params=_params("parallel"),
        name="proj",
    )(*h_parts, g, w_main, w_in_t, w_in_t)


FOX_AUG_LANES = 8


def _split3(x):
    p0 = x.astype(BF16)
    r = x - p0.astype(F32)
    p1 = r.astype(BF16)
    return p0, p1, (r - p1.astype(F32)).astype(BF16)


def _fcum_kernel(small_ref, bias_ref, ka_ref, qa_ref, f_ref, *, n_heads, scale):
    s = small_ref.shape[0]
    f_ref[...] = _log_sigmoid(small_ref[...] + bias_ref[...])
    row = lax.broadcasted_iota(jnp.int32, (LANES, LANES), 0)
    col = lax.broadcasted_iota(jnp.int32, (LANES, LANES), 1)
    tril = (row >= col).astype(BF16)

    carry = jnp.zeros((1, LANES), F32)
    for r in range(s // LANES):
        rows = slice(r * LANES, (r + 1) * LANES)
        c = sum(_dot(tril, piece) for piece in _split3(f_ref[rows, :]))
        f_ref[rows, :] = c + carry
        carry = carry + c[LANES - 1:LANES, :]

    route = ((col // FOX_AUG_LANES == row) & (col % FOX_AUG_LANES < 6)
             & (row < n_heads)).astype(BF16)
    routed = sum(_dot(piece, route) for piece in _split3(f_ref[...] * scale))
    p0, p1, p2 = (p.astype(F32) for p in _split3(routed))
    lane = lax.broadcasted_iota(jnp.int32, (s, LANES), 1)
    j = lane % FOX_AUG_LANES
    live = lane < n_heads * FOX_AUG_LANES
    piece = jnp.where(j % 3 == 0, p0, jnp.where(j % 3 == 1, p1, p2))
    ka_ref[0] = jnp.where(live & (j < 3), -piece,
                          jnp.where(live & (j < 6), 1.0, 0.0)).astype(BF16)
    qa_ref[0] = jnp.where(live & (j < 3), 1.0,
                          jnp.where(live & (j < 6), piece, 0.0)).astype(BF16)


def _fcum(small, bias_row, *, batch, seq, n_heads, head_dim):
    assert n_heads * FOX_AUG_LANES <= LANES
    blk = pl.BlockSpec((1, seq, LANES), lambda b: (b, 0, 0))
    return pl.pallas_call(
        functools.partial(_fcum_kernel, n_heads=n_heads, scale=head_dim ** 0.5),
        grid=(batch,),
        in_specs=[
            pl.BlockSpec((seq, LANES), lambda b: (b, 0)),
            pl.BlockSpec((1, LANES), lambda b: (0, 0)),
        ],
        out_specs=[blk, blk],
        out_shape=[jax.ShapeDtypeStruct((batch, seq, LANES), BF16)] * 2,
        scratch_shapes=[pltpu.VMEM((seq, LANES), F32)],
        compiler_params=_params("parallel"),
        name="fcum",
    )(small, bias_row)


def _fox_kernel(*refs, lives, t, c2, hp, dh):
    nc = len(lives)
    q_ref, k_ref, v_ref, ka_ref, qa_ref = refs[:5]
    o_ref = refs[5 + nc]
    (vt_ref, qaug_ref, causal_ref, m_ref, l_ref, acc_ref, s0_ref, s1_ref, mc0_ref,
     mc1_ref) = refs[6 + 2 * nc:]
    _run_casts(lives, refs[5:5 + nc], refs[6 + nc:6 + 2 * nc], 3)
    g = pl.program_id(1)
    i = pl.program_id(2)
    s_refs = (s0_ref, s1_ref)
    mc_refs = (mc0_ref, mc1_ref)

    @pl.when(i == 0)
    def _():
        for h in range(hp):
            vt_ref[h] = v_ref[0, :, h * dh:(h + 1) * dh].astype(F32).T.astype(BF16)
        key = lax.broadcasted_iota(jnp.int32, (t, t), 0)
        qry = lax.broadcasted_iota(jnp.int32, (t, t), 1)
        causal_ref[...] = jnp.where(key <= qry, 0.0, -jnp.inf)

    q0 = pl.multiple_of(i * t, t)
    m_ref[...] = jnp.full(m_ref.shape, -jnp.inf, F32)
    l_ref[...] = jnp.zeros(l_ref.shape, F32)
    acc_ref[...] = jnp.zeros(acc_ref.shape, F32)

    qa_row = qa_ref[0, pl.ds(q0, 16), :].astype(F32)[0:1, :]
    lane = lax.broadcasted_iota(jnp.int32, (1, LANES), 1)
    for h in range(hp):
        mine = lane // FOX_AUG_LANES == g * hp + h
        qaug_ref[h, :, :dh] = q_ref[0, :, h * dh:(h + 1) * dh]
        qaug_ref[h, :, dh:] = jnp.broadcast_to(
            jnp.where(mine, qa_row, 0.0), (t, LANES)).astype(BF16)

    def scores(kk, slot, masked, h):
        k0 = pl.multiple_of(kk * t, t)
        hs = slice(h * dh, (h + 1) * dh)
        k_aug = jnp.concatenate([k_ref[0, pl.ds(k0, t), hs], ka_ref[0, pl.ds(k0, t), :]], axis=1)
        s = lax.dot_general(k_aug, qaug_ref[h], (((1,), (1,)), ((), ())),
                            preferred_element_type=F32) * c2
        if masked:
            s = s + causal_ref[...]
        s_refs[slot][h] = s
        mc_refs[slot][h] = jnp.max(s, axis=0, keepdims=True)

    def absorb(kk, slot, h):
        k0 = pl.multiple_of(kk * t, t)
        m_old = m_ref[h]
        m_new = jnp.maximum(m_old, mc_refs[slot][h])
        alpha = jnp.exp2(m_old - m_new)
        p = jnp.exp2(s_refs[slot][h] - m_new)
        l_ref[h] = alpha * l_ref[h] + jnp.sum(p, axis=0, keepdims=True)
        acc_ref[h] = alpha * acc_ref[h] + _dot(vt_ref[h, :, pl.ds(k0, t)], p.astype(BF16))
        m_ref[h] = m_new

    for h in range(hp):
        scores(i, 0, True, h)

    def body(kk, carry):
        prev = jnp.where(kk == 0, i, kk - 1)
        for par in range(2):
            @pl.when(kk % 2 == par)
            def _():
                for h in range(hp):
                    scores(kk, 1 - par, False, h)
                    absorb(prev, par, h)
        return carry

    lax.fori_loop(0, i, body, 0)
    last = jnp.where(i == 0, i, i - 1)
    for par in range(2):
        @pl.when(i % 2 == par)
        def _():
            for h in range(hp):
                absorb(last, par, h)
    for h in range(hp):
        o_ref[0, :, h * dh:(h + 1) * dh] = (acc_ref[h] / l_ref[h]).T.astype(o_ref.dtype)


def _fox_grid(batch, seq, n_heads, t, hp):
    return (batch, n_heads // hp, seq // t)


def _fox(qkv, ka, qa, *, batch, seq, n_heads, head_dim, t, hp, casts=()):
    assert head_dim == LANES and n_heads % hp == 0
    c2 = (head_dim ** -0.5) * LOG2E
    ng = n_heads // hp
    w = hp * head_dim
    assert qkv.shape == (3 * ng, batch, seq, w)
    cast_in_specs, cast_out_specs, cast_shapes = _cast_specs(casts)
    return pl.pallas_call(
        functools.partial(_fox_kernel, lives=tuple((c.live, c.transpose) for c in casts), t=t, c2=c2, hp=hp,
                          dh=head_dim),
        grid=_fox_grid(batch, seq, n_heads, t, hp),
        in_specs=[
            pl.BlockSpec((None, 1, t, w), lambda b, g, i: (g, b, i, 0)),
            pl.BlockSpec((None, 1, seq, w), lambda b, g, i: (ng + g, b, 0, 0)),
            pl.BlockSpec((None, 1, seq, w), lambda b, g, i: (2 * ng + g, b, 0, 0)),
            pl.BlockSpec((1, seq, LANES), lambda b, g, i: (b, 0, 0)),
            pl.BlockSpec((1, seq, LANES), lambda b, g, i: (b, 0, 0)),
        ] + cast_in_specs,
        out_specs=[pl.BlockSpec((1, t, w), lambda b, g, i: (b, i, g))] + cast_out_specs,
        out_shape=[jax.ShapeDtypeStruct((batch, seq, n_heads * head_dim), BF16)] + cast_shapes,
        scratch_shapes=[pltpu.VMEM((hp, head_dim, seq), BF16),
                        pltpu.VMEM((hp, t, head_dim + LANES), BF16),
                        pltpu.VMEM((t, t), F32),
                        pltpu.VMEM((hp, 1, t), F32), pltpu.VMEM((hp, 1, t), F32),
                        pltpu.VMEM((hp, head_dim, t), F32),
                        pltpu.VMEM((hp, t, t), F32), pltpu.VMEM((hp, t, t), F32),
                        pltpu.VMEM((hp, 1, t), F32), pltpu.VMEM((hp, 1, t), F32)],
        compiler_params=_params("parallel", "arbitrary", "arbitrary"),
        name="fox",
    )(qkv, qkv, qkv, ka, qa, *[c.src for c in casts])


def _gla_kernel(*refs, lives, n_heads, dk, dv):
    nc = len(lives)
    q_ref, k_ref, v_ref, r_ref, small_ref, wgu_ref, gb_ref, gain_ref = refs[:8]
    o_ref = refs[8 + nc]
    st_ref, la_ref, kd_ref, ea_ref = refs[9 + 2 * nc:]
    _run_casts(lives, refs[8:8 + nc], refs[9 + nc:9 + 2 * nc], 1)
    nb, tb, tn = q_ref.shape[1:]
    n_chunks = tb // CHUNK

    def cols(ref, b, rows, start, size):
        return ref[start // tn, b, rows, start % tn:start % tn + size]

    @pl.when(pl.program_id(0) == 0)
    def _():
        st_ref[...] = jnp.zeros(st_ref.shape, F32)

    for b in range(nb):
        pre = _dot(small_ref[b].astype(BF16), wgu_ref[...]) + gb_ref[...]
        la_ref[b] = _log_sigmoid(pre) * (1.0 / GLA_GATE_TAU)

    row = lax.broadcasted_iota(jnp.int32, (CHUNK, CHUNK), 0)
    col = lax.broadcasted_iota(jnp.int32, (CHUNK, CHUNK), 1)
    after = (col > row).astype(BF16)

    def prep(c, carry):
        rows = pl.ds(pl.multiple_of(c * CHUNK, CHUNK), CHUNK)
        for b in range(nb):
            la = la_ref[b, rows, :]
            rev = sum(_dot(after, piece) for piece in _split3(la))
            for tt in range(k_ref.shape[0]):
                ts = slice(tt * tn, (tt + 1) * tn)
                kd_ref[b, rows, ts] = (k_ref[tt, b, rows, :] * jnp.exp(rev[:, ts])).astype(BF16)
            ea_ref[c, b:b + 1, :] = jnp.exp(jnp.sum(la, axis=0, keepdims=True))
        return carry

    lax.fori_loop(0, n_chunks, prep, 0)

    def step(c, carry):
        rows = pl.ds(pl.multiple_of(c * CHUNK, CHUNK), CHUNK)
        ea = ea_ref[c]
        for b in range(nb):
            for h in range(n_heads):
                ks = slice(h * dk, (h + 1) * dk)
                vs = slice(h * dv, (h + 1) * dv)
                kv_t = lax.dot_general(cols(v_ref, b, rows, h * dv, dv), kd_ref[b, rows, ks],
                                       (((0,), (0,)), ((), ())), preferred_element_type=F32)
                st = st_ref[b * n_heads + h] * ea[b:b + 1, ks] + kv_t
                st_ref[b * n_heads + h] = st
                o = lax.dot_general(cols(q_ref, b, rows, h * dk, dk), st.astype(BF16),
                                    (((1,), (1,)), ((), ())), preferred_element_type=F32)
                on = _rms(o, gain_ref[...])
                r = cols(r_ref, b, rows, h * dv, dv)
                o_ref[b, rows, vs] = (on * (r * jax.nn.sigmoid(r))).astype(o_ref.dtype)
        return carry

    lax.fori_loop(0, n_chunks, step, 0)


def _gla(q, k, v, r, small, wgu, gate_bias, gain, *, batch, seq, n_heads, dk, dv, tb, casts=()):
    kw, vw = n_heads * dk, n_heads * dv
    tn = q.shape[-1]
    assert tn % dk == 0 and tn % dv == 0
    tiles = lambda a: pl.BlockSpec((a.shape[0], batch, tb, tn), lambda t: (0, 0, t, 0))
    cast_in_specs, cast_out_specs, cast_shapes = _cast_specs(casts)
    return pl.pallas_call(
        functools.partial(_gla_kernel, lives=tuple((c.live, c.transpose) for c in casts), n_heads=n_heads,
                          dk=dk, dv=dv),
        grid=(seq // tb,),
        in_specs=[
            tiles(q), tiles(k), tiles(v), tiles(r),
            pl.BlockSpec((batch, tb, LANES), lambda t: (0, t, 0)),
            pl.BlockSpec((LANES, kw), lambda t: (0, 0)),
            pl.BlockSpec((1, kw), lambda t: (0, 0)),
            pl.BlockSpec((1, dv), lambda t: (0, 0)),
        ] + cast_in_specs,
        out_specs=[pl.BlockSpec((batch, tb, vw), lambda t: (0, t, 0))] + cast_out_specs,
        out_shape=[jax.ShapeDtypeStruct((batch, seq, vw), BF16)] + cast_shapes,
        scratch_shapes=[
            pltpu.VMEM((batch * n_heads, dv, dk), F32),
            pltpu.VMEM((batch, tb, kw), F32),
            pltpu.VMEM((batch, tb, kw), BF16),
            pltpu.VMEM((tb // CHUNK, batch, kw), F32),
        ],
        compiler_params=_params("arbitrary"),
        name="gla",
    )(q, k, v, r, small, wgu, gate_bias, gain, *[c.src for c in casts])


def _merge_kernel(*refs, lives, n_parts, nf, tn):
    nc = len(lives)
    h_refs = refs[:n_parts]
    g_ref, yf_ref, yg_ref, wg_ref, bg_ref, wa_ref, wb_ref, wo_ref = refs[n_parts:n_parts + 8]
    n_in = n_parts + 8
    o_ref = refs[n_in + nc]
    _run_casts(lives, refs[n_in:n_in + nc], refs[n_in + nc + 1:], 1)
    d = o_ref.shape[1]

    def body(h_ref):
        h = h_ref[...]
        un = _rms(h, g_ref[...]).astype(BF16)
        yf, yg = yf_ref[...], yg_ref[...]
        acc = h
        for c0 in range(0, d, tn):
            cs, gs = slice(c0, c0 + tn), slice(d + c0, d + c0 + tn)
            g_fox = jax.nn.sigmoid(_dot(un, wg_ref[:, cs]) + bg_ref[:, cs])
            g_gla = jax.nn.sigmoid(_dot(un, wg_ref[:, gs]) + bg_ref[:, gs])
            merged = g_fox * _dot(yf, wa_ref[:, cs]) + g_gla * _dot(yg, wb_ref[:, cs])
            acc = acc + _dot(merged.astype(BF16), wo_ref[cs, :])
        o_ref[...] = acc

    _for_row_part(h_refs, nf, body)


def _merge(h_parts, g, y_fox, y_gla, w_gate, b_gate, w_a, w_b, w_o, *, tm, tn, casts=()):
    n, d = sum(a.shape[0] for a in h_parts), h_parts[0].shape[1]
    once = dict(pipeline_mode=pl.Buffered(1))
    h_specs, nf = _row_parts_specs(h_parts, tm, **(once if len(h_parts) == 2 else {}))
    whole = lambda a: pl.BlockSpec(a.shape, lambda i: (0, 0), **once)
    cast_in_specs, cast_out_specs, cast_shapes = _cast_specs(casts)
    return pl.pallas_call(
        functools.partial(_merge_kernel, lives=tuple((c.live, c.transpose) for c in casts),
                          n_parts=len(h_parts), nf=nf, tn=tn),
        grid=(n // tm,),
        in_specs=h_specs + [
            pl.BlockSpec((1, d), lambda i: (0, 0)),
            pl.BlockSpec((tm, y_fox.shape[1]), lambda i: (i, 0)),
            pl.BlockSpec((tm, y_gla.shape[1]), lambda i: (i, 0)),
            whole(w_gate), pl.BlockSpec((1, 2 * d), lambda i: (0, 0)),
            whole(w_a), whole(w_b), whole(w_o),
        ] + cast_in_specs,
        out_specs=[pl.BlockSpec((tm, d), lambda i: (i, 0))] + cast_out_specs,
        out_shape=[jax.ShapeDtypeStruct((n, d), F32)] + cast_shapes,
        compiler_params=_params("parallel"),
        name="merge",
    )(*h_parts, g, y_fox, y_gla, w_gate, b_gate, w_a, w_b, w_o, *[c.src for c in casts])


def _ple_kernel(h_ref, p_ref, gp_ref, gf_ref, wpg_ref, wpe_ref, o_ref, *, final):
    h = h_ref[...]
    hn = _rms(h, gp_ref[...]).astype(BF16)
    gate = jax.nn.sigmoid(_dot(hn, wpg_ref[...]))
    out = h + gate * _dot(p_ref[...].astype(BF16), wpe_ref[...])
    if final:
        out = _rms(out, gf_ref[...])
    o_ref[...] = out


def _ple(h, p, g_ple, g_final, w_gate, w_proj, *, tm, final):
    n, d = h.shape
    dp = p.shape[1]
    return pl.pallas_call(
        functools.partial(_ple_kernel, final=final),
        grid=(n // tm,),
        in_specs=[
            pl.BlockSpec((tm, d), lambda i: (i, 0)),
            pl.BlockSpec((tm, dp), lambda i: (i, 0)),
            pl.BlockSpec((1, d), lambda i: (0, 0)),
            pl.BlockSpec((1, d), lambda i: (0, 0)),
            pl.BlockSpec((d, d), lambda i: (0, 0)),
            pl.BlockSpec((dp, d), lambda i: (0, 0)),
        ],
        out_specs=pl.BlockSpec((tm, d), lambda i: (i, 0)),
        out_shape=jax.ShapeDtypeStruct((n, d), F32),
        compiler_params=_params("parallel"),
        name="ple",
    )(h, p, g_ple, g_final, w_gate, w_proj)


def _tile(n, want):
    t = min(n, want)
    assert n % t == 0, (n, want)
    return t


def kernel(x, p, ffn1_norm, ffn1_w_gate, ffn1_w_up, ffn1_w_down, mix_norm, w_in, fox_forget_bias, gla_gate_up, gla_gate_bias, gla_head_norm, w_branch_fox, w_branch_gla, w_merge_gate, b_merge_gate, w_out, ffn2_norm, ffn2_w_gate, ffn2_w_up, ffn2_w_down, ple_norm, w_ple_proj, w_ple_gate, final_norm):
    batch, seq, d = x.shape
    depth = p.shape[0]
    n = batch * seq
    fox_heads = fox_forget_bias.shape[-1]
    fox_dim = w_branch_fox.shape[1] // fox_heads
    fox_w = fox_heads * fox_dim
    rank, gla_kw = gla_gate_up.shape[1:]
    gla_dv = gla_head_norm.shape[-1]
    gla_vw = w_branch_gla.shape[1]
    gla_heads = gla_vw // gla_dv
    gla_dk = gla_kw // gla_heads
    assert fox_heads + rank <= LANES

    tn = 512
    assert w_in.shape[-1] == 3 * fox_w + fox_heads + 2 * gla_kw + 2 * gla_vw + rank
    assert (3 * fox_w) % tn == 0 and gla_kw % tn == 0 and gla_vw % tn == 0
    assert fox_heads % 8 == 0 and rank % 8 == 0
    lead = 3 * fox_w // tn
    kt, vt = gla_kw // tn, gla_vw // tn
    proj_outs = (_ProjOut(0, lead, 1.0, BF16),
                 _ProjOut(lead, kt, gla_dk ** -0.5, BF16),
                 _ProjOut(lead + kt, kt, 1.0, F32),
                 _ProjOut(lead + 2 * kt, vt, 1.0, BF16),
                 _ProjOut(lead + 2 * kt + vt, vt, 1.0, F32))
    proj_rows = (lead + 2 * kt + 2 * vt) * tn
    fl_rows = (3 * fox_w, fox_heads)
    gd_rows = (w_in.shape[-1] - rank, rank)

    row = lambda v: v.reshape(1, -1).astype(F32)
    h = x.reshape(n, d)
    tm_ffn = _tile(n, 1024)
    tf = _tile(ffn1_w_gate.shape[-1], 512)
    tm_proj = _tile(n, 512)
    tm_merge = _tile(n, 256)
    tm_ple = _tile(n, 512)
    t_fox = _tile(seq, 512)
    fox_hp = 4
    tb_gla = _tile(seq, 256)

    tf_first = _tile(ffn1_w_gate.shape[-1], 256)
    ffn_rest_grid = (n // tm_ffn - 1, ffn1_w_gate.shape[-1] // tf)
    gla_grid = (seq // tb_gla,)

    for i in range(depth):
        fbias = jnp.zeros((1, LANES), F32).at[0, :fox_heads].set(fox_forget_bias[i])
        wgu = jnp.zeros((LANES, gla_kw), F32).at[fox_heads:fox_heads + rank].set(gla_gate_up[i]).astype(BF16)

        w_in_t = jnp.swapaxes(w_in, 1, 2)

        h1, w1g, w1u, w1d = _ffn_first(h, row(ffn1_norm[i]), ffn1_w_gate[i], ffn1_w_up[i],
                                       ffn1_w_down[i], tm=tm_ffn, tf=tf_first)
        if ffn_rest_grid[0]:
            h_rest, w_proj = _ffn(h, row(ffn1_norm[i]), w1g, w1u, w1d, tm=tm_ffn, tf=tf,
                                  first_tile=1, casts=(_cast_compact_rows_t(
                                      w_in_t, i, ffn_rest_grid, 3 * fox_w, fox_heads, proj_rows),))
            h_parts = (h1, h_rest)
        else:
            h_parts, w_proj = (h1,), jnp.concatenate(
                [w_in[i][:, :3 * fox_w], w_in[i][:, 3 * fox_w + fox_heads:][:, :proj_rows - 3 * fox_w]],
                axis=1).astype(BF16)

        tiled = lambda a: a.reshape(a.shape[0], batch, seq, a.shape[-1])
        fox_qkv, gq, gk, gv, gr, small = _proj(
            h_parts, row(mix_norm[i]), w_proj, w_in_t, i, tm=tm_proj, tn=tn, outs=proj_outs,
            fl_rows=fl_rows, gd_rows=gd_rows)

        fox_ka, fox_qa = _fcum(small, fbias, batch=batch, seq=seq, n_heads=fox_heads,
                               head_dim=fox_dim)
        y_fox, = _fox(tiled(fox_qkv), fox_ka, fox_qa, batch=batch, seq=seq, n_heads=fox_heads,
                      head_dim=fox_dim, t=t_fox, hp=fox_hp)

        y_gla, wmg, wbf, wbg, wo, w2d = _gla(
            tiled(gq), tiled(gk), tiled(gv), tiled(gr), small.reshape(batch, seq, LANES), wgu,
            row(gla_gate_bias[i]),
            row(gla_head_norm[i]), batch=batch, seq=seq, n_heads=gla_heads, dk=gla_dk, dv=gla_dv,
            tb=tb_gla,
            casts=tuple(_cast_row_blocks(w, gla_grid) for w in (
                w_merge_gate[i], w_branch_fox[i], w_branch_gla[i], w_out[i], ffn2_w_down[i])))

        h, w2g, w2u, wpg, wpe = _merge(
            h_parts, row(mix_norm[i]), y_fox.reshape(n, fox_w), y_gla.reshape(n, gla_vw),
            wmg, row(b_merge_gate[i]), wbf, wbg, wo, tm=tm_merge, tn=tn,
            casts=tuple(_cast_row_blocks(w, (n // tm_merge,)) for w in (
                ffn2_w_gate[i], ffn2_w_up[i], w_ple_gate[i], w_ple_proj[i])))

        h, = _ffn(h, row(ffn2_norm[i]), w2g, w2u, w2d, tm=tm_ffn, tf=tf)

        h = _ple(h, p[i].reshape(n, -1), row(ple_norm[i]), row(final_norm), wpg, wpe,
                 tm=tm_ple, final=(i == depth - 1))

    return h.reshape(batch, seq, d)
```

```python
import functools
import math
from typing import Callable, NamedTuple

import jax
import jax.numpy as jnp
from jax import lax
from jax.experimental import pallas as pl
from jax.experimental.pallas import tpu as pltpu

EPS = 1e-6
CHUNK = 64
GLA_GATE_TAU = 16.0
LANES = 128
V7X_VMEM_LIMIT_BYTES = 62 * 1024 * 1024
LOG2E = math.log2(math.e)

F32 = jnp.float32
BF16 = jnp.bfloat16


def _rms(x, g):
    return x * lax.rsqrt(jnp.mean(x * x, axis=-1, keepdims=True) + EPS) * g


def _log_sigmoid(z):
    return jnp.minimum(z, 0.0) - jnp.log1p(jnp.exp(-jnp.abs(z)))


def _dot(a, b):
    return lax.dot_general(a, b, (((1,), (0,)), ((), ())), preferred_element_type=F32)


def _params(*sem):
    return pltpu.CompilerParams(dimension_semantics=sem,
                                vmem_limit_bytes=V7X_VMEM_LIMIT_BYTES)


class _Cast(NamedTuple):
    src: jax.Array
    src_spec: pl.BlockSpec
    dst_spec: pl.BlockSpec
    dst_shape: tuple
    live: Callable
    transpose: bool = False


def _flat_step(grid):
    def step(*ids):
        s = ids[0]
        for size, idx in zip(grid[1:], ids[1:]):
            s = s * size + idx
        return s
    return step


def _row_block_count(rows, n_steps, extra=lambda rb: True):
    ok = lambda k: rows % k == 0 and (rows // k) % 16 == 0 and extra(rows // k)
    return max(k for k in range(1, n_steps + 1) if ok(k))


def _cast_row_blocks(src, grid):
    r, c = src.shape
    step = _flat_step(grid)
    nb = _row_block_count(r, math.prod(grid))
    spec = pl.BlockSpec((r // nb, c), lambda *ids: (jnp.minimum(step(*ids), nb - 1), 0))
    return _Cast(src, spec, spec, src.shape, lambda *ids: step(*ids) < nb)


def _cast_compact_rows_t(src, layer, grid, lead_rows, skip, total_rows):
    d = src.shape[2]
    step = _flat_step(grid)
    nb = _row_block_count(total_rows, math.prod(grid),
                          lambda rb: lead_rows % rb == 0 and rb % LANES == 0)
    rb = total_rows // nb
    blk = lambda *ids: jnp.minimum(step(*ids), nb - 1)
    src_row = lambda *ids: pl.multiple_of(
        blk(*ids) * rb + jnp.where(blk(*ids) * rb < lead_rows, 0, skip), 8)
    return _Cast(src,
                 pl.BlockSpec((None, pl.Element(rb), pl.Element(d)),
                              lambda *ids: (layer, src_row(*ids), 0)),
                 pl.BlockSpec((d, rb), lambda *ids: (0, blk(*ids))),
                 (d, total_rows),
                 lambda *ids: step(*ids) < nb,
                 transpose=True)


def _cast_specs(casts):
    shapes = [jax.ShapeDtypeStruct(c.dst_shape, BF16) for c in casts]
    return [c.src_spec for c in casts], [c.dst_spec for c in casts], shapes


def _run_casts(lives, src_refs, dst_refs, n_axes):
    ids = [pl.program_id(a) for a in range(n_axes)]
    for (live, transpose), src, dst in zip(lives, src_refs, dst_refs):
        @pl.when(live(*ids))
        def _():
            blk = src[...]
            dst[...] = (blk.T if transpose else blk).astype(BF16)


def _ffn_step(x_ref, g_ref, o_ref, xn_ref, wg, wu, wd):
    @pl.when(pl.program_id(1) == 0)
    def _():
        x = x_ref[...]
        xn_ref[...] = _rms(x, g_ref[...]).astype(BF16)
        o_ref[...] = x

    xn = xn_ref[...]
    gate = _dot(xn, wg())
    up = _dot(xn, wu())
    hid = (0.5 * (gate * jax.nn.sigmoid(gate)) * up).astype(BF16)
    o_ref[...] += _dot(hid, wd())


def _ffn_kernel(*refs, lives):
    nc = len(lives)
    x_ref, g_ref, wg_ref, wu_ref, wd_ref = refs[:5]
    o_ref, xn_ref = refs[5 + nc], refs[-1]
    _ffn_step(x_ref, g_ref, o_ref, xn_ref,
              lambda: wg_ref[...], lambda: wu_ref[...], lambda: wd_ref[...])
    _run_casts(lives, refs[5:5 + nc], refs[6 + nc:6 + 2 * nc], 2)


def _ffn(x, g, wg, wu, wd, *, tm, tf, casts=(), first_tile=0):
    n, d = x.shape
    f = wg.shape[1]
    cast_in_specs, cast_out_specs, cast_shapes = _cast_specs(casts)
    return pl.pallas_call(
        functools.partial(_ffn_kernel, lives=tuple((c.live, c.transpose) for c in casts)),
        grid=(n // tm - first_tile, f // tf),
        in_specs=[
            pl.BlockSpec((tm, d), lambda i, j: (i + first_tile, 0)),
            pl.BlockSpec((1, d), lambda i, j: (0, 0)),
            pl.BlockSpec((d, tf), lambda i, j: (0, j)),
            pl.BlockSpec((d, tf), lambda i, j: (0, j)),
            pl.BlockSpec((tf, d), lambda i, j: (j, 0)),
        ] + cast_in_specs,
        out_specs=[pl.BlockSpec((tm, d), lambda i, j: (i, 0))] + cast_out_specs,
        out_shape=[jax.ShapeDtypeStruct((n - first_tile * tm, d), F32)] + cast_shapes,
        scratch_shapes=[pltpu.VMEM((tm, d), BF16)],
        compiler_params=_params("parallel", "arbitrary"),
        name="ffn",
    )(x, g, wg, wu, wd, *[c.src for c in casts])


def _ffn_first_kernel(x_ref, g_ref, wg_ref, wu_ref, wd_ref, o_ref, wgb_ref, wub_ref, wdb_ref,
                      xn_ref):
    def cast(src, dst):
        def thunk():
            w = src[...].astype(BF16)
            dst[...] = w
            return w
        return thunk

    _ffn_step(x_ref, g_ref, o_ref, xn_ref, cast(wg_ref, wgb_ref), cast(wu_ref, wub_ref),
              cast(wd_ref, wdb_ref))


def _ffn_first(x, g, wg, wu, wd, *, tm, tf):
    n, d = x.shape
    f = wg.shape[1]
    w_specs = [pl.BlockSpec((d, tf), lambda i, j: (0, j)), pl.BlockSpec((d, tf), lambda i, j: (0, j)),
               pl.BlockSpec((tf, d), lambda i, j: (j, 0))]
    return pl.pallas_call(
        _ffn_first_kernel,
        grid=(1, f // tf),
        in_specs=[pl.BlockSpec((tm, d), lambda i, j: (0, 0)),
                  pl.BlockSpec((1, d), lambda i, j: (0, 0))] + w_specs,
        out_specs=[pl.BlockSpec((tm, d), lambda i, j: (0, 0))] + w_specs,
        out_shape=[jax.ShapeDtypeStruct((tm, d), F32)]
        + [jax.ShapeDtypeStruct(w.shape, BF16) for w in (wg, wu, wd)],
        scratch_shapes=[pltpu.VMEM((tm, d), BF16)],
        compiler_params=_params("arbitrary", "arbitrary"),
        name="ffn_first",
    )(x, g, wg, wu, wd)


class _ProjOut(NamedTuple):
    first: int
    count: int
    scale: float
    dtype: type


def _dot_nt(a, b):
    return lax.dot_general(a, b, (((1,), (1,)), ((), ())), preferred_element_type=F32)


def _row_parts_specs(parts, tm, **kw):
    d = parts[0].shape[1]
    nf = parts[0].shape[0] // tm
    specs = [pl.BlockSpec((tm, d), lambda i, *_: (jnp.minimum(i, nf - 1), 0), **kw)]
    if len(parts) == 2:
        specs.append(pl.BlockSpec((tm, d), lambda i, *_: (jnp.maximum(i - nf, 0), 0)))
    return specs, nf


def _for_row_part(part_refs, nf, fn):
    if len(part_refs) == 1:
        return fn(part_refs[0])
    i = pl.program_id(0)
    pl.when(i < nf)(lambda: fn(part_refs[0]))
    pl.when(i >= nf)(lambda: fn(part_refs[1]))


def _proj_kernel(*refs, outs, tn, n_fl, rank, n_parts, nf):
    h_refs = refs[:n_parts]
    g_ref, w_ref, wfl_ref, wgd_ref = refs[n_parts:n_parts + 4]
    out_refs = refs[n_parts + 4:n_parts + 4 + len(outs)]
    small_ref = refs[n_parts + 4 + len(outs)]

    def body(h_ref):
        un = _rms(h_ref[...], g_ref[...]).astype(BF16)
        small_ref[...] = jnp.zeros(small_ref.shape, F32)
        small_ref[:, :n_fl] = _dot_nt(un, wfl_ref[...])
        small_ref[:, n_fl:n_fl + rank] = _dot_nt(un, wgd_ref[...])
        for o_ref, o in zip(out_refs, outs):
            for t in range(o.count):
                c0 = (o.first + t) * tn
                tile = _dot(un, w_ref[:, c0:c0 + tn])
                if o.scale != 1.0:
                    tile = tile * o.scale
                o_ref[t] = tile.astype(o_ref.dtype)

    _for_row_part(h_refs, nf, body)


def _proj(h_parts, g, w_main, w_in_t, layer, *, tm, tn, outs, fl_rows, gd_rows):
    n, d = sum(a.shape[0] for a in h_parts), h_parts[0].shape[1]
    assert fl_rows[1] + gd_rows[1] <= LANES
    once = dict(pipeline_mode=pl.Buffered(1))
    h_specs, nf = _row_parts_specs(h_parts, tm, **(once if len(h_parts) == 2 else {}))
    kern = functools.partial(_proj_kernel, outs=outs, tn=tn, n_fl=fl_rows[1], rank=gd_rows[1],
                             n_parts=len(h_parts), nf=nf)
    rows = lambda start, size: pl.BlockSpec(
        (None, pl.Element(size), pl.Element(d)), lambda i: (layer, start, 0), **once)
    return pl.pallas_call(
        kern,
        grid=(n // tm,),
        in_specs=h_specs + [
            pl.BlockSpec((1, d), lambda i: (0, 0)),
            pl.BlockSpec(w_main.shape, lambda i: (0, 0), **once),
            rows(*fl_rows),
            rows(*gd_rows),
        ],
        out_specs=[pl.BlockSpec((o.count, tm, tn), lambda i: (0, i, 0)) for o in outs]
        + [pl.BlockSpec((tm, LANES), lambda i: (i, 0))],
        out_shape=[jax.ShapeDtypeStruct((o.count, n, tn), o.dtype) for o in outs]
        + [jax.ShapeDtypeStruct((n, LANES), F32)],
        compiler_params=_params("parallel"),
        name="proj",
    )(*h_parts, g, w_main, w_in_t, w_in_t)


FOX_AUG_LANES = 8


def _split3(x):
    p0 = x.astype(BF16)
    r = x - p0.astype(F32)
    p1 = r.astype(BF16)
    return p0, p1, (r - p1.astype(F32)).astype(BF16)


def _fcum_kernel(small_ref, bias_ref, ka_ref, qa_ref, f_ref, *, n_heads, scale):
    s = small_ref.shape[0]
    f_ref[...] = _log_sigmoid(small_ref[...] + bias_ref[...])
    row = lax.broadcasted_iota(jnp.int32, (LANES, LANES), 0)
    col = lax.broadcasted_iota(jnp.int32, (LANES, LANES), 1)
    tril = (row >= col).astype(BF16)

    carry = jnp.zeros((1, LANES), F32)
    for r in range(s // LANES):
        rows = slice(r * LANES, (r + 1) * LANES)
        c = sum(_dot(tril, piece) for piece in _split3(f_ref[rows, :]))
        f_ref[rows, :] = c + carry
        carry = carry + c[LANES - 1:LANES, :]

    route = ((col // FOX_AUG_LANES == row) & (col % FOX_AUG_LANES < 6)
             & (row < n_heads)).astype(BF16)
    routed = sum(_dot(piece, route) for piece in _split3(f_ref[...] * scale))
    p0, p1, p2 = (p.astype(F32) for p in _split3(routed))
    lane = lax.broadcasted_iota(jnp.int32, (s, LANES), 1)
    j = lane % FOX_AUG_LANES
    live = lane < n_heads * FOX_AUG_LANES
    piece = jnp.where(j % 3 == 0, p0, jnp.where(j % 3 == 1, p1, p2))
    ka_ref[0] = jnp.where(live & (j < 3), -piece,
                          jnp.where(live & (j < 6), 1.0, 0.0)).astype(BF16)
    qa_ref[0] = jnp.where(live & (j < 3), 1.0,
                          jnp.where(live & (j < 6), piece, 0.0)).astype(BF16)


def _fcum(small, bias_row, *, batch, seq, n_heads, head_dim):
    assert n_heads * FOX_AUG_LANES <= LANES
    blk = pl.BlockSpec((1, seq, LANES), lambda b: (b, 0, 0))
    return pl.pallas_call(
        functools.partial(_fcum_kernel, n_heads=n_heads, scale=head_dim ** 0.5),
        grid=(batch,),
        in_specs=[
            pl.BlockSpec((seq, LANES), lambda b: (b, 0)),
            pl.BlockSpec((1, LANES), lambda b: (0, 0)),
        ],
        out_specs=[blk, blk],
        out_shape=[jax.ShapeDtypeStruct((batch, seq, LANES), BF16)] * 2,
        scratch_shapes=[pltpu.VMEM((seq, LANES), F32)],
        compiler_params=_params("parallel"),
        name="fcum",
    )(small, bias_row)


def _fox_kernel(*refs, lives, t, c2, hp, dh):
    nc = len(lives)
    q_ref, k_ref, v_ref, ka_ref, qa_ref = refs[:5]
    o_ref = refs[5 + nc]
    (vt_ref, qaug_ref, causal_ref, m_ref, l_ref, acc_ref, s0_ref, s1_ref, mc0_ref,
     mc1_ref) = refs[6 + 2 * nc:]
    _run_casts(lives, refs[5:5 + nc], refs[6 + nc:6 + 2 * nc], 3)
    g = pl.program_id(1)
    i = pl.program_id(2)
    s_refs = (s0_ref, s1_ref)
    mc_refs = (mc0_ref, mc1_ref)

    @pl.when(i == 0)
    def _():
        for h in range(hp):
            vt_ref[h] = v_ref[0, :, h * dh:(h + 1) * dh].astype(F32).T.astype(BF16)
        key = lax.broadcasted_iota(jnp.int32, (t, t), 0)
        qry = lax.broadcasted_iota(jnp.int32, (t, t), 1)
        causal_ref[...] = jnp.where(key <= qry, 0.0, -jnp.inf)

    q0 = pl.multiple_of(i * t, t)
    m_ref[...] = jnp.full(m_ref.shape, -jnp.inf, F32)
    l_ref[...] = jnp.zeros(l_ref.shape, F32)
    acc_ref[...] = jnp.zeros(acc_ref.shape, F32)

    qa_row = qa_ref[0, pl.ds(q0, 16), :].astype(F32)[0:1, :]
    lane = lax.broadcasted_iota(jnp.int32, (1, LANES), 1)
    for h in range(hp):
        mine = lane // FOX_AUG_LANES == g * hp + h
        qaug_ref[h, :, :dh] = q_ref[0, :, h * dh:(h + 1) * dh]
        qaug_ref[h, :, dh:] = jnp.broadcast_to(
            jnp.where(mine, qa_row, 0.0), (t, LANES)).astype(BF16)

    def scores(kk, slot, masked, h):
        k0 = pl.multiple_of(kk * t, t)
        hs = slice(h * dh, (h + 1) * dh)
        k_aug = jnp.concatenate([k_ref[0, pl.ds(k0, t), hs], ka_ref[0, pl.ds(k0, t), :]], axis=1)
        s = lax.dot_general(k_aug, qaug_ref[h], (((1,), (1,)), ((), ())),
                            preferred_element_type=F32) * c2
        if masked:
            s = s + causal_ref[...]
        s_refs[slot][h] = s
        mc_refs[slot][h] = jnp.max(s, axis=0, keepdims=True)

    def absorb(kk, slot, h):
        k0 = pl.multiple_of(kk * t, t)
        m_old = m_ref[h]
        m_new = jnp.maximum(m_old, mc_refs[slot][h])
        alpha = jnp.exp2(m_old - m_new)
        p = jnp.exp2(s_refs[slot][h] - m_new)
        l_ref[h] = alpha * l_ref[h] + jnp.sum(p, axis=0, keepdims=True)
        acc_ref[h] = alpha * acc_ref[h] + _dot(vt_ref[h, :, pl.ds(k0, t)], p.astype(BF16))
        m_ref[h] = m_new

    for h in range(hp):
        scores(i, 0, True, h)

    def body(kk, carry):
        prev = jnp.where(kk == 0, i, kk - 1)
        for par in range(2):
            @pl.when(kk % 2 == par)
            def _():
                for h in range(hp):
                    scores(kk, 1 - par, False, h)
                    absorb(prev, par, h)
        return carry

    lax.fori_loop(0, i, body, 0)
    last = jnp.where(i == 0, i, i - 1)
    for par in range(2):
        @pl.when(i % 2 == par)
        def _():
            for h in range(hp):
                absorb(last, par, h)
    for h in range(hp):
        o_ref[0, :, h * dh:(h + 1) * dh] = (acc_ref[h] / l_ref[h]).T.astype(o_ref.dtype)


def _fox_grid(batch, seq, n_heads, t, hp):
    return (batch, n_heads // hp, seq // t)


def _fox(qkv, ka, qa, *, batch, seq, n_heads, head_dim, t, hp, casts=()):
    assert head_dim == LANES and n_heads % hp == 0
    c2 = (head_dim ** -0.5) * LOG2E
    ng = n_heads // hp
    w = hp * head_dim
    assert qkv.shape == (3 * ng, batch, seq, w)
    cast_in_specs, cast_out_specs, cast_shapes = _cast_specs(casts)
    return pl.pallas_call(
        functools.partial(_fox_kernel, lives=tuple((c.live, c.transpose) for c in casts), t=t, c2=c2, hp=hp,
                          dh=head_dim),
        grid=_fox_grid(batch, seq, n_heads, t, hp),
        in_specs=[
            pl.BlockSpec((None, 1, t, w), lambda b, g, i: (g, b, i, 0)),
            pl.BlockSpec((None, 1, seq, w), lambda b, g, i: (ng + g, b, 0, 0)),
            pl.BlockSpec((None, 1, seq, w), lambda b, g, i: (2 * ng + g, b, 0, 0)),
            pl.BlockSpec((1, seq, LANES), lambda b, g, i: (b, 0, 0)),
            pl.BlockSpec((1, seq, LANES), lambda b, g, i: (b, 0, 0)),
        ] + cast_in_specs,
        out_specs=[pl.BlockSpec((1, t, w), lambda b, g, i: (b, i, g))] + cast_out_specs,
        out_shape=[jax.ShapeDtypeStruct((batch, seq, n_heads * head_dim), BF16)] + cast_shapes,
        scratch_shapes=[pltpu.VMEM((hp, head_dim, seq), BF16),
                        pltpu.VMEM((hp, t, head_dim + LANES), BF16),
                        pltpu.VMEM((t, t), F32),
                        pltpu.VMEM((hp, 1, t), F32), pltpu.VMEM((hp, 1, t), F32),
                        pltpu.VMEM((hp, head_dim, t), F32),
                        pltpu.VMEM((hp, t, t), F32), pltpu.VMEM((hp, t, t), F32),
                        pltpu.VMEM((hp, 1, t), F32), pltpu.VMEM((hp, 1, t), F32)],
        compiler_params=_params("parallel", "arbitrary", "arbitrary"),
        name="fox",
    )(qkv, qkv, qkv, ka, qa, *[c.src for c in casts])


def _gla_kernel(*refs, lives, n_heads, dk, dv):
    nc = len(lives)
    q_ref, k_ref, v_ref, r_ref, small_ref, wgu_ref, gb_ref, gain_ref = refs[:8]
    o_ref = refs[8 + nc]
    st_ref, la_ref, kd_ref, ea_ref = refs[9 + 2 * nc:]
    _run_casts(lives, refs[8:8 + nc], refs[9 + nc:9 + 2 * nc], 1)
    nb, tb, tn = q_ref.shape[1:]
    n_chunks = tb // CHUNK

    def cols(ref, b, rows, start, size):
        return ref[start // tn, b, rows, start % tn:start % tn + size]

    @pl.when(pl.program_id(0) == 0)
    def _():
        st_ref[...] = jnp.zeros(st_ref.shape, F32)

    for b in range(nb):
        pre = _dot(small_ref[b].astype(BF16), wgu_ref[...]) + gb_ref[...]
        la_ref[b] = _log_sigmoid(pre) * (1.0 / GLA_GATE_TAU)

    row = lax.broadcasted_iota(jnp.int32, (CHUNK, CHUNK), 0)
    col = lax.broadcasted_iota(jnp.int32, (CHUNK, CHUNK), 1)
    after = (col > row).astype(BF16)

    def prep(c, carry):
        rows = pl.ds(pl.multiple_of(c * CHUNK, CHUNK), CHUNK)
        for b in range(nb):
            la = la_ref[b, rows, :]
            rev = sum(_dot(after, piece) for piece in _split3(la))
            for tt in range(k_ref.shape[0]):
                ts = slice(tt * tn, (tt + 1) * tn)
                kd_ref[b, rows, ts] = (k_ref[tt, b, rows, :] * jnp.exp(rev[:, ts])).astype(BF16)
            ea_ref[c, b:b + 1, :] = jnp.exp(jnp.sum(la, axis=0, keepdims=True))
        return carry

    lax.fori_loop(0, n_chunks, prep, 0)

    def step(c, carry):
        rows = pl.ds(pl.multiple_of(c * CHUNK, CHUNK), CHUNK)
        ea = ea_ref[c]
        for b in range(nb):
            for h in range(n_heads):
                ks = slice(h * dk, (h + 1) * dk)
                vs = slice(h * dv, (h + 1) * dv)
                kv_t = lax.dot_general(cols(v_ref, b, rows, h * dv, dv), kd_ref[b, rows, ks],
                                       (((0,), (0,)), ((), ())), preferred_element_type=F32)
                st = st_ref[b * n_heads + h] * ea[b:b + 1, ks] + kv_t
                st_ref[b * n_heads + h] = st
                o = lax.dot_general(cols(q_ref, b, rows, h * dk, dk), st.astype(BF16),
                                    (((1,), (1,)), ((), ())), preferred_element_type=F32)
                on = _rms(o, gain_ref[...])
                r = cols(r_ref, b, rows, h * dv, dv)
                o_ref[b, rows, vs] = (on * (r * jax.nn.sigmoid(r))).astype(o_ref.dtype)
        return carry

    lax.fori_loop(0, n_chunks, step, 0)


def _gla(q, k, v, r, small, wgu, gate_bias, gain, *, batch, seq, n_heads, dk, dv, tb, casts=()):
    kw, vw = n_heads * dk, n_heads * dv
    tn = q.shape[-1]
    assert tn % dk == 0 and tn % dv == 0
    tiles = lambda a: pl.BlockSpec((a.shape[0], batch, tb, tn), lambda t: (0, 0, t, 0))
    cast_in_specs, cast_out_specs, cast_shapes = _cast_specs(casts)
    return pl.pallas_call(
        functools.partial(_gla_kernel, lives=tuple((c.live, c.transpose) for c in casts), n_heads=n_heads,
                          dk=dk, dv=dv),
        grid=(seq // tb,),
        in_specs=[
            tiles(q), tiles(k), tiles(v), tiles(r),
            pl.BlockSpec((batch, tb, LANES), lambda t: (0, t, 0)),
            pl.BlockSpec((LANES, kw), lambda t: (0, 0)),
            pl.BlockSpec((1, kw), lambda t: (0, 0)),
            pl.BlockSpec((1, dv), lambda t: (0, 0)),
        ] + cast_in_specs,
        out_specs=[pl.BlockSpec((batch, tb, vw), lambda t: (0, t, 0))] + cast_out_specs,
        out_shape=[jax.ShapeDtypeStruct((batch, seq, vw), BF16)] + cast_shapes,
        scratch_shapes=[
            pltpu.VMEM((batch * n_heads, dv, dk), F32),
            pltpu.VMEM((batch, tb, kw), F32),
            pltpu.VMEM((batch, tb, kw), BF16),
            pltpu.VMEM((tb // CHUNK, batch, kw), F32),
        ],
        compiler_params=_params("arbitrary"),
        name="gla",
    )(q, k, v, r, small, wgu, gate_bias, gain, *[c.src for c in casts])


def _merge_kernel(*refs, lives, n_parts, nf, tn):
    nc = len(lives)
    h_refs = refs[:n_parts]
    g_ref, yf_ref, yg_ref, wg_ref, bg_ref, wa_ref, wb_ref, wo_ref = refs[n_parts:n_parts + 8]
    n_in = n_parts + 8
    o_ref = refs[n_in + nc]
    _run_casts(lives, refs[n_in:n_in + nc], refs[n_in + nc + 1:], 1)
    d = o_ref.shape[1]

    def body(h_ref):
        h = h_ref[...]
        un = _rms(h, g_ref[...]).astype(BF16)
        yf, yg = yf_ref[...], yg_ref[...]
        o_ref[...] = h
        for c0 in range(0, d, tn):
            cs, gs = slice(c0, c0 + tn), slice(d + c0, d + c0 + tn)
            g_fox = jax.nn.sigmoid(_dot(un, wg_ref[:, cs]) + bg_ref[:, cs])
            g_gla = jax.nn.sigmoid(_dot(un, wg_ref[:, gs]) + bg_ref[:, gs])
            merged = g_fox * _dot(yf, wa_ref[:, cs]) + g_gla * _dot(yg, wb_ref[:, cs])
            o_ref[...] += _dot(merged.astype(BF16), wo_ref[cs, :])

    _for_row_part(h_refs, nf, body)


def _merge(h_parts, g, y_fox, y_gla, w_gate, b_gate, w_a, w_b, w_o, *, tm, tn, casts=()):
    n, d = sum(a.shape[0] for a in h_parts), h_parts[0].shape[1]
    once = dict(pipeline_mode=pl.Buffered(1))
    h_specs, nf = _row_parts_specs(h_parts, tm, **(once if len(h_parts) == 2 else {}))
    whole = lambda a: pl.BlockSpec(a.shape, lambda i: (0, 0), **once)
    cast_in_specs, cast_out_specs, cast_shapes = _cast_specs(casts)
    return pl.pallas_call(
        functools.partial(_merge_kernel, lives=tuple((c.live, c.transpose) for c in casts),
                          n_parts=len(h_parts), nf=nf, tn=tn),
        grid=(n // tm,),
        in_specs=h_specs + [
            pl.BlockSpec((1, d), lambda i: (0, 0)),
            pl.BlockSpec((tm, y_fox.shape[1]), lambda i: (i, 0)),
            pl.BlockSpec((tm, y_gla.shape[1]), lambda i: (i, 0)),
            whole(w_gate), pl.BlockSpec((1, 2 * d), lambda i: (0, 0)),
            whole(w_a), whole(w_b), whole(w_o),
        ] + cast_in_specs,
        out_specs=[pl.BlockSpec((tm, d), lambda i: (i, 0))] + cast_out_specs,
        out_shape=[jax.ShapeDtypeStruct((n, d), F32)] + cast_shapes,
        compiler_params=_params("parallel"),
        name="merge",
    )(*h_parts, g, y_fox, y_gla, w_gate, b_gate, w_a, w_b, w_o, *[c.src for c in casts])


def _ple_kernel(h_ref, p_ref, gp_ref, gf_ref, wpg_ref, wpe_ref, o_ref, *, final):
    h = h_ref[...]
    hn = _rms(h, gp_ref[...]).astype(BF16)
    gate = jax.nn.sigmoid(_dot(hn, wpg_ref[...]))
    out = h + gate * _dot(p_ref[...].astype(BF16), wpe_ref[...])
    if final:
        out = _rms(out, gf_ref[...])
    o_ref[...] = out


def _ple(h, p, g_ple, g_final, w_gate, w_proj, *, tm, final):
    n, d = h.shape
    dp = p.shape[1]
    return pl.pallas_call(
        functools.partial(_ple_kernel, final=final),
        grid=(n // tm,),
        in_specs=[
            pl.BlockSpec((tm, d), lambda i: (i, 0)),
            pl.BlockSpec((tm, dp), lambda i: (i, 0)),
            pl.BlockSpec((1, d), lambda i: (0, 0)),
            pl.BlockSpec((1, d), lambda i: (0, 0)),
            pl.BlockSpec((d, d), lambda i: (0, 0)),
            pl.BlockSpec((dp, d), lambda i: (0, 0)),
        ],
        out_specs=pl.BlockSpec((tm, d), lambda i: (i, 0)),
        out_shape=jax.ShapeDtypeStruct((n, d), F32),
        compiler_params=_params("parallel"),
        name="ple",
    )(h, p, g_ple, g_final, w_gate, w_proj)


def _tile(n, want):
    t = min(n, want)
    assert n % t == 0, (n, want)
    return t


def kernel(x, p, ffn1_norm, ffn1_w_gate, ffn1_w_up, ffn1_w_down, mix_norm, w_in, fox_forget_bias, gla_gate_up, gla_gate_bias, gla_head_norm, w_branch_fox, w_branch_gla, w_merge_gate, b_merge_gate, w_out, ffn2_norm, ffn2_w_gate, ffn2_w_up, ffn2_w_down, ple_norm, w_ple_proj, w_ple_gate, final_norm):
    batch, seq, d = x.shape
    depth = p.shape[0]
    n = batch * seq
    fox_heads = fox_forget_bias.shape[-1]
    fox_dim = w_branch_fox.shape[1] // fox_heads
    fox_w = fox_heads * fox_dim
    rank, gla_kw = gla_gate_up.shape[1:]
    gla_dv = gla_head_norm.shape[-1]
    gla_vw = w_branch_gla.shape[1]
    gla_heads = gla_vw // gla_dv
    gla_dk = gla_kw // gla_heads
    assert fox_heads + rank <= LANES

    tn = 512
    assert w_in.shape[-1] == 3 * fox_w + fox_heads + 2 * gla_kw + 2 * gla_vw + rank
    assert (3 * fox_w) % tn == 0 and gla_kw % tn == 0 and gla_vw % tn == 0
    assert fox_heads % 8 == 0 and rank % 8 == 0
    lead = 3 * fox_w // tn
    kt, vt = gla_kw // tn, gla_vw // tn
    proj_outs = (_ProjOut(0, lead, 1.0, BF16),
                 _ProjOut(lead, kt, gla_dk ** -0.5, BF16),
                 _ProjOut(lead + kt, kt, 1.0, F32),
                 _ProjOut(lead + 2 * kt, vt, 1.0, BF16),
                 _ProjOut(lead + 2 * kt + vt, vt, 1.0, F32))
    proj_rows = (lead + 2 * kt + 2 * vt) * tn
    fl_rows = (3 * fox_w, fox_heads)
    gd_rows = (w_in.shape[-1] - rank, rank)

    row = lambda v: v.reshape(1, -1).astype(F32)
    h = x.reshape(n, d)
    tm_ffn = _tile(n, 1024)
    tf = _tile(ffn1_w_gate.shape[-1], 512)
    tm_proj = _tile(n, 512)
    tm_merge = _tile(n, 256)
    tm_ple = _tile(n, 512)
    t_fox = _tile(seq, 512)
    fox_hp = 4
    tb_gla = _tile(seq, 256)

    tf_first = _tile(ffn1_w_gate.shape[-1], 256)
    ffn_rest_grid = (n // tm_ffn - 1, ffn1_w_gate.shape[-1] // tf)
    gla_grid = (seq // tb_gla,)

    for i in range(depth):
        fbias = jnp.zeros((1, LANES), F32).at[0, :fox_heads].set(fox_forget_bias[i])
        wgu = jnp.zeros((LANES, gla_kw), F32).at[fox_heads:fox_heads + rank].set(gla_gate_up[i]).astype(BF16)

        w_in_t = jnp.swapaxes(w_in, 1, 2)

        h1, w1g, w1u, w1d = _ffn_first(h, row(ffn1_norm[i]), ffn1_w_gate[i], ffn1_w_up[i],
                                       ffn1_w_down[i], tm=tm_ffn, tf=tf_first)
        if ffn_rest_grid[0]:
            h_rest, w_proj = _ffn(h, row(ffn1_norm[i]), w1g, w1u, w1d, tm=tm_ffn, tf=tf,
                                  first_tile=1, casts=(_cast_compact_rows_t(
                                      w_in_t, i, ffn_rest_grid, 3 * fox_w, fox_heads, proj_rows),))
            h_parts = (h1, h_rest)
        else:
            h_parts, w_proj = (h1,), jnp.concatenate(
                [w_in[i][:, :3 * fox_w], w_in[i][:, 3 * fox_w + fox_heads:][:, :proj_rows - 3 * fox_w]],
                axis=1).astype(BF16)

        tiled = lambda a: a.reshape(a.shape[0], batch, seq, a.shape[-1])
        fox_qkv, gq, gk, gv, gr, small = _proj(
            h_parts, row(mix_norm[i]), w_proj, w_in_t, i, tm=tm_proj, tn=tn, outs=proj_outs,
            fl_rows=fl_rows, gd_rows=gd_rows)

        fox_ka, fox_qa = _fcum(small, fbias, batch=batch, seq=seq, n_heads=fox_heads,
                               head_dim=fox_dim)
        y_fox, = _fox(tiled(fox_qkv), fox_ka, fox_qa, batch=batch, seq=seq, n_heads=fox_heads,
                      head_dim=fox_dim, t=t_fox, hp=fox_hp)

        y_gla, wmg, wbf, wbg, wo, w2d = _gla(
            tiled(gq), tiled(gk), tiled(gv), tiled(gr), small.reshape(batch, seq, LANES), wgu,
            row(gla_gate_bias[i]),
            row(gla_head_norm[i]), batch=batch, seq=seq, n_heads=gla_heads, dk=gla_dk, dv=gla_dv,
            tb=tb_gla,
            casts=tuple(_cast_row_blocks(w, gla_grid) for w in (
                w_merge_gate[i], w_branch_fox[i], w_branch_gla[i], w_out[i], ffn2_w_down[i])))

        h, w2g, w2u, wpg, wpe = _merge(
            h_parts, row(mix_norm[i]), y_fox.reshape(n, fox_w), y_gla.reshape(n, gla_vw),
            wmg, row(b_merge_gate[i]), wbf, wbg, wo, tm=tm_merge, tn=tn,
            casts=tuple(_cast_row_blocks(w, (n // tm_merge,)) for w in (
                ffn2_w_gate[i], ffn2_w_up[i], w_ple_gate[i], w_ple_proj[i])))

        h, = _ffn(h, row(ffn2_norm[i]), w2g, w2u, w2d, tm=tm_ffn, tf=tf)

        h = _ple(h, p[i].reshape(n, -1), row(ple_norm[i]), row(final_norm), wpg, wpe,
                 tm=tm_ple, final=(i == depth - 1))

    return h.reshape(batch, seq, d)
```
